```python
import jax, jax.numpy as jnp
from jax import lax
import numpy as np

D_MODEL = 2048
BATCH = 4
SEQ = 4096
DEPTH = 4

N_MIXERS = 2
HEAD_DIM = 128
N_HEADS = D_MODEL // HEAD_DIM
MOBA_BLOCK = 256
MOBA_TOPK = 3
MOBA_QCHUNK = 16
HGRN_EXPAND = 128
HGRN_HEADS = D_MODEL // HGRN_EXPAND
HGRN_CHUNK = 64
N_GROUPS = 4
EXPERTS_PER_GROUP = 8
N_EXPERTS = N_GROUPS * EXPERTS_PER_GROUP
TOP_E = 2
D_EXPERT = 512
MOE_BLOCK = 128
N_MOBA_LAYERS = (DEPTH + 1) // 2
N_HGRN_LAYERS = DEPTH // 2
DEEPNORM_ALPHA = (2 * DEPTH) ** 0.25
DEEPNORM_BETA = (8 * DEPTH) ** -0.25
LN_EPS = 1e-5
RMS_EPS = 1e-6

kernel_name = "hybrid_moba_hgrn2_hmoe_deepnorm"


def layer_norm(x, g, b):
    xf = x.astype(jnp.float32)
    mu = jnp.mean(xf, axis=-1, keepdims=True)
    var = jnp.mean(jnp.square(xf - mu), axis=-1, keepdims=True)
    return ((xf - mu) * lax.rsqrt(var + LN_EPS) * g.astype(jnp.float32) + b.astype(jnp.float32)).astype(x.dtype)


def moba_attention(x, w_in, w_o):
    B, S, D = x.shape
    q, k, v = jnp.split(x @ w_in, 3, axis=-1)
    heads = lambda t: t.reshape(B, S, N_HEADS, HEAD_DIM).transpose(0, 2, 1, 3)
    q = heads(q) * (HEAD_DIM ** -0.5)
    k, v = heads(k), heads(v)
    nb = -(-S // MOBA_BLOCK)
    pad = nb * MOBA_BLOCK - S
    kp = jnp.pad(k, ((0, 0), (0, 0), (0, pad), (0, 0))).reshape(B, N_HEADS, nb, MOBA_BLOCK, HEAD_DIM)
    vp = jnp.pad(v, ((0, 0), (0, 0), (0, pad), (0, 0))).reshape(B, N_HEADS, nb, MOBA_BLOCK, HEAD_DIM)
    k_mean = jnp.mean(kp.astype(jnp.float32), axis=3).astype(x.dtype)
    gate = jnp.einsum('bhsd,bhnd->bhsn', q, k_mean).astype(jnp.float32)
    q_blk = jnp.arange(S) // MOBA_BLOCK
    fully_past = jnp.arange(nb)[None, :] < q_blk[:, None]
    gate = jnp.where(fully_past, gate, -jnp.inf)
    topk = min(MOBA_TOPK, nb)
    gval, sel = lax.top_k(gate, topk)
    sel_valid = gval > -jnp.inf
    gather_blocks = jax.vmap(jax.vmap(lambda blocks, idx: blocks[idx]))
    n_sel = topk * MOBA_BLOCK

    def chunk(c):
        s0 = c * MOBA_QCHUNK
        qc = lax.dynamic_slice_in_dim(q, s0, MOBA_QCHUNK, axis=2)
        selc = lax.dynamic_slice_in_dim(sel, s0, MOBA_QCHUNK, axis=2)
        validc = lax.dynamic_slice_in_dim(sel_valid, s0, MOBA_QCHUNK, axis=2)
        kg = gather_blocks(kp, selc)
        vg = gather_blocks(vp, selc)
        own = s0 // MOBA_BLOCK
        k_own = lax.dynamic_index_in_dim(kp, own, axis=2, keepdims=False)
        v_own = lax.dynamic_index_in_dim(vp, own, axis=2, keepdims=False)
        s_sel = jnp.einsum('bhqd,bhqnkd->bhqnk', qc, kg).astype(jnp.float32)
        s_sel = jnp.where(validc[..., None], s_sel, -jnp.inf).reshape(B, N_HEADS, MOBA_QCHUNK, n_sel)
        s_own = jnp.einsum('bhqd,bhkd->bhqk', qc, k_own).astype(jnp.float32)
        qpos = s0 + jnp.arange(MOBA_QCHUNK)
        kpos = own * MOBA_BLOCK + jnp.arange(MOBA_BLOCK)
        s_own = jnp.where(kpos[None, :] <= qpos[:, None], s_own, -jnp.inf)
        p = jax.nn.softmax(jnp.concatenate([s_sel, s_own], axis=-1), axis=-1).astype(v.dtype)
        vg = vg.reshape(B, N_HEADS, MOBA_QCHUNK, n_sel, HEAD_DIM)
        return (jnp.einsum('bhqn,bhqnd->bhqd', p[..., :n_sel], vg)
                + jnp.einsum('bhqk,bhkd->bhqd', p[..., n_sel:], v_own))

    o = lax.map(chunk, jnp.arange(S // MOBA_QCHUNK))
    o = o.transpose(1, 2, 0, 3, 4).reshape(B, N_HEADS, S, HEAD_DIM)
    o = o.transpose(0, 2, 1, 3).reshape(B, S, D)
    return o @ w_o


def hgrn2(x, w_in, g_norm, lb, w_o):
    B, S, D = x.shape
    q, f, i, g = jnp.split(x @ w_in, 4, axis=-1)
    q = jax.nn.silu(q.astype(jnp.float32))
    lbf = lb.astype(jnp.float32)
    fg = lbf + (1.0 - lbf) * jax.nn.sigmoid(f.astype(jnp.float32))
    log_f = jnp.log(fg)
    k = 1.0 - fg
    n_chunks = S // HGRN_CHUNK
    to_chunks = lambda t: t.reshape(B, n_chunks, HGRN_CHUNK, HGRN_HEADS, HGRN_EXPAND).transpose(1, 0, 3, 2, 4)
    qc, kc, lc = to_chunks(q), to_chunks(k), to_chunks(log_f)
    ic = i.astype(jnp.float32).reshape(B, n_chunks, HGRN_CHUNK, HGRN_HEADS, HEAD_DIM).transpose(1, 0, 3, 2, 4)
    causal = (jnp.arange(HGRN_CHUNK)[:, None] >= jnp.arange(HGRN_CHUNK)[None, :])[..., None]

    def step(state, inp):
        qb, kb, lb_, ib = inp
        G = jnp.cumsum(lb_, axis=-2)
        o_inter = jnp.einsum('bhck,bhkv->bhcv', qb * jnp.exp(G), state)
        diff = G[:, :, :, None, :] - G[:, :, None, :, :]
        decay = jnp.where(causal, jnp.exp(jnp.where(causal, diff, 0.0)), 0.0)
        A = jnp.einsum('bhtk,bhsk,bhtsk->bhts', qb, kb, decay)
        o_intra = jnp.einsum('bhts,bhsv->bhtv', A, ib)
        G_last = G[:, :, -1:, :]
        state = (jnp.exp(G_last[:, :, 0, :])[..., None] * state
                 + jnp.einsum('bhsk,bhsv->bhkv', kb * jnp.exp(G_last - G), ib))
        return state, o_inter + o_intra

    s0 = jnp.zeros((B, HGRN_HEADS, HGRN_EXPAND, HEAD_DIM), jnp.float32)
    _, o = lax.scan(step, s0, (qc, kc, lc, ic))
    o = o.transpose(1, 0, 3, 2, 4).reshape(B, S, HGRN_HEADS, HEAD_DIM)
    o = o * lax.rsqrt(jnp.mean(jnp.square(o), axis=-1, keepdims=True) + RMS_EPS) * g_norm.astype(jnp.float32)
    o = o.reshape(B, S, D) * jax.nn.silu(g.astype(jnp.float32))
    return o.astype(x.dtype) @ w_o


def hierarchical_moe(x, w_rg, b_rg, w_re, b_re, w_gate, w_up, w_down):
    B, S, D = x.shape
    T = B * S
    xt = x.reshape(T, D)
    pg = jax.nn.softmax((xt @ w_rg).astype(jnp.float32) + b_rg.astype(jnp.float32), axis=-1)
    g_w, g_idx = lax.top_k(pg, 1)
    g_w, g_idx = g_w[:, 0], g_idx[:, 0]
    le_all = jnp.einsum('td,gde->tge', xt, w_re).astype(jnp.float32) + b_re.astype(jnp.float32)
    le = jnp.take_along_axis(le_all, g_idx[:, None, None], axis=1)[:, 0]
    e_w, e_idx = lax.top_k(jax.nn.softmax(le, axis=-1), TOP_E)
    e_w = e_w / jnp.sum(e_w, axis=-1, keepdims=True)
    weights = (g_w[:, None] * e_w).reshape(-1)
    eid = (g_idx[:, None] * EXPERTS_PER_GROUP + e_idx).reshape(-1).astype(jnp.int32)
    tok = jnp.repeat(jnp.arange(T, dtype=jnp.int32), TOP_E)
    n_assign = T * TOP_E
    order = jnp.argsort(eid)
    eid_s, tok_s, w_s = eid[order], tok[order], weights[order]
    counts = jnp.bincount(eid, length=N_EXPERTS)
    starts = jnp.cumsum(counts) - counts
    padded = (counts + MOE_BLOCK - 1) // MOE_BLOCK * MOE_BLOCK
    pends = jnp.cumsum(padded)
    pstarts = pends - padded
    dest = pstarts[eid_s] + (jnp.arange(n_assign) - starts[eid_s])
    n_blk = -(-n_assign // MOE_BLOCK) + N_EXPERTS
    n_rows = n_blk * MOE_BLOCK
    row_tok = jnp.full((n_rows,), T, jnp.int32).at[dest].set(tok_s)
    row_w = jnp.zeros((n_rows,), jnp.float32).at[dest].set(w_s)
    blk_expert = jnp.minimum(jnp.searchsorted(pends, jnp.arange(n_blk) * MOE_BLOCK, side='right'), N_EXPERTS - 1)
    x_pad = jnp.concatenate([xt, jnp.zeros((1, D), xt.dtype)], axis=0)

    def expert_block(args):
        toks, e = args
        xb = x_pad[toks]
        h = jax.nn.silu(xb @ w_gate[e]) * (xb @ w_up[e])
        return h @ w_down[e]

    y_rows = lax.map(expert_block, (row_tok.reshape(n_blk, MOE_BLOCK), blk_expert))
    y_rows = y_rows.reshape(n_rows, D) * row_w[:, None].astype(x.dtype)
    y = jnp.zeros((T + 1, D), x.dtype).at[row_tok].add(y_rows)
    return y[:T].reshape(B, S, D)


def setup_inputs(seed: int = 0) -> dict:
    key = jax.random.key(seed)
    ks = jax.random.split(key, 20)
    nrm = lambda k, shape, s: jax.random.normal(k, shape, jnp.float32) * s
    sd = D_MODEL ** -0.5
    col_scale_moba = jnp.concatenate([jnp.ones((2 * D_MODEL,)), jnp.full((D_MODEL,), DEEPNORM_BETA)]).astype(jnp.float32)
    col_scale_hgrn = jnp.concatenate([jnp.ones((2 * D_MODEL,)), jnp.full((D_MODEL,), DEEPNORM_BETA),
                                      jnp.ones((D_MODEL,))]).astype(jnp.float32)
    return {
        "x": nrm(ks[0], (BATCH, SEQ, D_MODEL), 1.0),
        "moba_w_in": nrm(ks[1], (N_MOBA_LAYERS, D_MODEL, 3 * D_MODEL), sd) * col_scale_moba,
        "moba_w_o": nrm(ks[2], (N_MOBA_LAYERS, D_MODEL, D_MODEL), sd * DEEPNORM_BETA),
        "hgrn_w_in": nrm(ks[3], (N_HGRN_LAYERS, D_MODEL, 4 * D_MODEL), sd) * col_scale_hgrn,
        "hgrn_g_norm": 1.0 + nrm(ks[4], (N_HGRN_LAYERS, HEAD_DIM), 0.02),
        "hgrn_lb_raw": nrm(ks[5], (N_HGRN_LAYERS, D_MODEL), 0.1),
        "hgrn_w_o": nrm(ks[6], (N_HGRN_LAYERS, D_MODEL, D_MODEL), sd * DEEPNORM_BETA),
        "ln_mix_g": 1.0 + nrm(ks[7], (DEPTH, D_MODEL), 0.02),
        "ln_mix_b": nrm(ks[8], (DEPTH, D_MODEL), 0.02),
        "moe_w_rg": nrm(ks[9], (DEPTH, D_MODEL, N_GROUPS), sd),
        "moe_b_rg": nrm(ks[10], (DEPTH, N_GROUPS), 0.01),
        "moe_w_re": nrm(ks[11], (DEPTH, N_GROUPS, D_MODEL, EXPERTS_PER_GROUP), sd),
        "moe_b_re": nrm(ks[12], (DEPTH, N_GROUPS, EXPERTS_PER_GROUP), 0.01),
        "moe_w_gate": nrm(ks[13], (DEPTH, N_EXPERTS, D_MODEL, D_EXPERT), sd),
        "moe_w_up": nrm(ks[14], (DEPTH, N_EXPERTS, D_MODEL, D_EXPERT), sd * DEEPNORM_BETA),
        "moe_w_down": nrm(ks[15], (DEPTH, N_EXPERTS, D_EXPERT, D_MODEL), D_EXPERT ** -0.5 * DEEPNORM_BETA),
        "ln_ffn_g": 1.0 + nrm(ks[16], (DEPTH, D_MODEL), 0.02),
        "ln_ffn_b": nrm(ks[17], (DEPTH, D_MODEL), 0.02),
    }


def reference(x, moba_w_in, moba_w_o, hgrn_w_in, hgrn_g_norm, hgrn_lb_raw, hgrn_w_o, ln_mix_g, ln_mix_b,
              moe_w_rg, moe_b_rg, moe_w_re, moe_b_re, moe_w_gate, moe_w_up, moe_w_down, ln_ffn_g, ln_ffn_b):
    lb_all = jnp.cumsum(jax.nn.softmax(hgrn_lb_raw.astype(jnp.float32), axis=0), axis=0)
    lb_all = lb_all - lb_all[0]
    for layer in range(DEPTH):
        j = layer // N_MIXERS
        if layer % N_MIXERS == 0:
            h = moba_attention(x, moba_w_in[j], moba_w_o[j])
        else:
            h = hgrn2(x, hgrn_w_in[j], hgrn_g_norm[j], lb_all[j], hgrn_w_o[j])
        x = layer_norm(DEEPNORM_ALPHA * x + h, ln_mix_g[layer], ln_mix_b[layer])
        m = hierarchical_moe(x, moe_w_rg[layer], moe_b_rg[layer], moe_w_re[layer], moe_b_re[layer],
                             moe_w_gate[layer], moe_w_up[layer], moe_w_down[layer])
        x = layer_norm(DEEPNORM_ALPHA * x + m, ln_ffn_g[layer], ln_ffn_b[layer])
    return x
```

```python
import functools

import numpy as np
import jax
import jax.numpy as jnp
from jax import lax
from jax.experimental import pallas as pl
from jax.experimental.pallas import tpu as pltpu

F32 = jnp.float32
BF16 = jnp.bfloat16
I32 = jnp.int32

LANES = 128
VMEM_LIMIT_BYTES = 56 << 20

HEAD_DIM = 128
MOBA_BLOCK = 256
MOBA_TOPK = 3
HGRN_CHUNK = 64
N_GROUPS = 4
EXPERTS_PER_GROUP = 8
N_EXPERTS = N_GROUPS * EXPERTS_PER_GROUP
TOP_E = 2
LN_EPS = 1e-5
RMS_EPS = 1e-6

MOE_ROWS = 128
NT_DIMS = (((1,), (1,)), ((), ()))
TN_DIMS = (((0,), (0,)), ((), ()))


def _params(*semantics):
    return pltpu.CompilerParams(dimension_semantics=semantics, vmem_limit_bytes=VMEM_LIMIT_BYTES)


def _sigmoid(x):
    return 1.0 / (1.0 + jnp.exp(-x))


def _layer_norm(y, g, b):
    mu = jnp.mean(y, axis=-1, keepdims=True)
    d = y - mu
    var = jnp.mean(d * d, axis=-1, keepdims=True)
    return d * lax.rsqrt(var + LN_EPS) * g + b


def _matmul_kernel(x_ref, w_ref, o_ref, *, n_scaled_blocks, scale):
    acc = jnp.dot(x_ref[...], w_ref[...], preferred_element_type=F32)
    if n_scaled_blocks:
        acc = acc * jnp.where(pl.program_id(1) < n_scaled_blocks, scale, 1.0).astype(F32)
    o_ref[...] = acc.astype(o_ref.dtype)


def _matmul(x, w, out_dtype, *, tm, tn, scaled_cols=0, scale=1.0):
    m, k = x.shape
    n = w.shape[1]
    tm, tn = min(tm, m), min(tn, n)
    assert m % tm == 0 and n % tn == 0 and scaled_cols % tn == 0
    kern = functools.partial(_matmul_kernel, n_scaled_blocks=scaled_cols // tn, scale=scale)
    return pl.pallas_call(
        kern,
        grid=(m // tm, n // tn),
        in_specs=[pl.BlockSpec((tm, k), lambda i, j: (i, 0)),
                  pl.BlockSpec((k, tn), lambda i, j: (0, j))],
        out_specs=pl.BlockSpec((tm, tn), lambda i, j: (i, j)),
        out_shape=jax.ShapeDtypeStruct((m, n), out_dtype),
        compiler_params=_params("parallel", "parallel"),
        name="proj_matmul",
    )(x, w)


def _proj_ln_kernel(o_ref, w_ref, x_ref, g_ref, b_ref, xo_ref, xb_ref, *, alpha):
    h = jnp.dot(o_ref[...], w_ref[...], preferred_element_type=F32)
    out = _layer_norm(alpha * x_ref[...] + h, g_ref[...], b_ref[...])
    xo_ref[...] = out
    xb_ref[...] = out.astype(BF16)


def _proj_ln(o, w, x, g, b, *, alpha, tm):
    t, d = x.shape
    tm = min(tm, t)
    assert t % tm == 0
    row = lambda i: (i, 0)
    fixed = lambda i: (0, 0)
    return pl.pallas_call(
        functools.partial(_proj_ln_kernel, alpha=alpha),
        grid=(t // tm,),
        in_specs=[pl.BlockSpec((tm, d), row), pl.BlockSpec((d, d), fixed), pl.BlockSpec((tm, d), row),
                  pl.BlockSpec((1, d), fixed), pl.BlockSpec((1, d), fixed)],
        out_specs=[pl.BlockSpec((tm, d), row), pl.BlockSpec((tm, d), row)],
        out_shape=[jax.ShapeDtypeStruct((t, d), F32), jax.ShapeDtypeStruct((t, d), BF16)],
        compiler_params=_params("parallel"),
        name="proj_ln",
    )(o, w, x, g.reshape(1, d), b.reshape(1, d))


def _moba_kernel(q_ref, k_ref, v_ref, o_ref, kmean_ref, *, n_blocks):
    blk = MOBA_BLOCK
    i = pl.program_id(2)

    @pl.when(i == 0)
    def _():
        kmean_ref[...] = jnp.zeros_like(kmean_ref)
        for j in range(n_blocks):
            kj = k_ref[pl.ds(j * blk, blk), :].astype(F32)
            kmean_ref[pl.ds(j, 1), :] = jnp.mean(kj, axis=0, keepdims=True)

    q = q_ref[...]
    gate = lax.dot_general(q.astype(F32), kmean_ref[...], NT_DIMS,
                           precision=lax.Precision.HIGHEST, preferred_element_type=F32)
    lane = lax.broadcasted_iota(I32, gate.shape, 1)
    rank = jnp.zeros(gate.shape, I32)
    n_past = jnp.full(gate.shape, i, I32)
    for jp in range(n_blocks):
        col = gate[:, jp:jp + 1]
        beats = (col > gate) | ((col == gate) & (jp < lane))
        rank = rank + jnp.where(beats & (jp < n_past), 1, 0)
    sel = jnp.where((lane < i) & (rank < MOBA_TOPK), 1.0, 0.0)

    def scores(j):
        kj = k_ref[pl.ds(pl.multiple_of(j * blk, blk), blk), :]
        return lax.dot_general(q, kj, NT_DIMS, preferred_element_type=F32)

    def values(j):
        return v_ref[pl.ds(pl.multiple_of(j * blk, blk), blk), :]

    s = scores(i)
    r_idx = lax.broadcasted_iota(I32, s.shape, 0)
    c_idx = lax.broadcasted_iota(I32, s.shape, 1)
    s = jnp.where(c_idx <= r_idx, s, -jnp.inf)
    m0 = jnp.max(s, axis=1, keepdims=True)
    p = jnp.exp(s - m0)
    l0 = jnp.sum(p, axis=1, keepdims=True)
    acc0 = jnp.dot(p.astype(BF16), values(i), preferred_element_type=F32)

    def body(j, carry):
        m, l, acc = carry
        chosen = jnp.sum(jnp.where(lane == j, sel, 0.0), axis=1, keepdims=True) > 0.0
        s = jnp.where(chosen, scores(j), -jnp.inf)
        m_new = jnp.maximum(m, jnp.max(s, axis=1, keepdims=True))
        a = jnp.exp(m - m_new)
        p = jnp.exp(s - m_new)
        l = a * l + jnp.sum(p, axis=1, keepdims=True)
        acc = a * acc + jnp.dot(p.astype(BF16), values(j), preferred_element_type=F32)
        return m_new, l, acc

    _, l, acc = lax.fori_loop(0, i, body, (m0, l0, acc0))
    o_ref[...] = (acc / l).astype(o_ref.dtype)


def _moba_attention(qkv, batch, seq):
    t, d3 = qkv.shape
    d = d3 // 3
    n_heads = d // HEAD_DIM
    n_blocks = seq // MOBA_BLOCK
    assert seq % MOBA_BLOCK == 0 and n_blocks <= LANES
    kv_spec = lambda off: pl.BlockSpec((seq, HEAD_DIM), lambda b, h, i: (b, off + h))
    return pl.pallas_call(
        functools.partial(_moba_kernel, n_blocks=n_blocks),
        grid=(batch, n_heads, n_blocks),
        in_specs=[pl.BlockSpec((MOBA_BLOCK, HEAD_DIM), lambda b, h, i: (b * n_blocks + i, h)),
                  kv_spec(n_heads), kv_spec(2 * n_heads)],
        out_specs=pl.BlockSpec((MOBA_BLOCK, HEAD_DIM), lambda b, h, i: (b * n_blocks + i, h)),
        out_shape=jax.ShapeDtypeStruct((t, d), BF16),
        scratch_shapes=[pltpu.VMEM((LANES, HEAD_DIM), F32)],
        compiler_params=_params("parallel", "parallel", "arbitrary"),
        name="moba_attention",
    )(qkv, qkv, qkv)


_HGRN_LEVELS = (32, 16, 8, 4, 2, 1)


def _hgrn_constants():
    c = HGRN_CHUNK
    idx = np.arange(c)
    t, s = idx[:, None], idx[None, :]
    sums = [(s <= t)]
    upper, pair = [], []
    for m in _HGRN_LEVELS:
        same = (t // (2 * m)) == (s // (2 * m))
        t_up, s_up = (t % (2 * m)) >= m, (s % (2 * m)) >= m
        up_rows = same & t_up & s_up & (s <= t)
        lo_rows = same & ~t_up & ~s_up & (s > t)
        sums.append(up_rows | lo_rows)
        upper.append(np.broadcast_to(t_up, (c, LANES)))
        pair.append(same & t_up & ~s_up)
    return (np.concatenate(sums, 0).astype(np.float32),
            np.stack(upper).astype(np.float32), np.stack(pair).astype(np.float32))


def _hgrn_kernel(q_ref, f_ref, i_ref, g_ref, lbraw_ref, gn_ref, sums_ref, upper_ref, pair_ref, o_ref,
                 *, layer, n_chunks):
    c = HGRN_CHUNK
    raw = lbraw_ref[...]
    e = jnp.exp(raw - jnp.max(raw, axis=0, keepdims=True))
    sm = e / jnp.sum(e, axis=0, keepdims=True)
    cum = sm[0:1, :]
    for l in range(1, layer + 1):
        cum = cum + sm[l:l + 1, :]
    lb = cum - sm[0:1, :]
    gnorm = gn_ref[...]
    eye = (lax.broadcasted_iota(I32, (c, c), 0) == lax.broadcasted_iota(I32, (c, c), 1))

    def body(ci, state_t):
        rows = pl.ds(pl.multiple_of(ci * c, c), c)
        qv, fv, iv, gv = q_ref[rows, :], f_ref[rows, :], i_ref[rows, :], g_ref[rows, :]
        qs = qv * _sigmoid(qv)
        fg = lb + (1.0 - lb) * _sigmoid(fv)
        log_f = jnp.log(fg)
        kk = 1.0 - fg
        sums = jnp.dot(sums_ref[...], log_f, precision=lax.Precision.HIGHEST,
                       preferred_element_type=F32)
        gcum = sums[0:c, :]
        iv_b = iv.astype(BF16)
        o = lax.dot_general((qs * jnp.exp(gcum)).astype(BF16), state_t.astype(BF16), NT_DIMS,
                            preferred_element_type=F32)
        a = jnp.where(eye, jnp.sum(qs * kk, axis=1, keepdims=True), 0.0)
        for li in range(len(_HGRN_LEVELS)):
            dec = jnp.exp(sums[(li + 1) * c:(li + 2) * c, :])
            is_up = upper_ref[li] > 0.0
            qe = jnp.where(is_up, qs * dec, 0.0).astype(BF16)
            ke = jnp.where(is_up, 0.0, kk * dec).astype(BF16)
            prod = lax.dot_general(qe, ke, NT_DIMS, preferred_element_type=F32)
            a = a + jnp.where(pair_ref[li] > 0.0, prod, 0.0)
        o = o + jnp.dot(a.astype(BF16), iv_b, preferred_element_type=F32)
        g_last = gcum[c - 1:c, :]
        k_dec = (kk * jnp.exp(g_last - gcum)).astype(BF16)
        state_t = jnp.exp(g_last) * state_t + lax.dot_general(iv_b, k_dec, TN_DIMS,
                                                               preferred_element_type=F32)
        o = o * lax.rsqrt(jnp.mean(o * o, axis=1, keepdims=True) + RMS_EPS) * gnorm
        o_ref[rows, :] = (o * (gv * _sigmoid(gv))).astype(o_ref.dtype)
        return state_t

    lax.fori_loop(0, n_chunks, body, jnp.zeros((HEAD_DIM, HEAD_DIM), F32))


def _hgrn_mix(proj, lb_raw, g_norm, *, layer, batch, seq):
    t, d4 = proj.shape
    d = d4 // 4
    n_heads = d // HEAD_DIM
    assert seq % HGRN_CHUNK == 0
    sums, upper, pair = _hgrn_constants()
    n_layers = lb_raw.shape[0]
    part = lambda off: pl.BlockSpec((seq, HEAD_DIM), lambda b, h: (b, off + h))
    whole = lambda a: pl.BlockSpec(a.shape, lambda b, h: (0,) * a.ndim)
    return pl.pallas_call(
        functools.partial(_hgrn_kernel, layer=layer, n_chunks=seq // HGRN_CHUNK),
        grid=(batch, n_heads),
        in_specs=[part(0), part(n_heads), part(2 * n_heads), part(3 * n_heads),
                  pl.BlockSpec((n_layers, HEAD_DIM), lambda b, h: (0, h)),
                  pl.BlockSpec((1, HEAD_DIM), lambda b, h: (0, 0)),
                  whole(sums), whole(upper), whole(pair)],
        out_specs=pl.BlockSpec((seq, HEAD_DIM), lambda b, h: (b, h)),
        out_shape=jax.ShapeDtypeStruct((t, d), BF16),
        compiler_params=_params("parallel", "parallel"),
        name="hgrn_recurrence",
    )(proj, proj, proj, proj, lb_raw, g_norm.reshape(1, HEAD_DIM),
      jnp.asarray(sums), jnp.asarray(upper), jnp.asarray(pair))


def _router_kernel(x_ref, w_ref, b_ref, ids_ref, wts_ref):
    g, eg = N_GROUPS, EXPERTS_PER_GROUP
    logits = jnp.dot(x_ref[...], w_ref[...], precision=lax.Precision.HIGHEST,
                     preferred_element_type=F32) + b_ref[...]
    lane = lax.broadcasted_iota(I32, logits.shape, 1)

    def softmax_over(mask):
        z = jnp.where(mask, logits, -jnp.inf)
        ez = jnp.exp(z - jnp.max(z, axis=1, keepdims=True))
        return ez / jnp.sum(ez, axis=1, keepdims=True)

    def top1(p, mask):
        best = jnp.max(jnp.where(mask, p, -1.0), axis=1, keepdims=True)
        where = jnp.min(jnp.where(mask & (p == best), lane, LANES), axis=1, keepdims=True)
        return best, where

    is_group = lane < g
    g_w, g_idx = top1(softmax_over(is_group), is_group)
    lo = g + g_idx * eg
    in_group = (lane >= lo) & (lane < lo + eg)
    pe = softmax_over(in_group)
    w1, i1 = top1(pe, in_group)
    w2, i2 = top1(pe, in_group & (lane != i1))
    den = w1 + w2
    wt1, wt2 = g_w * (w1 / den), g_w * (w2 / den)
    ids_ref[...] = jnp.where(lane == 0, i1 - g, jnp.where(lane == 1, i2 - g, 0))
    wts_ref[...] = jnp.where(lane == 0, wt1, jnp.where(lane == 1, wt2, 0.0))


def _router(x, w_r, b_r, *, tm):
    t, d = x.shape
    tm = min(tm, t)
    assert t % tm == 0
    row = lambda i: (i, 0)
    fixed = lambda i: (0, 0)
    return pl.pallas_call(
        _router_kernel,
        grid=(t // tm,),
        in_specs=[pl.BlockSpec((tm, d), row), pl.BlockSpec((d, LANES), fixed), pl.BlockSpec((1, LANES), fixed)],
        out_specs=[pl.BlockSpec((tm, LANES), row), pl.BlockSpec((tm, LANES), row)],
        out_shape=[jax.ShapeDtypeStruct((t, LANES), I32), jax.ShapeDtypeStruct((t, LANES), F32)],
        compiler_params=_params("parallel"),
        name="moe_router",
    )(x, w_r, b_r)


def _expert_kernel(blk_expert_ref, row_tok_ref, n_used_ref, x_hbm, wg_ref, wu_ref, wd_ref, y_ref, xbuf, sem):
    del blk_expert_ref
    i = pl.program_id(0)
    rows = xbuf.shape[0]

    @pl.when(i < n_used_ref[0])
    def _():
        base = i * rows

        def issue(r, carry):
            tok = row_tok_ref[base + r]
            pltpu.make_async_copy(x_hbm.at[pl.ds(tok, 1), :], xbuf.at[pl.ds(r, 1), :], sem).start()
            return carry

        lax.fori_loop(0, rows, issue, 0)
        pltpu.make_async_copy(x_hbm.at[pl.ds(0, rows), :], xbuf, sem).wait()
        xb = xbuf[...].astype(BF16)
        hg = jnp.dot(xb, wg_ref[...], preferred_element_type=F32)
        hu = jnp.dot(xb, wu_ref[...], preferred_element_type=F32)
        h = (hg * _sigmoid(hg)) * hu
        y_ref[...] = jnp.dot(h.astype(BF16), wd_ref[...], preferred_element_type=F32)

    @pl.when(i >= n_used_ref[0])
    def _():
        y_ref[...] = jnp.zeros_like(y_ref)


def _expert_mlp(x, blk_expert, row_tok, n_used, w_gate, w_up, w_down):
    t, d = x.shape
    n_e, _, f = w_gate.shape
    n_blk = blk_expert.shape[0]
    rows = MOE_ROWS
    grid_spec = pltpu.PrefetchScalarGridSpec(
        num_scalar_prefetch=3,
        grid=(n_blk,),
        in_specs=[pl.BlockSpec(memory_space=pl.ANY),
                  pl.BlockSpec((None, d, f), lambda i, be, rt, nu: (be[i], 0, 0)),
                  pl.BlockSpec((None, d, f), lambda i, be, rt, nu: (be[i], 0, 0)),
                  pl.BlockSpec((None, f, d), lambda i, be, rt, nu: (be[i], 0, 0))],
        out_specs=pl.BlockSpec((rows, d), lambda i, be, rt, nu: (i, 0)),
        scratch_shapes=[pltpu.VMEM((rows, d), F32), pltpu.SemaphoreType.DMA(())],
    )
    return pl.pallas_call(
        _expert_kernel,
        grid_spec=grid_spec,
        out_shape=jax.ShapeDtypeStruct((n_blk * rows, d), F32),
        compiler_params=_params("arbitrary"),
        name="moe_experts",
    )(blk_expert, row_tok, n_used, x, w_gate, w_up, w_down)


def _combine_ln_kernel(pos_ref, y_hbm, x_ref, wts_ref, g_ref, b_ref, xo_ref, xb_ref, ybuf, sem, *, alpha):
    i = pl.program_id(0)
    tm = x_ref.shape[0]
    base = i * tm

    def issue(r, carry):
        for k in range(TOP_E):
            p = pos_ref[TOP_E * (base + r) + k]
            pltpu.make_async_copy(y_hbm.at[pl.ds(p, 1), :], ybuf.at[k, pl.ds(r, 1), :], sem).start()
        return carry

    lax.fori_loop(0, tm, issue, 0)
    for k in range(TOP_E):
        pltpu.make_async_copy(y_hbm.at[pl.ds(0, tm), :], ybuf.at[k], sem).wait()
    wts = wts_ref[...]
    m = wts[:, 0:1] * ybuf[0] + wts[:, 1:2] * ybuf[1]
    out = _layer_norm(alpha * x_ref[...] + m, g_ref[...], b_ref[...])
    xo_ref[...] = out
    xb_ref[...] = out.astype(BF16)


def _combine_ln(pos, y_rows, x, wts, g, b, *, alpha, tm):
    t, d = x.shape
    tm = min(tm, t)
    assert t % tm == 0
    row = lambda i, p: (i, 0)
    fixed = lambda i, p: (0, 0)
    grid_spec = pltpu.PrefetchScalarGridSpec(
        num_scalar_prefetch=1,
        grid=(t // tm,),
        in_specs=[pl.BlockSpec(memory_space=pl.ANY), pl.BlockSpec((tm, d), row), pl.BlockSpec((tm, LANES), row),
                  pl.BlockSpec((1, d), fixed), pl.BlockSpec((1, d), fixed)],
        out_specs=[pl.BlockSpec((tm, d), row), pl.BlockSpec((tm, d), row)],
        scratch_shapes=[pltpu.VMEM((TOP_E, tm, d), F32), pltpu.SemaphoreType.DMA(())],
    )
    return pl.pallas_call(
        functools.partial(_combine_ln_kernel, alpha=alpha),
        grid_spec=grid_spec,
        out_shape=[jax.ShapeDtypeStruct((t, d), F32), jax.ShapeDtypeStruct((t, d), BF16)],
        compiler_params=_params("arbitrary"),
        name="moe_combine_ln",
    )(pos, y_rows, x, wts, g.reshape(1, d), b.reshape(1, d))


def _dispatch_plan(ids, n_tokens):
    rows = MOE_ROWS
    n_assign = n_tokens * TOP_E
    eid = ids[:, :TOP_E].reshape(n_assign)
    onehot = (eid[:, None] == jnp.arange(N_EXPERTS, dtype=I32)[None, :]).astype(I32)
    csum = jnp.cumsum(onehot, axis=0)
    rank = jnp.sum(onehot * csum, axis=1) - 1
    counts = csum[-1]
    padded = (counts + rows - 1) // rows * rows
    pends = jnp.cumsum(padded)
    pstarts = pends - padded
    pos = (pstarts[eid] + rank).astype(I32)
    n_blk = n_assign // rows + N_EXPERTS
    tok = jnp.arange(n_assign, dtype=I32) // TOP_E
    row_tok = jnp.zeros((n_blk * rows,), I32).at[pos].set(tok)
    blk_expert = jnp.minimum(jnp.searchsorted(pends, jnp.arange(n_blk, dtype=I32) * rows, side='right'),
                             N_EXPERTS - 1).astype(I32)
    n_used = (pends[-1] // rows).astype(I32).reshape(1)
    return pos, row_tok, blk_expert, n_used


def _moe_ln(x, x_b, w_r, b_r, w_gate, w_up, w_down, g, b, *, alpha):
    del x_b
    t, _ = x.shape
    ids, wts = _router(x, w_r, b_r, tm=256)
    pos, row_tok, blk_expert, n_used = _dispatch_plan(ids, t)
    y_rows = _expert_mlp(x, blk_expert, row_tok, n_used, w_gate, w_up, w_down)
    return _combine_ln(pos, y_rows, x, wts, g, b, alpha=alpha, tm=256)


def kernel(x, moba_w_in, moba_w_o, hgrn_w_in, hgrn_g_norm, hgrn_lb_raw, hgrn_w_o, ln_mix_g, ln_mix_b,
           moe_w_rg, moe_b_rg, moe_w_re, moe_b_re, moe_w_gate, moe_w_up, moe_w_down, ln_ffn_g, ln_ffn_b):
    batch, seq, d = x.shape
    depth = ln_mix_g.shape[0]
    alpha = (2 * depth) ** 0.25
    t = batch * seq
    xf = x.reshape(t, d)
    xb = xf.astype(BF16)
    n_re = N_GROUPS * EXPERTS_PER_GROUP
    for layer in range(depth):
        j = layer // 2
        if layer % 2 == 0:
            qkv = _matmul(xb, moba_w_in[j].astype(BF16), BF16, tm=1024, tn=512,
                          scaled_cols=d, scale=HEAD_DIM ** -0.5)
            o = _moba_attention(qkv, batch, seq)
            w_o = moba_w_o[j]
        else:
            proj = _matmul(xb, hgrn_w_in[j].astype(BF16), F32, tm=1024, tn=512)
            o = _hgrn_mix(proj, hgrn_lb_raw, hgrn_g_norm[j], layer=j, batch=batch, seq=seq)
            w_o = hgrn_w_o[j]
        xf, xb = _proj_ln(o, w_o.astype(BF16), xf, ln_mix_g[layer], ln_mix_b[layer], alpha=alpha, tm=256)
        w_re = jnp.transpose(moe_w_re[layer], (1, 0, 2)).reshape(d, n_re)
        w_r = jnp.concatenate([moe_w_rg[layer], w_re, jnp.zeros((d, LANES - N_GROUPS - n_re), F32)], axis=1)
        b_r = jnp.concatenate([moe_b_rg[layer], moe_b_re[layer].reshape(n_re),
                               jnp.zeros((LANES - N_GROUPS - n_re,), F32)]).reshape(1, LANES)
        xf, xb = _moe_ln(xf, xb, w_r, b_r, moe_w_gate[layer].astype(BF16), moe_w_up[layer].astype(BF16),
                         moe_w_down[layer].astype(BF16), ln_ffn_g[layer], ln_ffn_b[layer], alpha=alpha)
    return xf.reshape(batch, seq, d)
```

```python
import functools

import numpy as np
import jax
import jax.numpy as jnp
from jax import lax
from jax.experimental import pallas as pl
from jax.experimental.pallas import tpu as pltpu

F32 = jnp.float32
BF16 = jnp.bfloat16
I32 = jnp.int32

LANES = 128
SUBLANES = 8
VMEM_LIMIT_BYTES = 56 << 20

HEAD_DIM = 128
MOBA_BLOCK = 256
MOBA_TOPK = 3
HGRN_CHUNK = 64
N_GROUPS = 4
EXPERTS_PER_GROUP = 8
N_EXPERTS = N_GROUPS * EXPERTS_PER_GROUP
TOP_E = 2
LN_EPS = 1e-5
RMS_EPS = 1e-6

MOE_ROWS = 128
NT_DIMS = (((1,), (1,)), ((), ()))
TN_DIMS = (((0,), (0,)), ((), ()))


def _params(*semantics):
    return pltpu.CompilerParams(dimension_semantics=semantics, vmem_limit_bytes=VMEM_LIMIT_BYTES)


def _sigmoid(x):
    return 1.0 / (1.0 + jnp.exp(-x))


def _layer_norm(y, g, b):
    mu = jnp.mean(y, axis=-1, keepdims=True)
    d = y - mu
    var = jnp.mean(d * d, axis=-1, keepdims=True)
    return d * lax.rsqrt(var + LN_EPS) * g + b


def _matmul_kernel(x_ref, w_ref, o_ref, *, n_scaled_blocks, scale):
    acc = jnp.dot(x_ref[...], w_ref[...], preferred_element_type=F32)
    if n_scaled_blocks:
        acc = acc * jnp.where(pl.program_id(1) < n_scaled_blocks, scale, 1.0).astype(F32)
    o_ref[...] = acc.astype(o_ref.dtype)


def _matmul(x, w, out_dtype, *, tm, tn, scaled_cols=0, scale=1.0):
    m, k = x.shape
    n = w.shape[1]
    tm, tn = min(tm, m), min(tn, n)
    assert m % tm == 0 and n % tn == 0 and scaled_cols % tn == 0
    kern = functools.partial(_matmul_kernel, n_scaled_blocks=scaled_cols // tn, scale=scale)
    return pl.pallas_call(
        kern,
        grid=(m // tm, n // tn),
        in_specs=[pl.BlockSpec((tm, k), lambda i, j: (i, 0)),
                  pl.BlockSpec((k, tn), lambda i, j: (0, j))],
        out_specs=pl.BlockSpec((tm, tn), lambda i, j: (i, j)),
        out_shape=jax.ShapeDtypeStruct((m, n), out_dtype),
        compiler_params=_params("parallel", "parallel"),
        name="proj_matmul",
    )(x, w)


def _proj_ln_kernel(o_ref, w_ref, x_ref, g_ref, b_ref, xo_ref, xb_ref, *, alpha):
    h = jnp.dot(o_ref[...], w_ref[...], preferred_element_type=F32)
    out = _layer_norm(alpha * x_ref[...] + h, g_ref[...], b_ref[...])
    xo_ref[...] = out
    xb_ref[...] = out.astype(BF16)


def _proj_ln(o, w, x, g, b, *, alpha, tm):
    t, d = x.shape
    tm = min(tm, t)
    assert t % tm == 0
    row = lambda i: (i, 0)
    fixed = lambda i: (0, 0)
    return pl.pallas_call(
        functools.partial(_proj_ln_kernel, alpha=alpha),
        grid=(t // tm,),
        in_specs=[pl.BlockSpec((tm, d), row), pl.BlockSpec((d, d), fixed), pl.BlockSpec((tm, d), row),
                  pl.BlockSpec((1, d), fixed), pl.BlockSpec((1, d), fixed)],
        out_specs=[pl.BlockSpec((tm, d), row), pl.BlockSpec((tm, d), row)],
        out_shape=[jax.ShapeDtypeStruct((t, d), F32), jax.ShapeDtypeStruct((t, d), BF16)],
        compiler_params=_params("parallel"),
        name="proj_ln",
    )(o, w, x, g.reshape(1, d), b.reshape(1, d))


MOBA_HEADS_PER_STEP = 2
MOBA_TILE = 2 * MOBA_BLOCK


def _moba_kernel(q_ref, k_ref, v_ref, o_ref, kmean_ref, vt_ref, sel_ref, *, n_blocks):
    blk, tile, hd = MOBA_BLOCK, MOBA_TILE, HEAD_DIM
    heads = range(MOBA_HEADS_PER_STEP)
    i = pl.program_id(2)

    @pl.when(i == 0)
    def _():
        kmean_ref[...] = jnp.zeros_like(kmean_ref)
        for h in heads:
            cols = slice(h * hd, (h + 1) * hd)
            for j in range(n_blocks):
                kj = k_ref[pl.ds(j * blk, blk), cols].astype(F32)
                kmean_ref[h, pl.ds(j, 1), :] = jnp.mean(kj, axis=0, keepdims=True)
            for p in range(n_blocks // 2):
                vt_ref[h, p] = v_ref[pl.ds(p * tile, tile), cols].T

    q = [q_ref[:, h * hd:(h + 1) * hd] for h in heads]
    for h in heads:
        gate = lax.dot_general(kmean_ref[h], q[h].astype(F32), NT_DIMS,
                               precision=lax.Precision.HIGHEST, preferred_element_type=F32)
        blk_idx = lax.broadcasted_iota(I32, gate.shape, 0)
        gate = jnp.where(blk_idx < i, gate, -jnp.inf)
        rank = jnp.zeros(gate.shape, I32)
        for jp in range(n_blocks):
            row = gate[jp:jp + 1, :]
            beats = (row > gate) | ((row == gate) & (jp < blk_idx))
            rank = rank + jnp.where(beats, 1, 0)
        sel_ref[h] = jnp.where((gate > -jnp.inf) & (rank < MOBA_TOPK), 1.0, 0.0)

    def scores(h, p):
        kt = k_ref[pl.ds(pl.multiple_of(p * tile, tile), tile), h * hd:(h + 1) * hd]
        return lax.dot_general(kt, q[h], NT_DIMS, preferred_element_type=F32)

    def masked(s, keep_top, keep_bot):
        return jnp.concatenate([jnp.where(keep_top, s[:blk], -jnp.inf),
                                jnp.where(keep_bot, s[blk:], -jnp.inf)], axis=0)

    p_own = i // 2
    causal = lax.broadcasted_iota(I32, (blk, blk), 0) <= lax.broadcasted_iota(I32, (blk, blk), 1)
    i_odd = jnp.full((blk, blk), i % 2, I32) == 1
    state = []
    for h in heads:
        chosen_prev = sel_ref[h, pl.ds(2 * p_own, 1), :] > 0.0
        s = masked(scores(h, p_own), (i_odd & chosen_prev) | (~i_odd & causal), i_odd & causal)
        m = jnp.max(s, axis=0, keepdims=True)
        p = jnp.exp(s - m)
        l = jnp.sum(p, axis=0, keepdims=True)
        acc = jnp.dot(vt_ref[h, p_own], p.astype(BF16), preferred_element_type=F32)
        state += [m, l, acc]

    def body(pt, carry):
        out = []
        for h in heads:
            m, l, acc = carry[3 * h:3 * h + 3]
            s = masked(scores(h, pt), sel_ref[h, pl.ds(2 * pt, 1), :] > 0.0,
                       sel_ref[h, pl.ds(2 * pt + 1, 1), :] > 0.0)
            m_new = jnp.maximum(m, jnp.max(s, axis=0, keepdims=True))
            a = jnp.exp(m - m_new)
            p = jnp.exp(s - m_new)
            l = a * l + jnp.sum(p, axis=0, keepdims=True)
            acc = a * acc + jnp.dot(vt_ref[h, pt], p.astype(BF16), preferred_element_type=F32)
            out += [m_new, l, acc]
        return tuple(out)

    state = lax.fori_loop(0, p_own, body, tuple(state))
    for h in heads:
        _, l, acc = state[3 * h:3 * h + 3]
        o_ref[:, h * hd:(h + 1) * hd] = (acc / l).T.astype(o_ref.dtype)


def _moba_attention(qkv, batch, seq):
    t, d3 = qkv.shape
    d = d3 // 3
    hs = MOBA_HEADS_PER_STEP
    n_head_groups = d // (HEAD_DIM * hs)
    n_blocks = seq // MOBA_BLOCK
    assert seq % MOBA_TILE == 0 and d % (HEAD_DIM * hs) == 0
    nb_pad = -(-n_blocks // SUBLANES) * SUBLANES
    kv_spec = lambda off: pl.BlockSpec((seq, HEAD_DIM * hs), lambda b, h, i: (b, off + h))
    q_spec = pl.BlockSpec((MOBA_BLOCK, HEAD_DIM * hs), lambda b, h, i: (b * n_blocks + i, h))
    return pl.pallas_call(
        functools.partial(_moba_kernel, n_blocks=n_blocks),
        grid=(batch, n_head_groups, n_blocks),
        in_specs=[q_spec, kv_spec(n_head_groups), kv_spec(2 * n_head_groups)],
        out_specs=q_spec,
        out_shape=jax.ShapeDtypeStruct((t, d), BF16),
        scratch_shapes=[pltpu.VMEM((hs, nb_pad, HEAD_DIM), F32),
                        pltpu.VMEM((hs, n_blocks // 2, HEAD_DIM, MOBA_TILE), BF16),
                        pltpu.VMEM((hs, nb_pad, MOBA_BLOCK), F32)],
        compiler_params=_params("parallel", "parallel", "arbitrary"),
        name="moba_attention",
    )(qkv, qkv, qkv)


_HGRN_LEVELS = (32, 16, 8, 4, 2, 1)
HGRN_UNROLL = 4


def _hgrn_constants():
    c = HGRN_CHUNK
    idx = np.arange(c)
    t, s = idx[:, None], idx[None, :]
    sums = [(s <= t)]
    upper, pair = [], []
    for m in _HGRN_LEVELS:
        same = (t // (2 * m)) == (s // (2 * m))
        t_up, s_up = (t % (2 * m)) >= m, (s % (2 * m)) >= m
        up_rows = same & t_up & s_up & (s <= t)
        lo_rows = same & ~t_up & ~s_up & (s > t)
        sums.append(up_rows | lo_rows)
        upper.append(np.broadcast_to(t_up, (c, LANES)))
        pair.append(same & t_up & ~s_up)
    return (np.concatenate(sums, 0).astype(np.float32),
            np.stack(upper).astype(np.float32), np.stack(pair).astype(np.float32))


def _hgrn_kernel(q_ref, f_ref, i_ref, g_ref, lbraw_ref, gn_ref, sums_ref, upper_ref, pair_ref, o_ref,
                 qg_ref, oin_ref, u_ref, dl_ref, *, layer, n_chunks):
    c = HGRN_CHUNK
    dk = HEAD_DIM
    raw = lbraw_ref[...]
    e = jnp.exp(raw - jnp.max(raw, axis=0, keepdims=True))
    sm = e / jnp.sum(e, axis=0, keepdims=True)
    cum = sm[0:1, :]
    for l in range(1, layer + 1):
        cum = cum + sm[l:l + 1, :]
    lb = cum - sm[0:1, :]
    gnorm = gn_ref[...]
    eye = (lax.broadcasted_iota(I32, (c, c), 0) == lax.broadcasted_iota(I32, (c, c), 1))

    def local(ci, carry):
        rows = pl.ds(pl.multiple_of(ci * c, c), c)
        qv, fv, iv = q_ref[rows, :], f_ref[rows, :], i_ref[rows, :]
        qs = qv * _sigmoid(qv)
        fg = lb + (1.0 - lb) * _sigmoid(fv)
        log_f = jnp.log(fg)
        kk = 1.0 - fg
        hi = log_f.astype(BF16)
        rest = log_f - hi.astype(F32)
        mid = rest.astype(BF16)
        lo = (rest - mid.astype(F32)).astype(BF16)
        s3 = jnp.dot(sums_ref[...], jnp.concatenate([hi, mid, lo], axis=1), preferred_element_type=F32)
        sums = (s3[:, 0:dk] + s3[:, dk:2 * dk]) + s3[:, 2 * dk:3 * dk]
        gcum = sums[0:c, :]
        qg_ref[rows, :] = (qs * jnp.exp(gcum)).astype(BF16)
        a = jnp.where(eye, jnp.sum(qs * kk, axis=1, keepdims=True), 0.0)
        for li in range(len(_HGRN_LEVELS)):
            dec = jnp.exp(sums[(li + 1) * c:(li + 2) * c, :])
            is_up = upper_ref[li] > 0.0
            qe = jnp.where(is_up, qs * dec, 0.0).astype(BF16)
            ke = jnp.where(is_up, 0.0, kk * dec).astype(BF16)
            prod = lax.dot_general(qe, ke, NT_DIMS, preferred_element_type=F32)
            a = a + jnp.where(pair_ref[li] > 0.0, prod, 0.0)
        iv_b = iv.astype(BF16)
        oin_ref[rows, :] = jnp.dot(a.astype(BF16), iv_b, preferred_element_type=F32)
        g_last = gcum[c - 1:c, :]
        k_dec = (kk * jnp.exp(g_last - gcum)).astype(BF16)
        u_ref[ci] = lax.dot_general(iv_b, k_dec, TN_DIMS, preferred_element_type=F32)
        dl_ref[pl.ds(ci, 1), :] = jnp.exp(g_last)
        return carry

    lax.fori_loop(0, n_chunks, local, 0, unroll=HGRN_UNROLL)

    def recur(ci, state_t):
        rows = pl.ds(pl.multiple_of(ci * c, c), c)
        o = oin_ref[rows, :] + lax.dot_general(qg_ref[rows, :], state_t.astype(BF16), NT_DIMS,
                                               preferred_element_type=F32)
        gv = g_ref[rows, :]
        o = o * lax.rsqrt(jnp.mean(o * o, axis=1, keepdims=True) + RMS_EPS) * gnorm
        o_ref[rows, :] = (o * (gv * _sigmoid(gv))).astype(o_ref.dtype)
        return dl_ref[pl.ds(ci, 1), :] * state_t + u_ref[ci]

    lax.fori_loop(0, n_chunks, recur, jnp.zeros((HEAD_DIM, HEAD_DIM), F32), unroll=HGRN_UNROLL)


def _hgrn_mix(proj, lb_raw, g_norm, *, layer, batch, seq):
    t, d4 = proj.shape
    d = d4 // 4
    n_heads = d // HEAD_DIM
    n_chunks = seq // HGRN_CHUNK
    assert seq % (HGRN_CHUNK * HGRN_UNROLL) == 0
    sums, upper, pair = _hgrn_constants()
    n_layers = lb_raw.shape[0]
    part = lambda off: pl.BlockSpec((seq, HEAD_DIM), lambda b, h: (b, off + h))
    whole = lambda a: pl.BlockSpec(a.shape, lambda b, h: (0,) * a.ndim)
    return pl.pallas_call(
        functools.partial(_hgrn_kernel, layer=layer, n_chunks=n_chunks),
        grid=(batch, n_heads),
        in_specs=[part(0), part(n_heads), part(2 * n_heads), part(3 * n_heads),
                  pl.BlockSpec((n_layers, HEAD_DIM), lambda b, h: (0, h)),
                  pl.BlockSpec((1, HEAD_DIM), lambda b, h: (0, 0)),
                  whole(sums), whole(upper), whole(pair)],
        out_specs=pl.BlockSpec((seq, HEAD_DIM), lambda b, h: (b, h)),
        out_shape=jax.ShapeDtypeStruct((t, d), BF16),
        scratch_shapes=[pltpu.VMEM((seq, HEAD_DIM), BF16),
                        pltpu.VMEM((seq, HEAD_DIM), F32),
                        pltpu.VMEM((n_chunks, HEAD_DIM, HEAD_DIM), F32),
                        pltpu.VMEM((n_chunks, HEAD_DIM), F32)],
        compiler_params=_params("parallel", "parallel"),
        name="hgrn_recurrence",
    )(proj, proj, proj, proj, lb_raw, g_norm.reshape(1, HEAD_DIM),
      jnp.asarray(sums, BF16), jnp.asarray(upper), jnp.asarray(pair))


def _router_kernel(x_ref, w_ref, b_ref, ids_ref, wts_ref):
    g, eg = N_GROUPS, EXPERTS_PER_GROUP
    logits = jnp.dot(x_ref[...], w_ref[...], precision=lax.Precision.HIGHEST,
                     preferred_element_type=F32) + b_ref[...]
    lane = lax.broadcasted_iota(I32, logits.shape, 1)

    def softmax_over(mask):
        z = jnp.where(mask, logits, -jnp.inf)
        ez = jnp.exp(z - jnp.max(z, axis=1, keepdims=True))
        return ez / jnp.sum(ez, axis=1, keepdims=True)

    def top1(p, mask):
        best = jnp.max(jnp.where(mask, p, -1.0), axis=1, keepdims=True)
        where = jnp.min(jnp.where(mask & (p == best), lane, LANES), axis=1, keepdims=True)
        return best, where

    is_group = lane < g
    g_w, g_idx = top1(softmax_over(is_group), is_group)
    lo = g + g_idx * eg
    in_group = (lane >= lo) & (lane < lo + eg)
    pe = softmax_over(in_group)
    w1, i1 = top1(pe, in_group)
    w2, i2 = top1(pe, in_group & (lane != i1))
    den = w1 + w2
    wt1, wt2 = g_w * (w1 / den), g_w * (w2 / den)
    ids_ref[...] = jnp.where(lane == 0, i1 - g, jnp.where(lane == 1, i2 - g, 0))
    wts_ref[...] = jnp.where(lane == 0, wt1, jnp.where(lane == 1, wt2, 0.0))


def _router(x, w_r, b_r, *, tm):
    t, d = x.shape
    tm = min(tm, t)
    assert t % tm == 0
    row = lambda i: (i, 0)
    fixed = lambda i: (0, 0)
    return pl.pallas_call(
        _router_kernel,
        grid=(t // tm,),
        in_specs=[pl.BlockSpec((tm, d), row), pl.BlockSpec((d, LANES), fixed), pl.BlockSpec((1, LANES), fixed)],
        out_specs=[pl.BlockSpec((tm, LANES), row), pl.BlockSpec((tm, LANES), row)],
        out_shape=[jax.ShapeDtypeStruct((t, LANES), I32), jax.ShapeDtypeStruct((t, LANES), F32)],
        compiler_params=_params("parallel"),
        name="moe_router",
    )(x, w_r, b_r)


def _expert_kernel(blk_expert_ref, row_tok_ref, n_used_ref, x_hbm, wg_ref, wu_ref, wd_ref, y_ref, xbuf, sem):
    del blk_expert_ref
    i = pl.program_id(0)
    rows = xbuf.shape[0]

    @pl.when(i < n_used_ref[0])
    def _():
        base = i * rows

        def issue(r, carry):
            tok = row_tok_ref[base + r]
            pltpu.make_async_copy(x_hbm.at[pl.ds(tok, 1), :], xbuf.at[pl.ds(r, 1), :], sem).start()
            return carry

        lax.fori_loop(0, rows, issue, 0)
        pltpu.make_async_copy(x_hbm.at[pl.ds(0, rows), :], xbuf, sem).wait()
        xb = xbuf[...].astype(BF16)
        hg = jnp.dot(xb, wg_ref[...], preferred_element_type=F32)
        hu = jnp.dot(xb, wu_ref[...], preferred_element_type=F32)
        h = (hg * _sigmoid(hg)) * hu
        y_ref[...] = jnp.dot(h.astype(BF16), wd_ref[...], preferred_element_type=F32)

    @pl.when(i >= n_used_ref[0])
    def _():
        y_ref[...] = jnp.zeros_like(y_ref)


def _expert_mlp(x, blk_expert, row_tok, n_used, w_gate, w_up, w_down):
    t, d = x.shape
    n_e, _, f = w_gate.shape
    n_blk = blk_expert.shape[0]
    rows = MOE_ROWS
    grid_spec = pltpu.PrefetchScalarGridSpec(
        num_scalar_prefetch=3,
        grid=(n_blk,),
        in_specs=[pl.BlockSpec(memory_space=pl.ANY),
                  pl.BlockSpec((None, d, f), lambda i, be, rt, nu: (be[i], 0, 0)),
                  pl.BlockSpec((None, d, f), lambda i, be, rt, nu: (be[i], 0, 0)),
                  pl.BlockSpec((None, f, d), lambda i, be, rt, nu: (be[i], 0, 0))],
        out_specs=pl.BlockSpec((rows, d), lambda i, be, rt, nu: (i, 0)),
        scratch_shapes=[pltpu.VMEM((rows, d), F32), pltpu.SemaphoreType.DMA(())],
    )
    return pl.pallas_call(
        _expert_kernel,
        grid_spec=grid_spec,
        out_shape=jax.ShapeDtypeStruct((n_blk * rows, d), F32),
        compiler_params=_params("arbitrary"),
        name="moe_experts",
    )(blk_expert, row_tok, n_used, x, w_gate, w_up, w_down)


def _combine_ln_kernel(pos_ref, y_hbm, x_ref, wts_ref, g_ref, b_ref, xo_ref, xb_ref, ybuf, sem, *, alpha):
    i = pl.program_id(0)
    tm = x_ref.shape[0]
    base = i * tm

    def issue(r, carry):
        for k in range(TOP_E):
            p = pos_ref[TOP_E * (base + r) + k]
            pltpu.make_async_copy(y_hbm.at[pl.ds(p, 1), :], ybuf.at[k, pl.ds(r, 1), :], sem).start()
        return carry

    lax.fori_loop(0, tm, issue, 0)
    for k in range(TOP_E):
        pltpu.make_async_copy(y_hbm.at[pl.ds(0, tm), :], ybuf.at[k], sem).wait()
    wts = wts_ref[...]
    m = wts[:, 0:1] * ybuf[0] + wts[:, 1:2] * ybuf[1]
    out = _layer_norm(alpha * x_ref[...] + m, g_ref[...], b_ref[...])
    xo_ref[...] = out
    xb_ref[...] = out.astype(BF16)


def _combine_ln(pos, y_rows, x, wts, g, b, *, alpha, tm):
    t, d = x.shape
    tm = min(tm, t)
    assert t % tm == 0
    row = lambda i, p: (i, 0)
    fixed = lambda i, p: (0, 0)
    grid_spec = pltpu.PrefetchScalarGridSpec(
        num_scalar_prefetch=1,
        grid=(t // tm,),
        in_specs=[pl.BlockSpec(memory_space=pl.ANY), pl.BlockSpec((tm, d), row), pl.BlockSpec((tm, LANES), row),
                  pl.BlockSpec((1, d), fixed), pl.BlockSpec((1, d), fixed)],
        out_specs=[pl.BlockSpec((tm, d), row), pl.BlockSpec((tm, d), row)],
        scratch_shapes=[pltpu.VMEM((TOP_E, tm, d), F32), pltpu.SemaphoreType.DMA(())],
    )
    return pl.pallas_call(
        functools.partial(_combine_ln_kernel, alpha=alpha),
        grid_spec=grid_spec,
        out_shape=[jax.ShapeDtypeStruct((t, d), F32), jax.ShapeDtypeStruct((t, d), BF16)],
        compiler_params=_params("arbitrary"),
        name="moe_combine_ln",
    )(pos, y_rows, x, wts, g.reshape(1, d), b.reshape(1, d))


def _dispatch_plan(ids, n_tokens):
    rows = MOE_ROWS
    n_assign = n_tokens * TOP_E
    eid = ids[:, :TOP_E].reshape(n_assign)
    onehot = (eid[:, None] == jnp.arange(N_EXPERTS, dtype=I32)[None, :]).astype(I32)
    csum = jnp.cumsum(onehot, axis=0)
    rank = jnp.sum(onehot * csum, axis=1) - 1
    counts = csum[-1]
    padded = (counts + rows - 1) // rows * rows
    pends = jnp.cumsum(padded)
    pstarts = pends - padded
    pos = (pstarts[eid] + rank).astype(I32)
    n_blk = n_assign // rows + N_EXPERTS
    tok = jnp.arange(n_assign, dtype=I32) // TOP_E
    row_tok = jnp.zeros((n_blk * rows,), I32).at[pos].set(tok)
    blk_expert = jnp.minimum(jnp.searchsorted(pends, jnp.arange(n_blk, dtype=I32) * rows, side='right'),
                             N_EXPERTS - 1).astype(I32)
    n_used = (pends[-1] // rows).astype(I32).reshape(1)
    return pos, row_tok, blk_expert, n_used


def _moe_ln(x, x_b, w_r, b_r, w_gate, w_up, w_down, g, b, *, alpha):
    del x_b
    t, _ = x.shape
    ids, wts = _router(x, w_r, b_r, tm=256)
    pos, row_tok, blk_expert, n_used = _dispatch_plan(ids, t)
    y_rows = _expert_mlp(x, blk_expert, row_tok, n_used, w_gate, w_up, w_down)
    return _combine_ln(pos, y_rows, x, wts, g, b, alpha=alpha, tm=256)


def kernel(x, moba_w_in, moba_w_o, hgrn_w_in, hgrn_g_norm, hgrn_lb_raw, hgrn_w_o, ln_mix_g, ln_mix_b,
           moe_w_rg, moe_b_rg, moe_w_re, moe_b_re, moe_w_gate, moe_w_up, moe_w_down, ln_ffn_g, ln_ffn_b):
    batch, seq, d = x.shape
    depth = ln_mix_g.shape[0]
    alpha = (2 * depth) ** 0.25
    t = batch * seq
    xf = x.reshape(t, d)
    xb = xf.astype(BF16)
    n_re = N_GROUPS * EXPERTS_PER_GROUP
    for layer in range(depth):
        j = layer // 2
        if layer % 2 == 0:
            qkv = _matmul(xb, moba_w_in[j].astype(BF16), BF16, tm=1024, tn=512,
                          scaled_cols=d, scale=HEAD_DIM ** -0.5)
            o = _moba_attention(qkv, batch, seq)
            w_o = moba_w_o[j]
        else:
            proj = _matmul(xb, hgrn_w_in[j].astype(BF16), F32, tm=1024, tn=512)
            o = _hgrn_mix(proj, hgrn_lb_raw, hgrn_g_norm[j], layer=j, batch=batch, seq=seq)
            w_o = hgrn_w_o[j]
        xf, xb = _proj_ln(o, w_o.astype(BF16), xf, ln_mix_g[layer], ln_mix_b[layer], alpha=alpha, tm=256)
        w_re = jnp.transpose(moe_w_re[layer], (1, 0, 2)).reshape(d, n_re)
        w_r = jnp.concatenate([moe_w_rg[layer], w_re, jnp.zeros((d, LANES - N_GROUPS - n_re), F32)], axis=1)
        b_r = jnp.concatenate([moe_b_rg[layer], moe_b_re[layer].reshape(n_re),
                               jnp.zeros((LANES - N_GROUPS - n_re,), F32)]).reshape(1, LANES)
        xf, xb = _moe_ln(xf, xb, w_r, b_r, moe_w_gate[layer].astype(BF16), moe_w_up[layer].astype(BF16),
                         moe_w_down[layer].astype(BF16), ln_ffn_g[layer], ln_ffn_b[layer], alpha=alpha)
    return xf.reshape(batch, seq, d)
```

```python
import functools

import numpy as np
import jax
import jax.numpy as jnp
from jax import lax
from jax.experimental import pallas as pl
from jax.experimental.pallas import tpu as pltpu

F32 = jnp.float32
BF16 = jnp.bfloat16
I32 = jnp.int32

LANES = 128
SUBLANES = 8
VMEM_LIMIT_BYTES = 56 << 20

HEAD_DIM = 128
MOBA_BLOCK = 256
MOBA_TOPK = 3
HGRN_CHUNK = 64
N_GROUPS = 4
EXPERTS_PER_GROUP = 8
N_EXPERTS = N_GROUPS * EXPERTS_PER_GROUP
TOP_E = 2
LN_EPS = 1e-5
RMS_EPS = 1e-6

MOE_ROWS = 256
NT_DIMS = (((1,), (1,)), ((), ()))
TN_DIMS = (((0,), (0,)), ((), ()))


def _params(*semantics):
    return pltpu.CompilerParams(dimension_semantics=semantics, vmem_limit_bytes=VMEM_LIMIT_BYTES)


def _sigmoid(x):
    return 1.0 / (1.0 + jnp.exp(-x))


def _layer_norm(y, g, b):
    mu = jnp.mean(y, axis=-1, keepdims=True)
    d = y - mu
    var = jnp.mean(d * d, axis=-1, keepdims=True)
    return d * lax.rsqrt(var + LN_EPS) * g + b


def _matmul_kernel(x_ref, w_ref, o_ref, *, n_scaled_blocks, scale):
    acc = jnp.dot(x_ref[...], w_ref[...], preferred_element_type=F32)
    if n_scaled_blocks:
        acc = acc * jnp.where(pl.program_id(1) < n_scaled_blocks, scale, 1.0).astype(F32)
    o_ref[...] = acc.astype(o_ref.dtype)


def _matmul(x, w, out_dtype, *, tm, tn, scaled_cols=0, scale=1.0):
    m, k = x.shape
    n = w.shape[1]
    tm, tn = min(tm, m), min(tn, n)
    assert m % tm == 0 and n % tn == 0 and scaled_cols % tn == 0
    kern = functools.partial(_matmul_kernel, n_scaled_blocks=scaled_cols // tn, scale=scale)
    return pl.pallas_call(
        kern,
        grid=(m // tm, n // tn),
        in_specs=[pl.BlockSpec((tm, k), lambda i, j: (i, 0)),
                  pl.BlockSpec((k, tn), lambda i, j: (0, j))],
        out_specs=pl.BlockSpec((tm, tn), lambda i, j: (i, j)),
        out_shape=jax.ShapeDtypeStruct((m, n), out_dtype),
        compiler_params=_params("parallel", "parallel"),
        name="proj_matmul",
    )(x, w)


def _proj_ln_kernel(o_ref, w_ref, x_ref, g_ref, b_ref, xo_ref, xb_ref, *, alpha):
    h = jnp.dot(o_ref[...], w_ref[...], preferred_element_type=F32)
    out = _layer_norm(alpha * x_ref[...] + h, g_ref[...], b_ref[...])
    xo_ref[...] = out
    xb_ref[...] = out.astype(BF16)


def _proj_ln(o, w, x, g, b, *, alpha, tm):
    t, d = x.shape
    tm = min(tm, t)
    assert t % tm == 0
    row = lambda i: (i, 0)
    fixed = lambda i: (0, 0)
    return pl.pallas_call(
        functools.partial(_proj_ln_kernel, alpha=alpha),
        grid=(t // tm,),
        in_specs=[pl.BlockSpec((tm, d), row), pl.BlockSpec((d, d), fixed), pl.BlockSpec((tm, d), row),
                  pl.BlockSpec((1, d), fixed), pl.BlockSpec((1, d), fixed)],
        out_specs=[pl.BlockSpec((tm, d), row), pl.BlockSpec((tm, d), row)],
        out_shape=[jax.ShapeDtypeStruct((t, d), F32), jax.ShapeDtypeStruct((t, d), BF16)],
        compiler_params=_params("parallel"),
        name="proj_ln",
    )(o, w, x, g.reshape(1, d), b.reshape(1, d))


MOBA_HEADS_PER_STEP = 4
MOBA_TILE = 2 * MOBA_BLOCK


def _moba_kernel(q_ref, k_ref, v_ref, o_ref, kmean_ref, vt_ref, sel_ref, *, n_blocks):
    blk, tile, hd = MOBA_BLOCK, MOBA_TILE, HEAD_DIM
    heads = range(MOBA_HEADS_PER_STEP)
    i = pl.program_id(2)

    @pl.when(i == 0)
    def _():
        kmean_ref[...] = jnp.zeros_like(kmean_ref)
        for h in heads:
            cols = slice(h * hd, (h + 1) * hd)
            for j in range(n_blocks):
                kj = k_ref[pl.ds(j * blk, blk), cols].astype(F32)
                kmean_ref[h, pl.ds(j, 1), :] = jnp.mean(kj, axis=0, keepdims=True)
            for p in range(n_blocks // 2):
                vt_ref[h, p] = v_ref[pl.ds(p * tile, tile), cols].T

    q = [q_ref[:, h * hd:(h + 1) * hd] for h in heads]
    for h in heads:
        gate = lax.dot_general(kmean_ref[h], q[h].astype(F32), NT_DIMS,
                               precision=lax.Precision.HIGHEST, preferred_element_type=F32)
        blk_idx = lax.broadcasted_iota(I32, gate.shape, 0)
        gate = jnp.where(blk_idx < i, gate, -jnp.inf)
        rank = jnp.zeros(gate.shape, I32)
        for jp in range(n_blocks):
            row = gate[jp:jp + 1, :]
            beats = (row > gate) | ((row == gate) & (jp < blk_idx))
            rank = rank + jnp.where(beats, 1, 0)
        sel_ref[h] = jnp.where((gate > -jnp.inf) & (rank < MOBA_TOPK), 1.0, 0.0)

    def scores(h, p):
        kt = k_ref[pl.ds(pl.multiple_of(p * tile, tile), tile), h * hd:(h + 1) * hd]
        return lax.dot_general(kt, q[h], NT_DIMS, preferred_element_type=F32)

    def masked(s, keep_top, keep_bot):
        return jnp.concatenate([jnp.where(keep_top, s[:blk], -jnp.inf),
                                jnp.where(keep_bot, s[blk:], -jnp.inf)], axis=0)

    p_own = i // 2
    causal = lax.broadcasted_iota(I32, (blk, blk), 0) <= lax.broadcasted_iota(I32, (blk, blk), 1)
    i_odd = jnp.full((blk, blk), i % 2, I32) == 1
    s_all = [scores(h, p_own) for h in heads]
    state, p_all = [], []
    for h in heads:
        chosen_prev = sel_ref[h, pl.ds(2 * p_own, 1), :] > 0.0
        s = masked(s_all[h], (i_odd & chosen_prev) | (~i_odd & causal), i_odd & causal)
        m = jnp.max(s, axis=0, keepdims=True)
        p = jnp.exp(s - m)
        state += [m, jnp.sum(p, axis=0, keepdims=True)]
        p_all.append(p.astype(BF16))
    for h in heads:
        acc = jnp.dot(vt_ref[h, p_own], p_all[h], preferred_element_type=F32)
        state.insert(3 * h + 2, acc)

    def body(pt, carry):
        s_all = [scores(h, pt) for h in heads]
        out, p_all, a_all = [], [], []
        for h in heads:
            m, l, _ = carry[3 * h:3 * h + 3]
            s = masked(s_all[h], sel_ref[h, pl.ds(2 * pt, 1), :] > 0.0,
                       sel_ref[h, pl.ds(2 * pt + 1, 1), :] > 0.0)
            m_new = jnp.maximum(m, jnp.max(s, axis=0, keepdims=True))
            a = jnp.exp(m - m_new)
            p = jnp.exp(s - m_new)
            out += [m_new, a * l + jnp.sum(p, axis=0, keepdims=True)]
            p_all.append(p.astype(BF16))
            a_all.append(a)
        for h in heads:
            acc = a_all[h] * carry[3 * h + 2] + jnp.dot(vt_ref[h, pt], p_all[h], preferred_element_type=F32)
            out.insert(3 * h + 2, acc)
        return tuple(out)

    state = lax.fori_loop(0, p_own, body, tuple(state))
    for h in heads:
        _, l, acc = state[3 * h:3 * h + 3]
        o_ref[:, h * hd:(h + 1) * hd] = (acc / l).T.astype(o_ref.dtype)


def _moba_attention(qkv, batch, seq):
    t, d3 = qkv.shape
    d = d3 // 3
    hs = MOBA_HEADS_PER_STEP
    n_head_groups = d // (HEAD_DIM * hs)
    n_blocks = seq // MOBA_BLOCK
    assert seq % MOBA_TILE == 0 and d % (HEAD_DIM * hs) == 0
    nb_pad = -(-n_blocks // SUBLANES) * SUBLANES
    kv_spec = lambda off: pl.BlockSpec((seq, HEAD_DIM * hs), lambda b, h, i: (b, off + h))
    q_spec = pl.BlockSpec((MOBA_BLOCK, HEAD_DIM * hs), lambda b, h, i: (b * n_blocks + i, h))
    return pl.pallas_call(
        functools.partial(_moba_kernel, n_blocks=n_blocks),
        grid=(batch, n_head_groups, n_blocks),
        in_specs=[q_spec, kv_spec(n_head_groups), kv_spec(2 * n_head_groups)],
        out_specs=q_spec,
        out_shape=jax.ShapeDtypeStruct((t, d), BF16),
        scratch_shapes=[pltpu.VMEM((hs, nb_pad, HEAD_DIM), F32),
                        pltpu.VMEM((hs, n_blocks // 2, HEAD_DIM, MOBA_TILE), BF16),
                        pltpu.VMEM((hs, nb_pad, MOBA_BLOCK), F32)],
        compiler_params=_params("parallel", "parallel", "arbitrary"),
        name="moba_attention",
    )(qkv, qkv, qkv)


_HGRN_LEVELS = (32, 16, 8, 4, 2, 1)
HGRN_UNROLL = 4


def _hgrn_constants():
    c = HGRN_CHUNK
    idx = np.arange(c)
    t, s = idx[:, None], idx[None, :]
    sums = [(s <= t)]
    upper, pair = [], []
    for m in _HGRN_LEVELS:
        same = (t // (2 * m)) == (s // (2 * m))
        t_up, s_up = (t % (2 * m)) >= m, (s % (2 * m)) >= m
        up_rows = same & t_up & s_up & (s <= t)
        lo_rows = same & ~t_up & ~s_up & (s > t)
        sums.append(up_rows | lo_rows)
        upper.append(np.broadcast_to(t_up, (c, LANES)))
        pair.append(same & t_up & ~s_up)
    return (np.concatenate(sums, 0).astype(np.float32),
            np.stack(upper).astype(np.float32), np.stack(pair).astype(np.float32))


def _hgrn_kernel(q_ref, f_ref, i_ref, g_ref, lbraw_ref, gn_ref, sums_ref, upper_ref, pair_ref, o_ref,
                 qg_ref, oin_ref, u_ref, dl_ref, *, layer, n_chunks):
    c = HGRN_CHUNK
    dk = HEAD_DIM
    raw = lbraw_ref[...]
    e = jnp.exp(raw - jnp.max(raw, axis=0, keepdims=True))
    sm = e / jnp.sum(e, axis=0, keepdims=True)
    cum = sm[0:1, :]
    for l in range(1, layer + 1):
        cum = cum + sm[l:l + 1, :]
    lb = cum - sm[0:1, :]
    gnorm = gn_ref[...]
    eye = (lax.broadcasted_iota(I32, (c, c), 0) == lax.broadcasted_iota(I32, (c, c), 1))

    def local(ci, carry):
        rows = pl.ds(pl.multiple_of(ci * c, c), c)
        qv, fv, iv = q_ref[rows, :], f_ref[rows, :], i_ref[rows, :]
        qs = qv * _sigmoid(qv)
        fg = lb + (1.0 - lb) * _sigmoid(fv)
        log_f = jnp.log(fg)
        kk = 1.0 - fg
        hi = log_f.astype(BF16)
        rest = log_f - hi.astype(F32)
        mid = rest.astype(BF16)
        lo = (rest - mid.astype(F32)).astype(BF16)
        s3 = jnp.dot(sums_ref[...], jnp.concatenate([hi, mid, lo], axis=1), preferred_element_type=F32)
        sums = (s3[:, 0:dk] + s3[:, dk:2 * dk]) + s3[:, 2 * dk:3 * dk]
        gcum = sums[0:c, :]
        qg_ref[rows, :] = (qs * jnp.exp(gcum)).astype(BF16)
        a = jnp.where(eye, jnp.sum(qs * kk, axis=1, keepdims=True), 0.0)
        for li in range(len(_HGRN_LEVELS)):
            dec = jnp.exp(sums[(li + 1) * c:(li + 2) * c, :])
            is_up = upper_ref[li] > 0.0
            qe = jnp.where(is_up, qs * dec, 0.0).astype(BF16)
            ke = jnp.where(is_up, 0.0, kk * dec).astype(BF16)
            prod = lax.dot_general(qe, ke, NT_DIMS, preferred_element_type=F32)
            a = a + jnp.where(pair_ref[li] > 0.0, prod, 0.0)
        iv_b = iv.astype(BF16)
        oin_ref[rows, :] = jnp.dot(a.astype(BF16), iv_b, preferred_element_type=F32)
        g_last = gcum[c - 1:c, :]
        k_dec = (kk * jnp.exp(g_last - gcum)).astype(BF16)
        u_ref[ci] = lax.dot_general(iv_b, k_dec, TN_DIMS, preferred_element_type=F32)
        dl_ref[pl.ds(ci, 1), :] = jnp.exp(g_last)
        return carry

    lax.fori_loop(0, n_chunks, local, 0, unroll=HGRN_UNROLL)

    def recur(ci, state_t):
        rows = pl.ds(pl.multiple_of(ci * c, c), c)
        o = oin_ref[rows, :] + lax.dot_general(qg_ref[rows, :], state_t.astype(BF16), NT_DIMS,
                                               preferred_element_type=F32)
        gv = g_ref[rows, :]
        o = o * lax.rsqrt(jnp.mean(o * o, axis=1, keepdims=True) + RMS_EPS) * gnorm
        o_ref[rows, :] = (o * (gv * _sigmoid(gv))).astype(o_ref.dtype)
        return dl_ref[pl.ds(ci, 1), :] * state_t + u_ref[ci]

    lax.fori_loop(0, n_chunks, recur, jnp.zeros((HEAD_DIM, HEAD_DIM), F32), unroll=HGRN_UNROLL)


def _hgrn_mix(proj, lb_raw, g_norm, *, layer, batch, seq):
    t, d4 = proj.shape
    d = d4 // 4
    n_heads = d // HEAD_DIM
    n_chunks = seq // HGRN_CHUNK
    assert seq % (HGRN_CHUNK * HGRN_UNROLL) == 0
    sums, upper, pair = _hgrn_constants()
    n_layers = lb_raw.shape[0]
    part = lambda off: pl.BlockSpec((seq, HEAD_DIM), lambda b, h: (b, off + h))
    whole = lambda a: pl.BlockSpec(a.shape, lambda b, h: (0,) * a.ndim)
    return pl.pallas_call(
        functools.partial(_hgrn_kernel, layer=layer, n_chunks=n_chunks),
        grid=(batch, n_heads),
        in_specs=[part(0), part(n_heads), part(2 * n_heads), part(3 * n_heads),
                  pl.BlockSpec((n_layers, HEAD_DIM), lambda b, h: (0, h)),
                  pl.BlockSpec((1, HEAD_DIM), lambda b, h: (0, 0)),
                  whole(sums), whole(upper), whole(pair)],
        out_specs=pl.BlockSpec((seq, HEAD_DIM), lambda b, h: (b, h)),
        out_shape=jax.ShapeDtypeStruct((t, d), BF16),
        scratch_shapes=[pltpu.VMEM((seq, HEAD_DIM), BF16),
                        pltpu.VMEM((seq, HEAD_DIM), F32),
                        pltpu.VMEM((n_chunks, HEAD_DIM, HEAD_DIM), F32),
                        pltpu.VMEM((n_chunks, HEAD_DIM), F32)],
        compiler_params=_params("parallel", "parallel"),
        name="hgrn_recurrence",
    )(proj, proj, proj, proj, lb_raw, g_norm.reshape(1, HEAD_DIM),
      jnp.asarray(sums, BF16), jnp.asarray(upper), jnp.asarray(pair))


def _router_kernel(x_ref, w_ref, b_ref, ids_ref, wts_ref):
    g, eg = N_GROUPS, EXPERTS_PER_GROUP
    logits = jnp.dot(x_ref[...], w_ref[...], precision=lax.Precision.HIGHEST,
                     preferred_element_type=F32) + b_ref[...]
    lane = lax.broadcasted_iota(I32, logits.shape, 1)

    def softmax_over(mask):
        z = jnp.where(mask, logits, -jnp.inf)
        ez = jnp.exp(z - jnp.max(z, axis=1, keepdims=True))
        return ez / jnp.sum(ez, axis=1, keepdims=True)

    def top1(p, mask):
        best = jnp.max(jnp.where(mask, p, -1.0), axis=1, keepdims=True)
        where = jnp.min(jnp.where(mask & (p == best), lane, LANES), axis=1, keepdims=True)
        return best, where

    is_group = lane < g
    g_w, g_idx = top1(softmax_over(is_group), is_group)
    lo = g + g_idx * eg
    in_group = (lane >= lo) & (lane < lo + eg)
    pe = softmax_over(in_group)
    w1, i1 = top1(pe, in_group)
    w2, i2 = top1(pe, in_group & (lane != i1))
    den = w1 + w2
    wt1, wt2 = g_w * (w1 / den), g_w * (w2 / den)
    ids_ref[...] = jnp.where(lane == 0, i1 - g, jnp.where(lane == 1, i2 - g, 0))
    wts_ref[...] = jnp.where(lane == 0, wt1, jnp.where(lane == 1, wt2, 0.0))


def _router(x, w_r, b_r, *, tm):
    t, d = x.shape
    tm = min(tm, t)
    assert t % tm == 0
    row = lambda i: (i, 0)
    fixed = lambda i: (0, 0)
    return pl.pallas_call(
        _router_kernel,
        grid=(t // tm,),
        in_specs=[pl.BlockSpec((tm, d), row), pl.BlockSpec((d, LANES), fixed), pl.BlockSpec((1, LANES), fixed)],
        out_specs=[pl.BlockSpec((tm, LANES), row), pl.BlockSpec((tm, LANES), row)],
        out_shape=[jax.ShapeDtypeStruct((t, LANES), I32), jax.ShapeDtypeStruct((t, LANES), F32)],
        compiler_params=_params("parallel"),
        name="moe_router",
    )(x, w_r, b_r)


def _expert_kernel(blk_expert_ref, row_src_ref, row_dst_ref, n_used_ref, x_hbm, wg_ref, wu_ref, wd_ref, y_hbm,
                   xbuf, ybuf, in_sems, out_sems):
    del blk_expert_ref
    i = pl.program_id(0)
    n_blk = pl.num_programs(0)
    rows = xbuf.shape[1]
    n_used = n_used_ref[0]
    active = i < n_used
    slot = i % 2

    def gather(block, s):
        for r in range(rows):
            tok = row_src_ref[block * rows + r]
            pltpu.make_async_copy(x_hbm.at[pl.ds(tok, 1), :], xbuf.at[s, pl.ds(r, 1), :], in_sems.at[s]).start()

    def scatter(block, s):
        for r in range(rows):
            dst = row_dst_ref[block * rows + r]
            pltpu.make_async_copy(ybuf.at[s, pl.ds(r, 1), :], y_hbm.at[pl.ds(dst, 1), :], out_sems.at[s]).start()

    def wait_gather(s):
        pltpu.make_async_copy(x_hbm.at[pl.ds(0, rows), :], xbuf.at[s], in_sems.at[s]).wait()

    def wait_scatter(s):
        pltpu.make_async_copy(ybuf.at[s], y_hbm.at[pl.ds(0, rows), :], out_sems.at[s]).wait()

    @pl.when(i == 0)
    def _():
        gather(0, 0)

    @pl.when(active & (i >= 2))
    def _():
        wait_scatter(slot)

    @pl.when(active)
    def _():
        wait_gather(slot)
        gather(jnp.minimum(i + 1, n_blk - 1), 1 - slot)
        xb = xbuf[slot].astype(BF16)
        hg = jnp.dot(xb, wg_ref[...], preferred_element_type=F32)
        hu = jnp.dot(xb, wu_ref[...], preferred_element_type=F32)
        h = (hg * _sigmoid(hg)) * hu
        ybuf[slot] = jnp.dot(h.astype(BF16), wd_ref[...], preferred_element_type=F32)
        scatter(i, slot)

    last_grid_step = active & (i == n_blk - 1)
    first_idle_step = i == n_used

    @pl.when(last_grid_step | first_idle_step)
    def _():
        wait_gather(jnp.where(active, 1 - slot, slot))
        wait_scatter(jnp.where(active, slot, 1 - slot))

    @pl.when((last_grid_step & (i >= 1)) | (first_idle_step & (i >= 2)))
    def _():
        wait_scatter(jnp.where(active, 1 - slot, slot))


def _expert_mlp(x, blk_expert, row_src, row_dst, n_used, w_gate, w_up, w_down):
    t, d = x.shape
    n_e, _, f = w_gate.shape
    n_blk = blk_expert.shape[0]
    rows = MOE_ROWS
    by_expert = lambda i, be, rs, rd, nu: (be[i], 0, 0)
    grid_spec = pltpu.PrefetchScalarGridSpec(
        num_scalar_prefetch=4,
        grid=(n_blk,),
        in_specs=[pl.BlockSpec(memory_space=pl.ANY),
                  pl.BlockSpec((None, d, f), by_expert), pl.BlockSpec((None, d, f), by_expert),
                  pl.BlockSpec((None, f, d), by_expert)],
        out_specs=pl.BlockSpec(memory_space=pl.ANY),
        scratch_shapes=[pltpu.VMEM((2, rows, d), F32), pltpu.VMEM((2, rows, d), F32),
                        pltpu.SemaphoreType.DMA((2,)), pltpu.SemaphoreType.DMA((2,))],
    )
    return pl.pallas_call(
        _expert_kernel,
        grid_spec=grid_spec,
        out_shape=jax.ShapeDtypeStruct((n_blk * rows, d), F32),
        compiler_params=_params("arbitrary"),
        name="moe_experts",
    )(blk_expert, row_src, row_dst, n_used, x, w_gate, w_up, w_down)


def _combine_ln_kernel(y_ref, x_ref, wts_ref, g_ref, b_ref, xo_ref, xb_ref, *, alpha):
    d = x_ref.shape[1]
    wts = wts_ref[...]
    m = wts[:, 0:1] * y_ref[:, 0:d] + wts[:, 1:2] * y_ref[:, d:2 * d]
    out = _layer_norm(alpha * x_ref[...] + m, g_ref[...], b_ref[...])
    xo_ref[...] = out
    xb_ref[...] = out.astype(BF16)


def _combine_ln(y_pairs, x, wts, g, b, *, alpha, tm):
    t, d = x.shape
    tm = min(tm, t)
    assert t % tm == 0
    row = lambda i: (i, 0)
    fixed = lambda i: (0, 0)
    return pl.pallas_call(
        functools.partial(_combine_ln_kernel, alpha=alpha),
        grid=(t // tm,),
        in_specs=[pl.BlockSpec((tm, TOP_E * d), row), pl.BlockSpec((tm, d), row), pl.BlockSpec((tm, LANES), row),
                  pl.BlockSpec((1, d), fixed), pl.BlockSpec((1, d), fixed)],
        out_specs=[pl.BlockSpec((tm, d), row), pl.BlockSpec((tm, d), row)],
        out_shape=[jax.ShapeDtypeStruct((t, d), F32), jax.ShapeDtypeStruct((t, d), BF16)],
        compiler_params=_params("parallel"),
        name="moe_combine_ln",
    )(y_pairs, x, wts, g.reshape(1, d), b.reshape(1, d))


def _dispatch_plan(ids, n_tokens):
    rows = MOE_ROWS
    n_assign = n_tokens * TOP_E
    eid = ids[:, :TOP_E].reshape(n_assign)
    onehot = (eid[:, None] == jnp.arange(N_EXPERTS, dtype=I32)[None, :]).astype(I32)
    csum = jnp.cumsum(onehot, axis=0)
    rank = jnp.sum(onehot * csum, axis=1) - 1
    counts = csum[-1]
    padded = (counts + rows - 1) // rows * rows
    pends = jnp.cumsum(padded)
    pstarts = pends - padded
    pos = (pstarts[eid] + rank).astype(I32)
    n_blk = n_assign // rows + N_EXPERTS
    assign = jnp.full((n_blk * rows,), -1, I32).at[pos].set(jnp.arange(n_assign, dtype=I32))
    is_pad = assign < 0
    row_src = jnp.where(is_pad, 0, assign // TOP_E)
    row_dst = jnp.where(is_pad, n_assign - 1 + jnp.cumsum(is_pad.astype(I32)), assign)
    blk_expert = jnp.minimum(jnp.searchsorted(pends, jnp.arange(n_blk, dtype=I32) * rows, side='right'),
                             N_EXPERTS - 1).astype(I32)
    n_used = (pends[-1] // rows).astype(I32).reshape(1)
    return row_src, row_dst, blk_expert, n_used


def _moe_ln(x, x_b, w_r, b_r, w_gate, w_up, w_down, g, b, *, alpha):
    del x_b
    t, d = x.shape
    ids, wts = _router(x, w_r, b_r, tm=256)
    row_src, row_dst, blk_expert, n_used = _dispatch_plan(ids, t)
    y_rows = _expert_mlp(x, blk_expert, row_src, row_dst, n_used, w_gate, w_up, w_down)
    y_pairs = y_rows.reshape(y_rows.shape[0] // TOP_E, TOP_E * d)
    return _combine_ln(y_pairs, x, wts, g, b, alpha=alpha, tm=256)


def kernel(x, moba_w_in, moba_w_o, hgrn_w_in, hgrn_g_norm, hgrn_lb_raw, hgrn_w_o, ln_mix_g, ln_mix_b,
           moe_w_rg, moe_b_rg, moe_w_re, moe_b_re, moe_w_gate, moe_w_up, moe_w_down, ln_ffn_g, ln_ffn_b):
    batch, seq, d = x.shape
    depth = ln_mix_g.shape[0]
    alpha = (2 * depth) ** 0.25
    t = batch * seq
    xf = x.reshape(t, d)
    xb = xf.astype(BF16)
    n_re = N_GROUPS * EXPERTS_PER_GROUP
    for layer in range(depth):
        j = layer // 2
        if layer % 2 == 0:
            qkv = _matmul(xb, moba_w_in[j].astype(BF16), BF16, tm=1024, tn=512,
                          scaled_cols=d, scale=HEAD_DIM ** -0.5)
            o = _moba_attention(qkv, batch, seq)
            w_o = moba_w_o[j]
        else:
            proj = _matmul(xb, hgrn_w_in[j].astype(BF16), F32, tm=1024, tn=512)
            o = _hgrn_mix(proj, hgrn_lb_raw, hgrn_g_norm[j], layer=j, batch=batch, seq=seq)
            w_o = hgrn_w_o[j]
        xf, xb = _proj_ln(o, w_o.astype(BF16), xf, ln_mix_g[layer], ln_mix_b[layer], alpha=alpha, tm=256)
        w_re = jnp.transpose(moe_w_re[layer], (1, 0, 2)).reshape(d, n_re)
        w_r = jnp.concatenate([moe_w_rg[layer], w_re, jnp.zeros((d, LANES - N_GROUPS - n_re), F32)], axis=1)
        b_r = jnp.concatenate([moe_b_rg[layer], moe_b_re[layer].reshape(n_re),
                               jnp.zeros((LANES - N_GROUPS - n_re,), F32)]).reshape(1, LANES)
        xf, xb = _moe_ln(xf, xb, w_r, b_r, moe_w_gate[layer].astype(BF16), moe_w_up[layer].astype(BF16),
                         moe_w_down[layer].astype(BF16), ln_ffn_g[layer], ln_ffn_b[layer], alpha=alpha)
    return xf.reshape(batch, seq, d)
```

```python
import functools

import numpy as np
import jax
import jax.numpy as jnp
from jax import lax
from jax.experimental import pallas as pl
from jax.experimental.pallas import tpu as pltpu

F32 = jnp.float32
BF16 = jnp.bfloat16
I32 = jnp.int32

LANES = 128
SUBLANES = 8
VMEM_LIMIT_BYTES = 56 << 20

HEAD_DIM = 128
MOBA_BLOCK = 256
MOBA_TOPK = 3
HGRN_CHUNK = 64
N_GROUPS = 4
EXPERTS_PER_GROUP = 8
N_EXPERTS = N_GROUPS * EXPERTS_PER_GROUP
TOP_E = 2
LN_EPS = 1e-5
RMS_EPS = 1e-6

MOE_ROWS = 256
NT_DIMS = (((1,), (1,)), ((), ()))
TN_DIMS = (((0,), (0,)), ((), ()))


def _params(*semantics):
    return pltpu.CompilerParams(dimension_semantics=semantics, vmem_limit_bytes=VMEM_LIMIT_BYTES)


def _sigmoid(x):
    return 1.0 / (1.0 + jnp.exp(-x))


def _layer_norm(y, g, b):
    mu = jnp.mean(y, axis=-1, keepdims=True)
    d = y - mu
    var = jnp.mean(d * d, axis=-1, keepdims=True)
    return d * lax.rsqrt(var + LN_EPS) * g + b


def _matmul_kernel(x_ref, w_ref, o_ref, *, n_scaled_blocks, scale):
    acc = jnp.dot(x_ref[...], w_ref[...], preferred_element_type=F32)
    if n_scaled_blocks:
        acc = acc * jnp.where(pl.program_id(1) < n_scaled_blocks, scale, 1.0).astype(F32)
    o_ref[...] = acc.astype(o_ref.dtype)


def _matmul(x, w, out_dtype, *, tm, tn, scaled_cols=0, scale=1.0):
    m, k = x.shape
    n = w.shape[1]
    tm, tn = min(tm, m), min(tn, n)
    assert m % tm == 0 and n % tn == 0 and scaled_cols % tn == 0
    kern = functools.partial(_matmul_kernel, n_scaled_blocks=scaled_cols // tn, scale=scale)
    return pl.pallas_call(
        kern,
        grid=(m // tm, n // tn),
        in_specs=[pl.BlockSpec((tm, k), lambda i, j: (i, 0)),
                  pl.BlockSpec((k, tn), lambda i, j: (0, j))],
        out_specs=pl.BlockSpec((tm, tn), lambda i, j: (i, j)),
        out_shape=jax.ShapeDtypeStruct((m, n), out_dtype),
        compiler_params=_params("parallel", "parallel"),
        name="proj_matmul",
    )(x, w)


def _proj_ln_kernel(o_ref, w_ref, x_ref, g_ref, b_ref, xo_ref, xb_ref, *, alpha):
    h = jnp.dot(o_ref[...], w_ref[...], preferred_element_type=F32)
    out = _layer_norm(alpha * x_ref[...] + h, g_ref[...], b_ref[...])
    xo_ref[...] = out
    xb_ref[...] = out.astype(BF16)


def _proj_ln(o, w, x, g, b, *, alpha, tm):
    t, d = x.shape
    tm = min(tm, t)
    assert t % tm == 0
    row = lambda i: (i, 0)
    fixed = lambda i: (0, 0)
    return pl.pallas_call(
        functools.partial(_proj_ln_kernel, alpha=alpha),
        grid=(t // tm,),
        in_specs=[pl.BlockSpec((tm, d), row), pl.BlockSpec((d, d), fixed), pl.BlockSpec((tm, d), row),
                  pl.BlockSpec((1, d), fixed), pl.BlockSpec((1, d), fixed)],
        out_specs=[pl.BlockSpec((tm, d), row), pl.BlockSpec((tm, d), row)],
        out_shape=[jax.ShapeDtypeStruct((t, d), F32), jax.ShapeDtypeStruct((t, d), BF16)],
        compiler_params=_params("parallel"),
        name="proj_ln",
    )(o, w, x, g.reshape(1, d), b.reshape(1, d))


MOBA_HEADS_PER_STEP = 4
MOBA_TILE = 2 * MOBA_BLOCK


def _moba_kernel(q_ref, k_ref, v_ref, o_ref, kmean_ref, vt_ref, sel_ref, *, n_blocks):
    blk, tile, hd = MOBA_BLOCK, MOBA_TILE, HEAD_DIM
    heads = range(MOBA_HEADS_PER_STEP)
    i = pl.program_id(2)

    @pl.when(i == 0)
    def _():
        kmean_ref[...] = jnp.zeros_like(kmean_ref)
        for h in heads:
            cols = slice(h * hd, (h + 1) * hd)
            for j in range(n_blocks):
                kj = k_ref[pl.ds(j * blk, blk), cols].astype(F32)
                kmean_ref[h, pl.ds(j, 1), :] = jnp.mean(kj, axis=0, keepdims=True)
            for p in range(n_blocks // 2):
                vt_ref[h, p] = v_ref[pl.ds(p * tile, tile), cols].T

    q = [q_ref[:, h * hd:(h + 1) * hd] for h in heads]
    for h in heads:
        gate = lax.dot_general(kmean_ref[h], q[h].astype(F32), NT_DIMS,
                               precision=lax.Precision.HIGHEST, preferred_element_type=F32)
        blk_idx = lax.broadcasted_iota(I32, gate.shape, 0)
        gate = jnp.where(blk_idx < i, gate, -jnp.inf)
        rank = jnp.zeros(gate.shape, I32)
        for jp in range(n_blocks):
            row = gate[jp:jp + 1, :]
            beats = (row > gate) | ((row == gate) & (jp < blk_idx))
            rank = rank + jnp.where(beats, 1, 0)
        sel_ref[h] = jnp.where((gate > -jnp.inf) & (rank < MOBA_TOPK), 1.0, 0.0)

    def scores(h, p):
        kt = k_ref[pl.ds(pl.multiple_of(p * tile, tile), tile), h * hd:(h + 1) * hd]
        return lax.dot_general(kt, q[h], NT_DIMS, preferred_element_type=F32)

    def masked(s, keep_top, keep_bot):
        return jnp.concatenate([jnp.where(keep_top, s[:blk], -jnp.inf),
                                jnp.where(keep_bot, s[blk:], -jnp.inf)], axis=0)

    p_own = i // 2
    causal = lax.broadcasted_iota(I32, (blk, blk), 0) <= lax.broadcasted_iota(I32, (blk, blk), 1)
    i_odd = jnp.full((blk, blk), i % 2, I32) == 1
    s_all = [scores(h, p_own) for h in heads]
    state, p_all = [], []
    for h in heads:
        chosen_prev = sel_ref[h, pl.ds(2 * p_own, 1), :] > 0.0
        s = masked(s_all[h], (i_odd & chosen_prev) | (~i_odd & causal), i_odd & causal)
        m = jnp.max(s, axis=0, keepdims=True)
        p = jnp.exp(s - m)
        state += [m, jnp.sum(p, axis=0, keepdims=True)]
        p_all.append(p.astype(BF16))
    for h in heads:
        acc = jnp.dot(vt_ref[h, p_own], p_all[h], preferred_element_type=F32)
        state.insert(3 * h + 2, acc)

    def body(pt, carry):
        s_all = [scores(h, pt) for h in heads]
        out, p_all, a_all = [], [], []
        for h in heads:
            m, l, _ = carry[3 * h:3 * h + 3]
            s = masked(s_all[h], sel_ref[h, pl.ds(2 * pt, 1), :] > 0.0,
                       sel_ref[h, pl.ds(2 * pt + 1, 1), :] > 0.0)
            m_new = jnp.maximum(m, jnp.max(s, axis=0, keepdims=True))
            a = jnp.exp(m - m_new)
            p = jnp.exp(s - m_new)
            out += [m_new, a * l + jnp.sum(p, axis=0, keepdims=True)]
            p_all.append(p.astype(BF16))
            a_all.append(a)
        for h in heads:
            acc = a_all[h] * carry[3 * h + 2] + jnp.dot(vt_ref[h, pt], p_all[h], preferred_element_type=F32)
            out.insert(3 * h + 2, acc)
        return tuple(out)

    state = lax.fori_loop(0, p_own, body, tuple(state))
    for h in heads:
        _, l, acc = state[3 * h:3 * h + 3]
        o_ref[:, h * hd:(h + 1) * hd] = (acc / l).T.astype(o_ref.dtype)


def _moba_attention(qkv, batch, seq):
    t, d3 = qkv.shape
    d = d3 // 3
    hs = MOBA_HEADS_PER_STEP
    n_head_groups = d // (HEAD_DIM * hs)
    n_blocks = seq // MOBA_BLOCK
    assert seq % MOBA_TILE == 0 and d % (HEAD_DIM * hs) == 0
    nb_pad = -(-n_blocks // SUBLANES) * SUBLANES
    kv_spec = lambda off: pl.BlockSpec((seq, HEAD_DIM * hs), lambda b, h, i: (b, off + h))
    q_spec = pl.BlockSpec((MOBA_BLOCK, HEAD_DIM * hs), lambda b, h, i: (b * n_blocks + i, h))
    return pl.pallas_call(
        functools.partial(_moba_kernel, n_blocks=n_blocks),
        grid=(batch, n_head_groups, n_blocks),
        in_specs=[q_spec, kv_spec(n_head_groups), kv_spec(2 * n_head_groups)],
        out_specs=q_spec,
        out_shape=jax.ShapeDtypeStruct((t, d), BF16),
        scratch_shapes=[pltpu.VMEM((hs, nb_pad, HEAD_DIM), F32),
                        pltpu.VMEM((hs, n_blocks // 2, HEAD_DIM, MOBA_TILE), BF16),
                        pltpu.VMEM((hs, nb_pad, MOBA_BLOCK), F32)],
        compiler_params=_params("parallel", "parallel", "arbitrary"),
        name="moba_attention",
    )(qkv, qkv, qkv)


_HGRN_LEVELS = (32, 16, 8, 4, 2, 1)
HGRN_GROUP = 4


def _hgrn_constants():
    c = HGRN_CHUNK
    idx = np.arange(c)
    t, s = idx[:, None], idx[None, :]
    sums = [(s <= t)]
    upper, pair = [], []
    for m in _HGRN_LEVELS:
        same = (t // (2 * m)) == (s // (2 * m))
        t_up, s_up = (t % (2 * m)) >= m, (s % (2 * m)) >= m
        up_rows = same & t_up & s_up & (s <= t)
        lo_rows = same & ~t_up & ~s_up & (s > t)
        sums.append(up_rows | lo_rows)
        upper.append(np.broadcast_to(t_up, (c, LANES)))
        pair.append(same & t_up & ~s_up)
    return (np.concatenate(sums, 0).astype(np.float32),
            np.stack(upper).astype(np.float32), np.stack(pair).astype(np.float32))


def _hgrn_kernel(q_ref, f_ref, i_ref, g_ref, lbraw_ref, gn_ref, sums_ref, upper_ref, pair_ref, o_ref,
                 qg_ref, oin_ref, u_ref, dl_ref, *, layer, n_chunks):
    c = HGRN_CHUNK
    dk = HEAD_DIM
    raw = lbraw_ref[...]
    e = jnp.exp(raw - jnp.max(raw, axis=0, keepdims=True))
    sm = e / jnp.sum(e, axis=0, keepdims=True)
    cum = sm[0:1, :]
    for l in range(1, layer + 1):
        cum = cum + sm[l:l + 1, :]
    lb = cum - sm[0:1, :]
    gnorm = gn_ref[...]
    eye = (lax.broadcasted_iota(I32, (c, c), 0) == lax.broadcasted_iota(I32, (c, c), 1))

    group = range(HGRN_GROUP)
    chunk = lambda x, n: x[n * c:(n + 1) * c]
    n_levels = len(_HGRN_LEVELS)

    def local(gi, carry):
        rows = pl.ds(pl.multiple_of(gi * (HGRN_GROUP * c), HGRN_GROUP * c), HGRN_GROUP * c)
        qv, fv, iv = q_ref[rows, :], f_ref[rows, :], i_ref[rows, :]
        qs = qv * _sigmoid(qv)
        fg = lb + (1.0 - lb) * _sigmoid(fv)
        log_f = jnp.log(fg)
        kk = 1.0 - fg
        iv_b = iv.astype(BF16)
        hi = log_f.astype(BF16)
        rest = log_f - hi.astype(F32)
        mid = rest.astype(BF16)
        lo = (rest - mid.astype(F32)).astype(BF16)
        parts = jnp.concatenate([chunk(t, n) for n in group for t in (hi, mid, lo)], axis=1)
        s3 = jnp.dot(sums_ref[...], parts, preferred_element_type=F32)
        sums = [(s3[:, (3 * n) * dk:(3 * n + 1) * dk] + s3[:, (3 * n + 1) * dk:(3 * n + 2) * dk])
                + s3[:, (3 * n + 2) * dk:(3 * n + 3) * dk] for n in group]
        gcum = jnp.concatenate([s[0:c, :] for s in sums], axis=0)
        qg_ref[rows, :] = (qs * jnp.exp(gcum)).astype(BF16)
        qe, ke = [], []
        for n in group:
            for li in range(n_levels):
                dec = jnp.exp(sums[n][(li + 1) * c:(li + 2) * c, :])
                is_up = upper_ref[li] > 0.0
                qe.append(jnp.where(is_up, chunk(qs, n) * dec, 0.0).astype(BF16))
                ke.append(jnp.where(is_up, 0.0, chunk(kk, n) * dec).astype(BF16))
        prods = [lax.dot_general(a, b, NT_DIMS, preferred_element_type=F32) for a, b in zip(qe, ke)]
        diag = jnp.sum(qs * kk, axis=1, keepdims=True)
        a_all = []
        for n in group:
            a = jnp.where(eye, chunk(diag, n), 0.0)
            for li in range(n_levels):
                a = a + jnp.where(pair_ref[li] > 0.0, prods[n * n_levels + li], 0.0)
            a_all.append(a.astype(BF16))
        o_in = [jnp.dot(a_all[n], chunk(iv_b, n), preferred_element_type=F32) for n in group]
        oin_ref[rows, :] = jnp.concatenate(o_in, axis=0)
        g_last = [gcum[(n + 1) * c - 1:(n + 1) * c, :] for n in group]
        k_dec = [(chunk(kk, n) * jnp.exp(g_last[n] - chunk(gcum, n))).astype(BF16) for n in group]
        for n in group:
            u_ref[gi * HGRN_GROUP + n] = lax.dot_general(chunk(iv_b, n), k_dec[n], TN_DIMS,
                                                         preferred_element_type=F32)
        dl_ref[pl.ds(pl.multiple_of(gi * SUBLANES, SUBLANES), SUBLANES), :] = jnp.concatenate(
            [jnp.exp(g) for g in g_last] + [jnp.zeros((SUBLANES - HGRN_GROUP, dk), F32)], axis=0)
        return carry

    lax.fori_loop(0, n_chunks // HGRN_GROUP, local, 0)

    def recur(gi, state_t):
        rows = pl.ds(pl.multiple_of(gi * (HGRN_GROUP * c), HGRN_GROUP * c), HGRN_GROUP * c)
        decay = dl_ref[pl.ds(pl.multiple_of(gi * SUBLANES, SUBLANES), SUBLANES), :]
        states = []
        for n in group:
            states.append(state_t.astype(BF16))
            state_t = decay[n:n + 1, :] * state_t + u_ref[gi * HGRN_GROUP + n]
        qg = qg_ref[rows, :]
        o = oin_ref[rows, :] + jnp.concatenate(
            [lax.dot_general(chunk(qg, n), states[n], NT_DIMS, preferred_element_type=F32) for n in group], axis=0)
        gv = g_ref[rows, :]
        o = o * lax.rsqrt(jnp.mean(o * o, axis=1, keepdims=True) + RMS_EPS) * gnorm
        o_ref[rows, :] = (o * (gv * _sigmoid(gv))).astype(o_ref.dtype)
        return state_t

    lax.fori_loop(0, n_chunks // HGRN_GROUP, recur, jnp.zeros((HEAD_DIM, HEAD_DIM), F32))


def _hgrn_mix(proj, lb_raw, g_norm, *, layer, batch, seq):
    t, d4 = proj.shape
    d = d4 // 4
    n_heads = d // HEAD_DIM
    n_chunks = seq // HGRN_CHUNK
    assert seq % (HGRN_CHUNK * HGRN_GROUP) == 0 and HGRN_GROUP <= SUBLANES
    sums, upper, pair = _hgrn_constants()
    n_layers = lb_raw.shape[0]
    part = lambda off: pl.BlockSpec((seq, HEAD_DIM), lambda b, h: (b, off + h))
    whole = lambda a: pl.BlockSpec(a.shape, lambda b, h: (0,) * a.ndim)
    return pl.pallas_call(
        functools.partial(_hgrn_kernel, layer=layer, n_chunks=n_chunks),
        grid=(batch, n_heads),
        in_specs=[part(0), part(n_heads), part(2 * n_heads), part(3 * n_heads),
                  pl.BlockSpec((n_layers, HEAD_DIM), lambda b, h: (0, h)),
                  pl.BlockSpec((1, HEAD_DIM), lambda b, h: (0, 0)),
                  whole(sums), whole(upper), whole(pair)],
        out_specs=pl.BlockSpec((seq, HEAD_DIM), lambda b, h: (b, h)),
        out_shape=jax.ShapeDtypeStruct((t, d), BF16),
        scratch_shapes=[pltpu.VMEM((seq, HEAD_DIM), BF16),
                        pltpu.VMEM((seq, HEAD_DIM), F32),
                        pltpu.VMEM((n_chunks, HEAD_DIM, HEAD_DIM), F32),
                        pltpu.VMEM((n_chunks // HGRN_GROUP * SUBLANES, HEAD_DIM), F32)],
        compiler_params=_params("parallel", "parallel"),
        name="hgrn_recurrence",
    )(proj, proj, proj, proj, lb_raw, g_norm.reshape(1, HEAD_DIM),
      jnp.asarray(sums, BF16), jnp.asarray(upper), jnp.asarray(pair))


def _router_kernel(x_ref, w_ref, b_ref, ids_ref, wts_ref):
    g, eg = N_GROUPS, EXPERTS_PER_GROUP
    logits = jnp.dot(x_ref[...], w_ref[...], precision=lax.Precision.HIGHEST,
                     preferred_element_type=F32) + b_ref[...]
    lane = lax.broadcasted_iota(I32, logits.shape, 1)

    def softmax_over(mask):
        z = jnp.where(mask, logits, -jnp.inf)
        ez = jnp.exp(z - jnp.max(z, axis=1, keepdims=True))
        return ez / jnp.sum(ez, axis=1, keepdims=True)

    def top1(p, mask):
        best = jnp.max(jnp.where(mask, p, -1.0), axis=1, keepdims=True)
        where = jnp.min(jnp.where(mask & (p == best), lane, LANES), axis=1, keepdims=True)
        return best, where

    is_group = lane < g
    g_w, g_idx = top1(softmax_over(is_group), is_group)
    lo = g + g_idx * eg
    in_group = (lane >= lo) & (lane < lo + eg)
    pe = softmax_over(in_group)
    w1, i1 = top1(pe, in_group)
    w2, i2 = top1(pe, in_group & (lane != i1))
    den = w1 + w2
    wt1, wt2 = g_w * (w1 / den), g_w * (w2 / den)
    ids_ref[...] = jnp.where(lane == 0, i1 - g, jnp.where(lane == 1, i2 - g, 0))
    wts_ref[...] = jnp.where(lane == 0, wt1, jnp.where(lane == 1, wt2, 0.0))


def _router(x, w_r, b_r, *, tm):
    t, d = x.shape
    tm = min(tm, t)
    assert t % tm == 0
    row = lambda i: (i, 0)
    fixed = lambda i: (0, 0)
    return pl.pallas_call(
        _router_kernel,
        grid=(t // tm,),
        in_specs=[pl.BlockSpec((tm, d), row), pl.BlockSpec((d, LANES), fixed), pl.BlockSpec((1, LANES), fixed)],
        out_specs=[pl.BlockSpec((tm, LANES), row), pl.BlockSpec((tm, LANES), row)],
        out_shape=[jax.ShapeDtypeStruct((t, LANES), I32), jax.ShapeDtypeStruct((t, LANES), F32)],
        compiler_params=_params("parallel"),
        name="moe_router",
    )(x, w_r, b_r)


def _expert_kernel(blk_expert_ref, row_src_ref, row_dst_ref, n_used_ref, x_hbm, wg_ref, wu_ref, wd_ref, y_hbm,
                   xbuf0, xbuf1, ybuf0, ybuf1, in_sems, out_sems, fill_sem):
    del blk_expert_ref
    i = pl.program_id(0)
    n_blk = pl.num_programs(0)
    rows = xbuf0.shape[0]
    n_used = n_used_ref[0]
    active = i < n_used
    xbufs, ybufs = (xbuf0, xbuf1), (ybuf0, ybuf1)

    def gather(block, s):
        for r in range(rows):
            tok = row_src_ref[block * rows + r]
            pltpu.make_async_copy(x_hbm.at[pl.ds(tok, 1), :], xbufs[s].at[pl.ds(r, 1), :], in_sems.at[s]).start()

    def scatter(block, s):
        for r in range(rows):
            dst = row_dst_ref[block * rows + r]
            pltpu.make_async_copy(ybufs[s].at[pl.ds(r, 1), :], y_hbm.at[pl.ds(dst, 1), :], out_sems.at[s]).start()

    def wait_gather(s):
        pltpu.make_async_copy(x_hbm.at[pl.ds(0, rows), :], xbufs[s], in_sems.at[s]).wait()

    def wait_scatter(s):
        pltpu.make_async_copy(ybufs[s], y_hbm.at[pl.ds(0, rows), :], out_sems.at[s]).wait()

    def mlp(s):
        xb = xbufs[s][...].astype(BF16)
        hg = jnp.dot(xb, wg_ref[...], preferred_element_type=F32)
        hu = jnp.dot(xb, wu_ref[...], preferred_element_type=F32)
        h = (hg * _sigmoid(hg)) * hu
        ybufs[s][...] = jnp.dot(h.astype(BF16), wd_ref[...], preferred_element_type=F32)

    def finish(s):
        @pl.when(i == n_blk - 1)
        def _():
            scatter(i, s)
            wait_scatter(s)
            wait_gather(1 - s)

            @pl.when(i >= 1)
            def _():
                wait_scatter(1 - s)

    @pl.when(i == 0)
    def _():
        gather(0, 0)
        wait_gather(0)
        gather(jnp.minimum(1, n_blk - 1), 1)
        mlp(0)
        finish(0)

    for s in range(2):
        @pl.when(active & (i >= 1) & (i % 2 == s))
        def _(s=s):
            wait_gather(s)

            @pl.when(i >= 2)
            def _():
                wait_scatter(s)

            gather(jnp.minimum(i + 1, n_blk - 1), 1 - s)
            scatter(i - 1, 1 - s)
            mlp(s)
            finish(s)

        @pl.when((i == n_used) & (i % 2 == s))
        def _(s=s):
            wait_gather(s)
            scatter(i - 1, 1 - s)
            wait_scatter(1 - s)

            @pl.when(i >= 2)
            def _():
                wait_scatter(s)

    @pl.when(jnp.logical_not(active))
    def _():
        ybuf0[...] = jnp.zeros_like(ybuf0)
        fill = pltpu.make_async_copy(ybuf0, y_hbm.at[pl.ds(pl.multiple_of(i * rows, rows), rows), :], fill_sem)
        fill.start()
        fill.wait()


def _expert_mlp(x, blk_expert, row_src, row_dst, n_used, w_gate, w_up, w_down):
    t, d = x.shape
    n_e, _, f = w_gate.shape
    n_blk = blk_expert.shape[0]
    rows = MOE_ROWS
    by_expert = lambda i, be, rs, rd, nu: (be[i], 0, 0)
    row_buf = pltpu.VMEM((rows, d), F32)
    grid_spec = pltpu.PrefetchScalarGridSpec(
        num_scalar_prefetch=4,
        grid=(n_blk,),
        in_specs=[pl.BlockSpec(memory_space=pl.ANY),
                  pl.BlockSpec((None, d, f), by_expert), pl.BlockSpec((None, d, f), by_expert),
                  pl.BlockSpec((None, f, d), by_expert)],
        out_specs=pl.BlockSpec(memory_space=pl.ANY),
        scratch_shapes=[row_buf, row_buf, row_buf, row_buf,
                        pltpu.SemaphoreType.DMA((2,)), pltpu.SemaphoreType.DMA((2,)), pltpu.SemaphoreType.DMA(())],
    )
    return pl.pallas_call(
        _expert_kernel,
        grid_spec=grid_spec,
        out_shape=jax.ShapeDtypeStruct((n_blk * rows, d), F32),
        compiler_params=_params("arbitrary"),
        name="moe_experts",
    )(blk_expert, row_src, row_dst, n_used, x, w_gate, w_up, w_down)


def _combine_ln_kernel(y0_ref, y1_ref, x_ref, wts_ref, g_ref, b_ref, xo_ref, xb_ref, *, alpha):
    wts = wts_ref[...]
    m = wts[:, 0:1] * y0_ref[...] + wts[:, 1:2] * y1_ref[...]
    out = _layer_norm(alpha * x_ref[...] + m, g_ref[...], b_ref[...])
    xo_ref[...] = out
    xb_ref[...] = out.astype(BF16)


def _combine_ln(y_rows, x, wts, g, b, *, alpha, tm):
    t, d = x.shape
    tm = min(tm, t)
    assert t % tm == 0
    row = lambda i: (i, 0)
    fixed = lambda i: (0, 0)
    return pl.pallas_call(
        functools.partial(_combine_ln_kernel, alpha=alpha),
        grid=(t // tm,),
        in_specs=[pl.BlockSpec((tm, d), row), pl.BlockSpec((tm, d), lambda i: (t // tm + i, 0)),
                  pl.BlockSpec((tm, d), row), pl.BlockSpec((tm, LANES), row),
                  pl.BlockSpec((1, d), fixed), pl.BlockSpec((1, d), fixed)],
        out_specs=[pl.BlockSpec((tm, d), row), pl.BlockSpec((tm, d), row)],
        out_shape=[jax.ShapeDtypeStruct((t, d), F32), jax.ShapeDtypeStruct((t, d), BF16)],
        compiler_params=_params("parallel"),
        name="moe_combine_ln",
    )(y_rows, y_rows, x, wts, g.reshape(1, d), b.reshape(1, d))


def _dispatch_plan(ids, n_tokens):
    rows = MOE_ROWS
    n_assign = n_tokens * TOP_E
    eid = ids[:, :TOP_E].reshape(n_assign)
    onehot = (eid[:, None] == jnp.arange(N_EXPERTS, dtype=I32)[None, :]).astype(I32)
    csum = jnp.cumsum(onehot, axis=0)
    rank = jnp.sum(onehot * csum, axis=1) - 1
    counts = csum[-1]
    padded = (counts + rows - 1) // rows * rows
    pends = jnp.cumsum(padded)
    pstarts = pends - padded
    pos = (pstarts[eid] + rank).astype(I32)
    n_blk = n_assign // rows + N_EXPERTS
    assign = jnp.full((n_blk * rows,), -1, I32).at[pos].set(jnp.arange(n_assign, dtype=I32))
    is_pad = assign < 0
    row_src = jnp.where(is_pad, 0, assign // TOP_E)
    row_dst = jnp.where(is_pad, n_assign - 1 + jnp.cumsum(is_pad.astype(I32)),
                        (assign % TOP_E) * n_tokens + assign // TOP_E)
    blk_expert = jnp.minimum(jnp.searchsorted(pends, jnp.arange(n_blk, dtype=I32) * rows, side='right'),
                             N_EXPERTS - 1).astype(I32)
    n_used = (pends[-1] // rows).astype(I32).reshape(1)
    return row_src, row_dst, blk_expert, n_used


def _moe_ln(x, x_b, w_r, b_r, w_gate, w_up, w_down, g, b, *, alpha):
    del x_b
    t, d = x.shape
    ids, wts = _router(x, w_r, b_r, tm=256)
    row_src, row_dst, blk_expert, n_used = _dispatch_plan(ids, t)
    y_rows = _expert_mlp(x, blk_expert, row_src, row_dst, n_used, w_gate, w_up, w_down)
    return _combine_ln(y_rows, x, wts, g, b, alpha=alpha, tm=256)


def kernel(x, moba_w_in, moba_w_o, hgrn_w_in, hgrn_g_norm, hgrn_lb_raw, hgrn_w_o, ln_mix_g, ln_mix_b,
           moe_w_rg, moe_b_rg, moe_w_re, moe_b_re, moe_w_gate, moe_w_up, moe_w_down, ln_ffn_g, ln_ffn_b):
    batch, seq, d = x.shape
    depth = ln_mix_g.shape[0]
    alpha = (2 * depth) ** 0.25
    t = batch * seq
    xf = x.reshape(t, d)
    xb = xf.astype(BF16)
    n_re = N_GROUPS * EXPERTS_PER_GROUP
    for layer in range(depth):
        j = layer // 2
        if layer % 2 == 0:
            qkv = _matmul(xb, moba_w_in[j].astype(BF16), BF16, tm=1024, tn=512,
                          scaled_cols=d, scale=HEAD_DIM ** -0.5)
            o = _moba_attention(qkv, batch, seq)
            w_o = moba_w_o[j]
        else:
            proj = _matmul(xb, hgrn_w_in[j].astype(BF16), F32, tm=1024, tn=512)
            o = _hgrn_mix(proj, hgrn_lb_raw, hgrn_g_norm[j], layer=j, batch=batch, seq=seq)
            w_o = hgrn_w_o[j]
        xf, xb = _proj_ln(o, w_o.astype(BF16), xf, ln_mix_g[layer], ln_mix_b[layer], alpha=alpha, tm=256)
        w_re = jnp.transpose(moe_w_re[layer], (1, 0, 2)).reshape(d, n_re)
        w_r = jnp.concatenate([moe_w_rg[layer], w_re, jnp.zeros((d, LANES - N_GROUPS - n_re), F32)], axis=1)
        b_r = jnp.concatenate([moe_b_rg[layer], moe_b_re[layer].reshape(n_re),
                               jnp.zeros((LANES - N_GROUPS - n_re,), F32)]).reshape(1, LANES)
        xf, xb = _moe_ln(xf, xb, w_r, b_r, moe_w_gate[layer].astype(BF16), moe_w_up[layer].astype(BF16),
                         moe_w_down[layer].astype(BF16), ln_ffn_g[layer], ln_ffn_b[layer], alpha=alpha)
    return xf.reshape(batch, seq, d)
```

```python
import functools

import numpy as np
import jax
import jax.numpy as jnp
from jax import lax
from jax.experimental import pallas as pl
from jax.experimental.pallas import tpu as pltpu

F32 = jnp.float32
BF16 = jnp.bfloat16
I32 = jnp.int32

LANES = 128
SUBLANES = 8
VMEM_LIMIT_BYTES = 56 << 20

HEAD_DIM = 128
MOBA_BLOCK = 256
MOBA_TOPK = 3
HGRN_CHUNK = 64
N_GROUPS = 4
EXPERTS_PER_GROUP = 8
N_EXPERTS = N_GROUPS * EXPERTS_PER_GROUP
TOP_E = 2
LN_EPS = 1e-5
RMS_EPS = 1e-6

MOE_ROWS = 256
NT_DIMS = (((1,), (1,)), ((), ()))
TN_DIMS = (((0,), (0,)), ((), ()))


def _params(*semantics):
    return pltpu.CompilerParams(dimension_semantics=semantics, vmem_limit_bytes=VMEM_LIMIT_BYTES)


def _sigmoid(x):
    return 1.0 / (1.0 + jnp.exp(-x))


def _layer_norm(y, g, b):
    mu = jnp.mean(y, axis=-1, keepdims=True)
    d = y - mu
    var = jnp.mean(d * d, axis=-1, keepdims=True)
    return d * lax.rsqrt(var + LN_EPS) * g + b


def _matmul_kernel(x_ref, w_ref, o_ref, *, n_scaled_blocks, scale):
    acc = jnp.dot(x_ref[...], w_ref[...], preferred_element_type=F32)
    if n_scaled_blocks:
        acc = acc * jnp.where(pl.program_id(1) < n_scaled_blocks, scale, 1.0).astype(F32)
    o_ref[...] = acc.astype(o_ref.dtype)


def _matmul(x, w, out_dtype, *, tm, tn, scaled_cols=0, scale=1.0):
    m, k = x.shape
    n = w.shape[1]
    tm, tn = min(tm, m), min(tn, n)
    assert m % tm == 0 and n % tn == 0 and scaled_cols % tn == 0
    kern = functools.partial(_matmul_kernel, n_scaled_blocks=scaled_cols // tn, scale=scale)
    return pl.pallas_call(
        kern,
        grid=(m // tm, n // tn),
        in_specs=[pl.BlockSpec((tm, k), lambda i, j: (i, 0)),
                  pl.BlockSpec((k, tn), lambda i, j: (0, j))],
        out_specs=pl.BlockSpec((tm, tn), lambda i, j: (i, j)),
        out_shape=jax.ShapeDtypeStruct((m, n), out_dtype),
        compiler_params=_params("parallel", "parallel"),
        name="proj_matmul",
    )(x, w)


def _store_slabs(slab_ref, x):
    rows, d = x.shape
    n = d // LANES
    for c in range(n):
        slab_ref[pl.ds(c, rows, stride=n), :] = x[:, c * LANES:(c + 1) * LANES]


def _load_slabs(slab_ref, rows, d):
    n = d // LANES
    return jnp.concatenate([slab_ref[pl.ds(c, rows, stride=n), :] for c in range(n)], axis=1)


def _proj_ln_kernel(o_ref, w_ref, x_ref, g_ref, b_ref, xo_ref, xs_ref, *, alpha):
    h = jnp.dot(o_ref[...], w_ref[...], preferred_element_type=F32)
    out = _layer_norm(alpha * x_ref[...] + h, g_ref[...], b_ref[...])
    xo_ref[...] = out
    _store_slabs(xs_ref, out)


def _proj_ln(o, w, x, g, b, *, alpha, tm):
    t, d = x.shape
    tm = min(tm, t)
    n = d // LANES
    assert t % tm == 0
    row = lambda i: (i, 0)
    fixed = lambda i: (0, 0)
    return pl.pallas_call(
        functools.partial(_proj_ln_kernel, alpha=alpha),
        grid=(t // tm,),
        in_specs=[pl.BlockSpec((tm, d), row), pl.BlockSpec((d, d), fixed), pl.BlockSpec((tm, d), row),
                  pl.BlockSpec((1, d), fixed), pl.BlockSpec((1, d), fixed)],
        out_specs=[pl.BlockSpec((tm, d), row), pl.BlockSpec((tm * n, LANES), row)],
        out_shape=[jax.ShapeDtypeStruct((t, d), F32), jax.ShapeDtypeStruct((t * n, LANES), F32)],
        compiler_params=_params("parallel"),
        name="proj_ln",
    )(o, w, x, g.reshape(1, d), b.reshape(1, d))


MOBA_HEADS_PER_STEP = 4
MOBA_TILE = 2 * MOBA_BLOCK


def _moba_kernel(q_ref, k_ref, v_ref, o_ref, kmean_ref, vt_ref, sel_ref, *, n_blocks):
    blk, tile, hd = MOBA_BLOCK, MOBA_TILE, HEAD_DIM
    heads = range(MOBA_HEADS_PER_STEP)
    i = pl.program_id(2)

    @pl.when(i == 0)
    def _():
        kmean_ref[...] = jnp.zeros_like(kmean_ref)
        for h in heads:
            cols = slice(h * hd, (h + 1) * hd)
            for j in range(n_blocks):
                kj = k_ref[pl.ds(j * blk, blk), cols].astype(F32)
                kmean_ref[h, pl.ds(j, 1), :] = jnp.mean(kj, axis=0, keepdims=True)
            for p in range(n_blocks // 2):
                vt_ref[h, p] = v_ref[pl.ds(p * tile, tile), cols].T

    q = [q_ref[:, h * hd:(h + 1) * hd] for h in heads]
    for h in heads:
        gate = lax.dot_general(kmean_ref[h], q[h].astype(F32), NT_DIMS,
                               precision=lax.Precision.HIGHEST, preferred_element_type=F32)
        blk_idx = lax.broadcasted_iota(I32, gate.shape, 0)
        gate = jnp.where(blk_idx < i, gate, -jnp.inf)
        rank = jnp.zeros(gate.shape, I32)
        for jp in range(n_blocks):
            row = gate[jp:jp + 1, :]
            beats = (row > gate) | ((row == gate) & (jp < blk_idx))
            rank = rank + jnp.where(beats, 1, 0)
        sel_ref[h] = jnp.where((gate > -jnp.inf) & (rank < MOBA_TOPK), 1.0, 0.0)

    def scores(h, p):
        kt = k_ref[pl.ds(pl.multiple_of(p * tile, tile), tile), h * hd:(h + 1) * hd]
        return lax.dot_general(kt, q[h], NT_DIMS, preferred_element_type=F32)

    def masked(s, keep_top, keep_bot):
        return jnp.concatenate([jnp.where(keep_top, s[:blk], -jnp.inf),
                                jnp.where(keep_bot, s[blk:], -jnp.inf)], axis=0)

    p_own = i // 2
    causal = lax.broadcasted_iota(I32, (blk, blk), 0) <= lax.broadcasted_iota(I32, (blk, blk), 1)
    i_odd = jnp.full((blk, blk), i % 2, I32) == 1
    s_all = [scores(h, p_own) for h in heads]
    state, p_all = [], []
    for h in heads:
        chosen_prev = sel_ref[h, pl.ds(2 * p_own, 1), :] > 0.0
        s = masked(s_all[h], (i_odd & chosen_prev) | (~i_odd & causal), i_odd & causal)
        m = jnp.max(s, axis=0, keepdims=True)
        p = jnp.exp(s - m)
        state += [m, jnp.sum(p, axis=0, keepdims=True)]
        p_all.append(p.astype(BF16))
    for h in heads:
        acc = jnp.dot(vt_ref[h, p_own], p_all[h], preferred_element_type=F32)
        state.insert(3 * h + 2, acc)

    def body(pt, carry):
        s_all = [scores(h, pt) for h in heads]
        out, p_all, a_all = [], [], []
        for h in heads:
            m, l, _ = carry[3 * h:3 * h + 3]
            s = masked(s_all[h], sel_ref[h, pl.ds(2 * pt, 1), :] > 0.0,
                       sel_ref[h, pl.ds(2 * pt + 1, 1), :] > 0.0)
            m_new = jnp.maximum(m, jnp.max(s, axis=0, keepdims=True))
            a = jnp.exp(m - m_new)
            p = jnp.exp(s - m_new)
            out += [m_new, a * l + jnp.sum(p, axis=0, keepdims=True)]
            p_all.append(p.astype(BF16))
            a_all.append(a)
        for h in heads:
            acc = a_all[h] * carry[3 * h + 2] + jnp.dot(vt_ref[h, pt], p_all[h], preferred_element_type=F32)
            out.insert(3 * h + 2, acc)
        return tuple(out)

    state = lax.fori_loop(0, p_own, body, tuple(state))
    for h in heads:
        _, l, acc = state[3 * h:3 * h + 3]
        o_ref[:, h * hd:(h + 1) * hd] = (acc / l).T.astype(o_ref.dtype)


def _moba_attention(qkv, batch, seq):
    t, d3 = qkv.shape
    d = d3 // 3
    hs = MOBA_HEADS_PER_STEP
    n_head_groups = d // (HEAD_DIM * hs)
    n_blocks = seq // MOBA_BLOCK
    assert seq % MOBA_TILE == 0 and d % (HEAD_DIM * hs) == 0
    nb_pad = -(-n_blocks // SUBLANES) * SUBLANES
    kv_spec = lambda off: pl.BlockSpec((seq, HEAD_DIM * hs), lambda b, h, i: (b, off + h))
    q_spec = pl.BlockSpec((MOBA_BLOCK, HEAD_DIM * hs), lambda b, h, i: (b * n_blocks + i, h))
    return pl.pallas_call(
        functools.partial(_moba_kernel, n_blocks=n_blocks),
        grid=(batch, n_head_groups, n_blocks),
        in_specs=[q_spec, kv_spec(n_head_groups), kv_spec(2 * n_head_groups)],
        out_specs=q_spec,
        out_shape=jax.ShapeDtypeStruct((t, d), BF16),
        scratch_shapes=[pltpu.VMEM((hs, nb_pad, HEAD_DIM), F32),
                        pltpu.VMEM((hs, n_blocks // 2, HEAD_DIM, MOBA_TILE), BF16),
                        pltpu.VMEM((hs, nb_pad, MOBA_BLOCK), F32)],
        compiler_params=_params("parallel", "parallel", "arbitrary"),
        name="moba_attention",
    )(qkv, qkv, qkv)


_HGRN_LEVELS = (32, 16, 8, 4, 2, 1)
HGRN_GROUP = 4


def _hgrn_constants():
    c = HGRN_CHUNK
    idx = np.arange(c)
    t, s = idx[:, None], idx[None, :]
    sums = [(s <= t)]
    upper, pair = [], []
    for m in _HGRN_LEVELS:
        same = (t // (2 * m)) == (s // (2 * m))
        t_up, s_up = (t % (2 * m)) >= m, (s % (2 * m)) >= m
        up_rows = same & t_up & s_up & (s <= t)
        lo_rows = same & ~t_up & ~s_up & (s > t)
        sums.append(up_rows | lo_rows)
        upper.append(np.broadcast_to(t_up, (c, LANES)))
        pair.append(same & t_up & ~s_up)
    return (np.concatenate(sums, 0).astype(np.float32),
            np.stack(upper).astype(np.float32), np.stack(pair).astype(np.float32))


def _hgrn_kernel(q_ref, f_ref, i_ref, g_ref, lbraw_ref, gn_ref, sums_ref, upper_ref, pair_ref, o_ref,
                 qg_ref, oin_ref, u_ref, dl_ref, *, layer, n_chunks):
    c = HGRN_CHUNK
    dk = HEAD_DIM
    raw = lbraw_ref[...]
    e = jnp.exp(raw - jnp.max(raw, axis=0, keepdims=True))
    sm = e / jnp.sum(e, axis=0, keepdims=True)
    cum = sm[0:1, :]
    for l in range(1, layer + 1):
        cum = cum + sm[l:l + 1, :]
    lb = cum - sm[0:1, :]
    gnorm = gn_ref[...]
    eye = (lax.broadcasted_iota(I32, (c, c), 0) == lax.broadcasted_iota(I32, (c, c), 1))

    group = range(HGRN_GROUP)
    chunk = lambda x, n: x[n * c:(n + 1) * c]
    n_levels = len(_HGRN_LEVELS)

    def local(gi, carry):
        rows = pl.ds(pl.multiple_of(gi * (HGRN_GROUP * c), HGRN_GROUP * c), HGRN_GROUP * c)
        qv, fv, iv = q_ref[rows, :], f_ref[rows, :], i_ref[rows, :]
        qs = qv * _sigmoid(qv)
        fg = lb + (1.0 - lb) * _sigmoid(fv)
        log_f = jnp.log(fg)
        kk = 1.0 - fg
        iv_b = iv.astype(BF16)
        hi = log_f.astype(BF16)
        rest = log_f - hi.astype(F32)
        mid = rest.astype(BF16)
        lo = (rest - mid.astype(F32)).astype(BF16)
        parts = jnp.concatenate([chunk(t, n) for n in group for t in (hi, mid, lo)], axis=1)
        s3 = jnp.dot(sums_ref[...], parts, preferred_element_type=F32)
        sums = [(s3[:, (3 * n) * dk:(3 * n + 1) * dk] + s3[:, (3 * n + 1) * dk:(3 * n + 2) * dk])
                + s3[:, (3 * n + 2) * dk:(3 * n + 3) * dk] for n in group]
        gcum = jnp.concatenate([s[0:c, :] for s in sums], axis=0)
        qg_ref[rows, :] = (qs * jnp.exp(gcum)).astype(BF16)
        qe, ke = [], []
        for n in group:
            for li in range(n_levels):
                dec = jnp.exp(sums[n][(li + 1) * c:(li + 2) * c, :])
                is_up = upper_ref[li] > 0.0
                qe.append(jnp.where(is_up, chunk(qs, n) * dec, 0.0).astype(BF16))
                ke.append(jnp.where(is_up, 0.0, chunk(kk, n) * dec).astype(BF16))
        prods = [lax.dot_general(a, b, NT_DIMS, preferred_element_type=F32) for a, b in zip(qe, ke)]
        diag = jnp.sum(qs * kk, axis=1, keepdims=True)
        a_all = []
        for n in group:
            a = jnp.where(eye, chunk(diag, n), 0.0)
            for li in range(n_levels):
                a = a + jnp.where(pair_ref[li] > 0.0, prods[n * n_levels + li], 0.0)
            a_all.append(a.astype(BF16))
        o_in = [jnp.dot(a_all[n], chunk(iv_b, n), preferred_element_type=F32) for n in group]
        oin_ref[rows, :] = jnp.concatenate(o_in, axis=0)
        g_last = [gcum[(n + 1) * c - 1:(n + 1) * c, :] for n in group]
        k_dec = [(chunk(kk, n) * jnp.exp(g_last[n] - chunk(gcum, n))).astype(BF16) for n in group]
        for n in group:
            u_ref[gi * HGRN_GROUP + n] = lax.dot_general(chunk(iv_b, n), k_dec[n], TN_DIMS,
                                                         preferred_element_type=F32)
        dl_ref[pl.ds(pl.multiple_of(gi * SUBLANES, SUBLANES), SUBLANES), :] = jnp.concatenate(
            [jnp.exp(g) for g in g_last] + [jnp.zeros((SUBLANES - HGRN_GROUP, dk), F32)], axis=0)
        return carry

    lax.fori_loop(0, n_chunks // HGRN_GROUP, local, 0)

    def recur(gi, state_t):
        rows = pl.ds(pl.multiple_of(gi * (HGRN_GROUP * c), HGRN_GROUP * c), HGRN_GROUP * c)
        decay = dl_ref[pl.ds(pl.multiple_of(gi * SUBLANES, SUBLANES), SUBLANES), :]
        states = []
        for n in group:
            states.append(state_t.astype(BF16))
            state_t = decay[n:n + 1, :] * state_t + u_ref[gi * HGRN_GROUP + n]
        qg = qg_ref[rows, :]
        o = oin_ref[rows, :] + jnp.concatenate(
            [lax.dot_general(chunk(qg, n), states[n], NT_DIMS, preferred_element_type=F32) for n in group], axis=0)
        gv = g_ref[rows, :]
        o = o * lax.rsqrt(jnp.mean(o * o, axis=1, keepdims=True) + RMS_EPS) * gnorm
        o_ref[rows, :] = (o * (gv * _sigmoid(gv))).astype(o_ref.dtype)
        return state_t

    lax.fori_loop(0, n_chunks // HGRN_GROUP, recur, jnp.zeros((HEAD_DIM, HEAD_DIM), F32))


def _hgrn_mix(proj, lb_raw, g_norm, *, layer, batch, seq):
    t, d4 = proj.shape
    d = d4 // 4
    n_heads = d // HEAD_DIM
    n_chunks = seq // HGRN_CHUNK
    assert seq % (HGRN_CHUNK * HGRN_GROUP) == 0 and HGRN_GROUP <= SUBLANES
    sums, upper, pair = _hgrn_constants()
    n_layers = lb_raw.shape[0]
    part = lambda off: pl.BlockSpec((seq, HEAD_DIM), lambda b, h: (b, off + h))
    whole = lambda a: pl.BlockSpec(a.shape, lambda b, h: (0,) * a.ndim)
    return pl.pallas_call(
        functools.partial(_hgrn_kernel, layer=layer, n_chunks=n_chunks),
        grid=(batch, n_heads),
        in_specs=[part(0), part(n_heads), part(2 * n_heads), part(3 * n_heads),
                  pl.BlockSpec((n_layers, HEAD_DIM), lambda b, h: (0, h)),
                  pl.BlockSpec((1, HEAD_DIM), lambda b, h: (0, 0)),
                  whole(sums), whole(upper), whole(pair)],
        out_specs=pl.BlockSpec((seq, HEAD_DIM), lambda b, h: (b, h)),
        out_shape=jax.ShapeDtypeStruct((t, d), BF16),
        scratch_shapes=[pltpu.VMEM((seq, HEAD_DIM), BF16),
                        pltpu.VMEM((seq, HEAD_DIM), F32),
                        pltpu.VMEM((n_chunks, HEAD_DIM, HEAD_DIM), F32),
                        pltpu.VMEM((n_chunks // HGRN_GROUP * SUBLANES, HEAD_DIM), F32)],
        compiler_params=_params("parallel", "parallel"),
        name="hgrn_recurrence",
    )(proj, proj, proj, proj, lb_raw, g_norm.reshape(1, HEAD_DIM),
      jnp.asarray(sums, BF16), jnp.asarray(upper), jnp.asarray(pair))


def _router_kernel(x_ref, w_ref, b_ref, ids_ref, wts_ref):
    g, eg = N_GROUPS, EXPERTS_PER_GROUP
    logits = jnp.dot(x_ref[...], w_ref[...], precision=lax.Precision.HIGHEST,
                     preferred_element_type=F32) + b_ref[...]
    lane = lax.broadcasted_iota(I32, logits.shape, 1)

    def softmax_over(mask):
        z = jnp.where(mask, logits, -jnp.inf)
        ez = jnp.exp(z - jnp.max(z, axis=1, keepdims=True))
        return ez / jnp.sum(ez, axis=1, keepdims=True)

    def top1(p, mask):
        best = jnp.max(jnp.where(mask, p, -1.0), axis=1, keepdims=True)
        where = jnp.min(jnp.where(mask & (p == best), lane, LANES), axis=1, keepdims=True)
        return best, where

    is_group = lane < g
    g_w, g_idx = top1(softmax_over(is_group), is_group)
    lo = g + g_idx * eg
    in_group = (lane >= lo) & (lane < lo + eg)
    pe = softmax_over(in_group)
    w1, i1 = top1(pe, in_group)
    w2, i2 = top1(pe, in_group & (lane != i1))
    den = w1 + w2
    wt1, wt2 = g_w * (w1 / den), g_w * (w2 / den)
    ids_ref[...] = jnp.where(lane == 0, i1 - g, jnp.where(lane == 1, i2 - g, 0))
    wts_ref[...] = jnp.where(lane == 0, wt1, jnp.where(lane == 1, wt2, 0.0))


def _router(x, w_r, b_r, *, tm):
    t, d = x.shape
    tm = min(tm, t)
    assert t % tm == 0
    row = lambda i: (i, 0)
    fixed = lambda i: (0, 0)
    return pl.pallas_call(
        _router_kernel,
        grid=(t // tm,),
        in_specs=[pl.BlockSpec((tm, d), row), pl.BlockSpec((d, LANES), fixed), pl.BlockSpec((1, LANES), fixed)],
        out_specs=[pl.BlockSpec((tm, LANES), row), pl.BlockSpec((tm, LANES), row)],
        out_shape=[jax.ShapeDtypeStruct((t, LANES), I32), jax.ShapeDtypeStruct((t, LANES), F32)],
        compiler_params=_params("parallel"),
        name="moe_router",
    )(x, w_r, b_r)


def _expert_kernel(blk_expert_ref, row_src_ref, row_dst_ref, n_used_ref, x_hbm, wg_ref, wu_ref, wd_ref, y_hbm,
                   xbuf0, xbuf1, ybuf0, ybuf1, in_sems, out_sems, fill_sem):
    del blk_expert_ref
    i = pl.program_id(0)
    n_blk = pl.num_programs(0)
    d = wg_ref.shape[0]
    n = d // LANES
    rows = xbuf0.shape[0] // n
    n_used = n_used_ref[0]
    active = i < n_used
    xbufs, ybufs = (xbuf0, xbuf1), (ybuf0, ybuf1)

    def gather(block, s):
        for r in range(rows):
            tok = pl.multiple_of(row_src_ref[block * rows + r] * n, n)
            pltpu.make_async_copy(x_hbm.at[pl.ds(tok, n), :], xbufs[s].at[pl.ds(r * n, n), :], in_sems.at[s]).start()

    def scatter(block, s):
        for r in range(rows):
            dst = pl.multiple_of(row_dst_ref[block * rows + r] * n, n)
            pltpu.make_async_copy(ybufs[s].at[pl.ds(r * n, n), :], y_hbm.at[pl.ds(dst, n), :], out_sems.at[s]).start()

    def wait_gather(s):
        pltpu.make_async_copy(x_hbm.at[pl.ds(0, rows * n), :], xbufs[s], in_sems.at[s]).wait()

    def wait_scatter(s):
        pltpu.make_async_copy(ybufs[s], y_hbm.at[pl.ds(0, rows * n), :], out_sems.at[s]).wait()

    def mlp(s):
        xb = _load_slabs(xbufs[s], rows, d).astype(BF16)
        hg = jnp.dot(xb, wg_ref[...], preferred_element_type=F32)
        hu = jnp.dot(xb, wu_ref[...], preferred_element_type=F32)
        h = (hg * _sigmoid(hg)) * hu
        _store_slabs(ybufs[s], jnp.dot(h.astype(BF16), wd_ref[...], preferred_element_type=F32))

    def finish(s):
        @pl.when(i == n_blk - 1)
        def _():
            scatter(i, s)
            wait_scatter(s)
            wait_gather(1 - s)

            @pl.when(i >= 1)
            def _():
                wait_scatter(1 - s)

    @pl.when(i == 0)
    def _():
        gather(0, 0)
        wait_gather(0)
        gather(jnp.minimum(1, n_blk - 1), 1)
        mlp(0)
        finish(0)

    for s in range(2):
        @pl.when(active & (i >= 1) & (i % 2 == s))
        def _(s=s):
            wait_gather(s)

            @pl.when(i >= 2)
            def _():
                wait_scatter(s)

            gather(jnp.minimum(i + 1, n_blk - 1), 1 - s)
            scatter(i - 1, 1 - s)
            mlp(s)
            finish(s)

        @pl.when((i == n_used) & (i % 2 == s))
        def _(s=s):
            wait_gather(s)
            scatter(i - 1, 1 - s)
            wait_scatter(1 - s)

            @pl.when(i >= 2)
            def _():
                wait_scatter(s)

    @pl.when(jnp.logical_not(active))
    def _():
        ybuf0[...] = jnp.zeros_like(ybuf0)
        fill = pltpu.make_async_copy(ybuf0, y_hbm.at[pl.ds(pl.multiple_of(i * (rows * n), rows * n), rows * n), :],
                                     fill_sem)
        fill.start()
        fill.wait()


def _expert_mlp(x_slabs, blk_expert, row_src, row_dst, n_used, w_gate, w_up, w_down):
    n_e, d, f = w_gate.shape
    n_blk = blk_expert.shape[0]
    rows = MOE_ROWS
    n = d // LANES
    by_expert = lambda i, be, rs, rd, nu: (be[i], 0, 0)
    row_buf = pltpu.VMEM((rows * n, LANES), F32)
    grid_spec = pltpu.PrefetchScalarGridSpec(
        num_scalar_prefetch=4,
        grid=(n_blk,),
        in_specs=[pl.BlockSpec(memory_space=pl.ANY),
                  pl.BlockSpec((None, d, f), by_expert), pl.BlockSpec((None, d, f), by_expert),
                  pl.BlockSpec((None, f, d), by_expert)],
        out_specs=pl.BlockSpec(memory_space=pl.ANY),
        scratch_shapes=[row_buf, row_buf, row_buf, row_buf,
                        pltpu.SemaphoreType.DMA((2,)), pltpu.SemaphoreType.DMA((2,)), pltpu.SemaphoreType.DMA(())],
    )
    return pl.pallas_call(
        _expert_kernel,
        grid_spec=grid_spec,
        out_shape=jax.ShapeDtypeStruct((n_blk * rows * n, LANES), F32),
        compiler_params=_params("arbitrary"),
        name="moe_experts",
    )(blk_expert, row_src, row_dst, n_used, x_slabs, w_gate, w_up, w_down)


def _combine_ln_kernel(y0_ref, y1_ref, x_ref, wts_ref, g_ref, b_ref, xo_ref, xb_ref, *, alpha):
    tm, d = x_ref.shape
    wts = wts_ref[...]
    m = wts[:, 0:1] * _load_slabs(y0_ref, tm, d) + wts[:, 1:2] * _load_slabs(y1_ref, tm, d)
    out = _layer_norm(alpha * x_ref[...] + m, g_ref[...], b_ref[...])
    xo_ref[...] = out
    xb_ref[...] = out.astype(BF16)


def _combine_ln(y_slabs, x, wts, g, b, *, alpha, tm):
    t, d = x.shape
    tm = min(tm, t)
    n = d // LANES
    assert t % tm == 0
    row = lambda i: (i, 0)
    fixed = lambda i: (0, 0)
    return pl.pallas_call(
        functools.partial(_combine_ln_kernel, alpha=alpha),
        grid=(t // tm,),
        in_specs=[pl.BlockSpec((tm * n, LANES), row), pl.BlockSpec((tm * n, LANES), lambda i: (t // tm + i, 0)),
                  pl.BlockSpec((tm, d), row), pl.BlockSpec((tm, LANES), row),
                  pl.BlockSpec((1, d), fixed), pl.BlockSpec((1, d), fixed)],
        out_specs=[pl.BlockSpec((tm, d), row), pl.BlockSpec((tm, d), row)],
        out_shape=[jax.ShapeDtypeStruct((t, d), F32), jax.ShapeDtypeStruct((t, d), BF16)],
        compiler_params=_params("parallel"),
        name="moe_combine_ln",
    )(y_slabs, y_slabs, x, wts, g.reshape(1, d), b.reshape(1, d))


def _dispatch_plan(ids, n_tokens):
    rows = MOE_ROWS
    n_assign = n_tokens * TOP_E
    eid = ids[:, :TOP_E].reshape(n_assign)
    onehot = (eid[:, None] == jnp.arange(N_EXPERTS, dtype=I32)[None, :]).astype(I32)
    csum = jnp.cumsum(onehot, axis=0)
    rank = jnp.sum(onehot * csum, axis=1) - 1
    counts = csum[-1]
    padded = (counts + rows - 1) // rows * rows
    pends = jnp.cumsum(padded)
    pstarts = pends - padded
    pos = (pstarts[eid] + rank).astype(I32)
    n_blk = n_assign // rows + N_EXPERTS
    assign = jnp.full((n_blk * rows,), -1, I32).at[pos].set(jnp.arange(n_assign, dtype=I32))
    is_pad = assign < 0
    row_src = jnp.where(is_pad, 0, assign // TOP_E)
    row_dst = jnp.where(is_pad, n_assign - 1 + jnp.cumsum(is_pad.astype(I32)),
                        (assign % TOP_E) * n_tokens + assign // TOP_E)
    blk_expert = jnp.minimum(jnp.searchsorted(pends, jnp.arange(n_blk, dtype=I32) * rows, side='right'),
                             N_EXPERTS - 1).astype(I32)
    n_used = (pends[-1] // rows).astype(I32).reshape(1)
    return row_src, row_dst, blk_expert, n_used


def _moe_ln(x, x_slabs, w_r, b_r, w_gate, w_up, w_down, g, b, *, alpha):
    ids, wts = _router(x, w_r, b_r, tm=256)
    row_src, row_dst, blk_expert, n_used = _dispatch_plan(ids, x.shape[0])
    y_slabs = _expert_mlp(x_slabs, blk_expert, row_src, row_dst, n_used, w_gate, w_up, w_down)
    return _combine_ln(y_slabs, x, wts, g, b, alpha=alpha, tm=256)


def kernel(x, moba_w_in, moba_w_o, hgrn_w_in, hgrn_g_norm, hgrn_lb_raw, hgrn_w_o, ln_mix_g, ln_mix_b,
           moe_w_rg, moe_b_rg, moe_w_re, moe_b_re, moe_w_gate, moe_w_up, moe_w_down, ln_ffn_g, ln_ffn_b):
    batch, seq, d = x.shape
    depth = ln_mix_g.shape[0]
    alpha = (2 * depth) ** 0.25
    t = batch * seq
    xf = x.reshape(t, d)
    xb = xf.astype(BF16)
    n_re = N_GROUPS * EXPERTS_PER_GROUP
    for layer in range(depth):
        j = layer // 2
        if layer % 2 == 0:
            qkv = _matmul(xb, moba_w_in[j].astype(BF16), BF16, tm=1024, tn=512,
                          scaled_cols=d, scale=HEAD_DIM ** -0.5)
            o = _moba_attention(qkv, batch, seq)
            w_o = moba_w_o[j]
        else:
            proj = _matmul(xb, hgrn_w_in[j].astype(BF16), F32, tm=1024, tn=512)
            o = _hgrn_mix(proj, hgrn_lb_raw, hgrn_g_norm[j], layer=j, batch=batch, seq=seq)
            w_o = hgrn_w_o[j]
        xf, xs = _proj_ln(o, w_o.astype(BF16), xf, ln_mix_g[layer], ln_mix_b[layer], alpha=alpha, tm=256)
        w_re = jnp.transpose(moe_w_re[layer], (1, 0, 2)).reshape(d, n_re)
        w_r = jnp.concatenate([moe_w_rg[layer], w_re, jnp.zeros((d, LANES - N_GROUPS - n_re), F32)], axis=1)
        b_r = jnp.concatenate([moe_b_rg[layer], moe_b_re[layer].reshape(n_re),
                               jnp.zeros((LANES - N_GROUPS - n_re,), F32)]).reshape(1, LANES)
        xf, xb = _moe_ln(xf, xs, w_r, b_r, moe_w_gate[layer].astype(BF16), moe_w_up[layer].astype(BF16),
                         moe_w_down[layer].astype(BF16), ln_ffn_g[layer], ln_ffn_b[layer], alpha=alpha)
    return xf.reshape(batch, seq, d)
```

```python
import functools

import numpy as np
import jax
import jax.numpy as jnp
from jax import lax
from jax.experimental import pallas as pl
from jax.experimental.pallas import tpu as pltpu

F32 = jnp.float32
BF16 = jnp.bfloat16
I32 = jnp.int32

LANES = 128
SUBLANES = 8
VMEM_LIMIT_BYTES = 56 << 20

HEAD_DIM = 128
MOBA_BLOCK = 256
MOBA_TOPK = 3
HGRN_CHUNK = 64
N_GROUPS = 4
EXPERTS_PER_GROUP = 8
N_EXPERTS = N_GROUPS * EXPERTS_PER_GROUP
TOP_E = 2
LN_EPS = 1e-5
RMS_EPS = 1e-6

MOE_ROWS = 256
NT_DIMS = (((1,), (1,)), ((), ()))
TN_DIMS = (((0,), (0,)), ((), ()))


def _params(*semantics):
    return pltpu.CompilerParams(dimension_semantics=semantics, vmem_limit_bytes=VMEM_LIMIT_BYTES)


def _sigmoid(x):
    return 1.0 / (1.0 + jnp.exp(-x))


def _layer_norm(y, g, b):
    mu = jnp.mean(y, axis=-1, keepdims=True)
    d = y - mu
    var = jnp.mean(d * d, axis=-1, keepdims=True)
    return d * lax.rsqrt(var + LN_EPS) * g + b


def _matmul_kernel(x_ref, w_ref, o_ref, *, n_scaled_blocks, scale):
    acc = jnp.dot(x_ref[...], w_ref[...], preferred_element_type=F32)
    if n_scaled_blocks:
        acc = acc * jnp.where(pl.program_id(1) < n_scaled_blocks, scale, 1.0).astype(F32)
    o_ref[...] = acc.astype(o_ref.dtype)


def _matmul(x, w, out_dtype, *, tm, tn, scaled_cols=0, scale=1.0):
    m, k = x.shape
    n = w.shape[1]
    tm, tn = min(tm, m), min(tn, n)
    assert m % tm == 0 and n % tn == 0 and scaled_cols % tn == 0
    kern = functools.partial(_matmul_kernel, n_scaled_blocks=scaled_cols // tn, scale=scale)
    return pl.pallas_call(
        kern,
        grid=(m // tm, n // tn),
        in_specs=[pl.BlockSpec((tm, k), lambda i, j: (i, 0)),
                  pl.BlockSpec((k, tn), lambda i, j: (0, j))],
        out_specs=pl.BlockSpec((tm, tn), lambda i, j: (i, j)),
        out_shape=jax.ShapeDtypeStruct((m, n), out_dtype),
        compiler_params=_params("parallel", "parallel"),
        name="proj_matmul",
    )(x, w)


def _store_slabs(slab_ref, x):
    rows, d = x.shape
    n = d // LANES
    for c in range(n):
        slab_ref[pl.ds(c, rows, stride=n), :] = x[:, c * LANES:(c + 1) * LANES]


def _load_slabs(slab_ref, rows, d):
    n = d // LANES
    return jnp.concatenate([slab_ref[pl.ds(c, rows, stride=n), :] for c in range(n)], axis=1)


def _proj_ln_kernel(o_ref, w_ref, x_ref, g_ref, b_ref, xo_ref, xs_ref, *, alpha):
    h = jnp.dot(o_ref[...], w_ref[...], preferred_element_type=F32)
    out = _layer_norm(alpha * x_ref[...] + h, g_ref[...], b_ref[...])
    xo_ref[...] = out
    _store_slabs(xs_ref, out)


def _proj_ln(o, w, x, g, b, *, alpha, tm):
    t, d = x.shape
    tm = min(tm, t)
    n = d // LANES
    assert t % tm == 0
    row = lambda i: (i, 0)
    fixed = lambda i: (0, 0)
    return pl.pallas_call(
        functools.partial(_proj_ln_kernel, alpha=alpha),
        grid=(t // tm,),
        in_specs=[pl.BlockSpec((tm, d), row), pl.BlockSpec((d, d), fixed), pl.BlockSpec((tm, d), row),
                  pl.BlockSpec((1, d), fixed), pl.BlockSpec((1, d), fixed)],
        out_specs=[pl.BlockSpec((tm, d), row), pl.BlockSpec((tm * n, LANES), row)],
        out_shape=[jax.ShapeDtypeStruct((t, d), F32), jax.ShapeDtypeStruct((t * n, LANES), F32)],
        compiler_params=_params("parallel"),
        name="proj_ln",
    )(o, w, x, g.reshape(1, d), b.reshape(1, d))


MOBA_HEADS_PER_STEP = 4
MOBA_TILE = 2 * MOBA_BLOCK


def _moba_kernel(q_ref, k_ref, v_ref, o_ref, kmean_ref, vt_ref, sel_ref, *, n_blocks):
    blk, tile, hd = MOBA_BLOCK, MOBA_TILE, HEAD_DIM
    heads = range(MOBA_HEADS_PER_STEP)
    i = pl.program_id(2)

    @pl.when(i == 0)
    def _():
        kmean_ref[...] = jnp.zeros_like(kmean_ref)
        for h in heads:
            cols = slice(h * hd, (h + 1) * hd)
            for j in range(n_blocks):
                kj = k_ref[pl.ds(j * blk, blk), cols].astype(F32)
                kmean_ref[h, pl.ds(j, 1), :] = jnp.mean(kj, axis=0, keepdims=True)
            for p in range(n_blocks // 2):
                vt_ref[h, p] = v_ref[pl.ds(p * tile, tile), cols].T

    q = [q_ref[:, h * hd:(h + 1) * hd] for h in heads]
    for h in heads:
        gate = lax.dot_general(kmean_ref[h], q[h].astype(F32), NT_DIMS,
                               precision=lax.Precision.HIGHEST, preferred_element_type=F32)
        blk_idx = lax.broadcasted_iota(I32, gate.shape, 0)
        gate = jnp.where(blk_idx < i, gate, -jnp.inf)
        rank = jnp.zeros(gate.shape, I32)
        for jp in range(n_blocks):
            row = gate[jp:jp + 1, :]
            beats = (row > gate) | ((row == gate) & (jp < blk_idx))
            rank = rank + jnp.where(beats, 1, 0)
        sel_ref[h] = jnp.where((gate > -jnp.inf) & (rank < MOBA_TOPK), 1.0, 0.0)

    def scores(h, p):
        kt = k_ref[pl.ds(pl.multiple_of(p * tile, tile), tile), h * hd:(h + 1) * hd]
        return lax.dot_general(kt, q[h], NT_DIMS, preferred_element_type=F32)

    def masked(s, keep_top, keep_bot):
        return jnp.concatenate([jnp.where(keep_top, s[:blk], -jnp.inf),
                                jnp.where(keep_bot, s[blk:], -jnp.inf)], axis=0)

    p_own = i // 2
    causal = lax.broadcasted_iota(I32, (blk, blk), 0) <= lax.broadcasted_iota(I32, (blk, blk), 1)
    i_odd = jnp.full((blk, blk), i % 2, I32) == 1
    s_all = [scores(h, p_own) for h in heads]
    state, p_all = [], []
    for h in heads:
        chosen_prev = sel_ref[h, pl.ds(2 * p_own, 1), :] > 0.0
        s = masked(s_all[h], (i_odd & chosen_prev) | (~i_odd & causal), i_odd & causal)
        m = jnp.max(s, axis=0, keepdims=True)
        p = jnp.exp(s - m)
        state += [m, jnp.sum(p, axis=0, keepdims=True)]
        p_all.append(p.astype(BF16))
    for h in heads:
        acc = jnp.dot(vt_ref[h, p_own], p_all[h], preferred_element_type=F32)
        state.insert(3 * h + 2, acc)

    def body(pt, carry):
        s_all = [scores(h, pt) for h in heads]
        out, p_all, a_all = [], [], []
        for h in heads:
            m, l, _ = carry[3 * h:3 * h + 3]
            s = masked(s_all[h], sel_ref[h, pl.ds(2 * pt, 1), :] > 0.0,
                       sel_ref[h, pl.ds(2 * pt + 1, 1), :] > 0.0)
            m_new = jnp.maximum(m, jnp.max(s, axis=0, keepdims=True))
            a = jnp.exp(m - m_new)
            p = jnp.exp(s - m_new)
            out += [m_new, a * l + jnp.sum(p, axis=0, keepdims=True)]
            p_all.append(p.astype(BF16))
            a_all.append(a)
        for h in heads:
            acc = a_all[h] * carry[3 * h + 2] + jnp.dot(vt_ref[h, pt], p_all[h], preferred_element_type=F32)
            out.insert(3 * h + 2, acc)
        return tuple(out)

    state = lax.fori_loop(0, p_own, body, tuple(state))
    for h in heads:
        _, l, acc = state[3 * h:3 * h + 3]
        o_ref[:, h * hd:(h + 1) * hd] = (acc / l).T.astype(o_ref.dtype)


def _moba_attention(qkv, batch, seq):
    t, d3 = qkv.shape
    d = d3 // 3
    hs = MOBA_HEADS_PER_STEP
    n_head_groups = d // (HEAD_DIM * hs)
    n_blocks = seq // MOBA_BLOCK
    assert seq % MOBA_TILE == 0 and d % (HEAD_DIM * hs) == 0
    nb_pad = -(-n_blocks // SUBLANES) * SUBLANES
    kv_spec = lambda off: pl.BlockSpec((seq, HEAD_DIM * hs), lambda b, h, i: (b, off + h))
    q_spec = pl.BlockSpec((MOBA_BLOCK, HEAD_DIM * hs), lambda b, h, i: (b * n_blocks + i, h))
    return pl.pallas_call(
        functools.partial(_moba_kernel, n_blocks=n_blocks),
        grid=(batch, n_head_groups, n_blocks),
        in_specs=[q_spec, kv_spec(n_head_groups), kv_spec(2 * n_head_groups)],
        out_specs=q_spec,
        out_shape=jax.ShapeDtypeStruct((t, d), BF16),
        scratch_shapes=[pltpu.VMEM((hs, nb_pad, HEAD_DIM), F32),
                        pltpu.VMEM((hs, n_blocks // 2, HEAD_DIM, MOBA_TILE), BF16),
                        pltpu.VMEM((hs, nb_pad, MOBA_BLOCK), F32)],
        compiler_params=_params("parallel", "parallel", "arbitrary"),
        name="moba_attention",
    )(qkv, qkv, qkv)


_HGRN_LEVELS = (32, 16, 8, 4, 2, 1)
HGRN_GROUP = 4


def _hgrn_constants():
    c = HGRN_CHUNK
    idx = np.arange(c)
    t, s = idx[:, None], idx[None, :]
    sums = [(s <= t)]
    upper, pair = [], []
    for m in _HGRN_LEVELS:
        same = (t // (2 * m)) == (s // (2 * m))
        t_up, s_up = (t % (2 * m)) >= m, (s % (2 * m)) >= m
        up_rows = same & t_up & s_up & (s <= t)
        lo_rows = same & ~t_up & ~s_up & (s > t)
        sums.append(up_rows | lo_rows)
        upper.append(np.broadcast_to(t_up, (c, LANES)))
        pair.append(same & t_up & ~s_up)
    return (np.concatenate(sums, 0).astype(np.float32),
            np.stack(upper).astype(np.float32), np.stack(pair).astype(np.float32))


def _hgrn_kernel(q_ref, f_ref, i_ref, g_ref, lbraw_ref, gn_ref, sums_ref, upper_ref, pair_ref, o_ref,
                 qg_ref, oin_ref, u_ref, dl_ref, *, layer, n_chunks):
    c = HGRN_CHUNK
    dk = HEAD_DIM
    raw = lbraw_ref[...]
    e = jnp.exp(raw - jnp.max(raw, axis=0, keepdims=True))
    sm = e / jnp.sum(e, axis=0, keepdims=True)
    cum = sm[0:1, :]
    for l in range(1, layer + 1):
        cum = cum + sm[l:l + 1, :]
    lb = cum - sm[0:1, :]
    gnorm = gn_ref[...]
    eye = (lax.broadcasted_iota(I32, (c, c), 0) == lax.broadcasted_iota(I32, (c, c), 1))

    group = range(HGRN_GROUP)
    chunk = lambda x, n: x[n * c:(n + 1) * c]
    n_levels = len(_HGRN_LEVELS)

    def local(gi, carry):
        rows = pl.ds(pl.multiple_of(gi * (HGRN_GROUP * c), HGRN_GROUP * c), HGRN_GROUP * c)
        qv, fv, iv = q_ref[rows, :], f_ref[rows, :], i_ref[rows, :]
        qs = qv * _sigmoid(qv)
        fg = lb + (1.0 - lb) * _sigmoid(fv)
        log_f = jnp.log(fg)
        kk = 1.0 - fg
        iv_b = iv.astype(BF16)
        hi = log_f.astype(BF16)
        rest = log_f - hi.astype(F32)
        mid = rest.astype(BF16)
        lo = (rest - mid.astype(F32)).astype(BF16)
        parts = jnp.concatenate([chunk(t, n) for n in group for t in (hi, mid, lo)], axis=1)
        s3 = jnp.dot(sums_ref[...], parts, preferred_element_type=F32)
        sums = [(s3[:, (3 * n) * dk:(3 * n + 1) * dk] + s3[:, (3 * n + 1) * dk:(3 * n + 2) * dk])
                + s3[:, (3 * n + 2) * dk:(3 * n + 3) * dk] for n in group]
        gcum = jnp.concatenate([s[0:c, :] for s in sums], axis=0)
        qg_ref[rows, :] = (qs * jnp.exp(gcum)).astype(BF16)
        qe, ke = [], []
        for n in group:
            for li in range(n_levels):
                dec = jnp.exp(sums[n][(li + 1) * c:(li + 2) * c, :])
                is_up = upper_ref[li] > 0.0
                qe.append(jnp.where(is_up, chunk(qs, n) * dec, 0.0).astype(BF16))
                ke.append(jnp.where(is_up, 0.0, chunk(kk, n) * dec).astype(BF16))
        prods = [lax.dot_general(a, b, NT_DIMS, preferred_element_type=F32) for a, b in zip(qe, ke)]
        diag = jnp.sum(qs * kk, axis=1, keepdims=True)
        a_all = []
        for n in group:
            a = jnp.where(eye, chunk(diag, n), 0.0)
            for li in range(n_levels):
                a = a + jnp.where(pair_ref[li] > 0.0, prods[n * n_levels + li], 0.0)
            a_all.append(a.astype(BF16))
        o_in = [jnp.dot(a_all[n], chunk(iv_b, n), preferred_element_type=F32) for n in group]
        oin_ref[rows, :] = jnp.concatenate(o_in, axis=0)
        g_last = [gcum[(n + 1) * c - 1:(n + 1) * c, :] for n in group]
        k_dec = [(chunk(kk, n) * jnp.exp(g_last[n] - chunk(gcum, n))).astype(BF16) for n in group]
        for n in group:
            u_ref[gi * HGRN_GROUP + n] = lax.dot_general(chunk(iv_b, n), k_dec[n], TN_DIMS,
                                                         preferred_element_type=F32)
        dl_ref[pl.ds(pl.multiple_of(gi * SUBLANES, SUBLANES), SUBLANES), :] = jnp.concatenate(
            [jnp.exp(g) for g in g_last] + [jnp.zeros((SUBLANES - HGRN_GROUP, dk), F32)], axis=0)
        return carry

    lax.fori_loop(0, n_chunks // HGRN_GROUP, local, 0)

    def recur(gi, state_t):
        rows = pl.ds(pl.multiple_of(gi * (HGRN_GROUP * c), HGRN_GROUP * c), HGRN_GROUP * c)
        decay = dl_ref[pl.ds(pl.multiple_of(gi * SUBLANES, SUBLANES), SUBLANES), :]
        states = []
        for n in group:
            states.append(state_t.astype(BF16))
            state_t = decay[n:n + 1, :] * state_t + u_ref[gi * HGRN_GROUP + n]
        qg = qg_ref[rows, :]
        o = oin_ref[rows, :] + jnp.concatenate(
            [lax.dot_general(chunk(qg, n), states[n], NT_DIMS, preferred_element_type=F32) for n in group], axis=0)
        gv = g_ref[rows, :]
        o = o * lax.rsqrt(jnp.mean(o * o, axis=1, keepdims=True) + RMS_EPS) * gnorm
        o_ref[rows, :] = (o * (gv * _sigmoid(gv))).astype(o_ref.dtype)
        return state_t

    lax.fori_loop(0, n_chunks // HGRN_GROUP, recur, jnp.zeros((HEAD_DIM, HEAD_DIM), F32))


def _hgrn_mix(proj, lb_raw, g_norm, *, layer, batch, seq):
    t, d4 = proj.shape
    d = d4 // 4
    n_heads = d // HEAD_DIM
    n_chunks = seq // HGRN_CHUNK
    assert seq % (HGRN_CHUNK * HGRN_GROUP) == 0 and HGRN_GROUP <= SUBLANES
    sums, upper, pair = _hgrn_constants()
    n_layers = lb_raw.shape[0]
    part = lambda off: pl.BlockSpec((seq, HEAD_DIM), lambda b, h: (b, off + h))
    whole = lambda a: pl.BlockSpec(a.shape, lambda b, h: (0,) * a.ndim)
    return pl.pallas_call(
        functools.partial(_hgrn_kernel, layer=layer, n_chunks=n_chunks),
        grid=(batch, n_heads),
        in_specs=[part(0), part(n_heads), part(2 * n_heads), part(3 * n_heads),
                  pl.BlockSpec((n_layers, HEAD_DIM), lambda b, h: (0, h)),
                  pl.BlockSpec((1, HEAD_DIM), lambda b, h: (0, 0)),
                  whole(sums), whole(upper), whole(pair)],
        out_specs=pl.BlockSpec((seq, HEAD_DIM), lambda b, h: (b, h)),
        out_shape=jax.ShapeDtypeStruct((t, d), BF16),
        scratch_shapes=[pltpu.VMEM((seq, HEAD_DIM), BF16),
                        pltpu.VMEM((seq, HEAD_DIM), F32),
                        pltpu.VMEM((n_chunks, HEAD_DIM, HEAD_DIM), F32),
                        pltpu.VMEM((n_chunks // HGRN_GROUP * SUBLANES, HEAD_DIM), F32)],
        compiler_params=_params("parallel", "parallel"),
        name="hgrn_recurrence",
    )(proj, proj, proj, proj, lb_raw, g_norm.reshape(1, HEAD_DIM),
      jnp.asarray(sums, BF16), jnp.asarray(upper), jnp.asarray(pair))


def _router_kernel(x_ref, w_ref, b_ref, ids_ref, wts_ref):
    g, eg = N_GROUPS, EXPERTS_PER_GROUP
    logits = jnp.dot(x_ref[...], w_ref[...], precision=lax.Precision.HIGHEST,
                     preferred_element_type=F32) + b_ref[...]
    lane = lax.broadcasted_iota(I32, logits.shape, 1)

    def softmax_over(mask):
        z = jnp.where(mask, logits, -jnp.inf)
        ez = jnp.exp(z - jnp.max(z, axis=1, keepdims=True))
        return ez / jnp.sum(ez, axis=1, keepdims=True)

    def top1(p, mask):
        best = jnp.max(jnp.where(mask, p, -1.0), axis=1, keepdims=True)
        where = jnp.min(jnp.where(mask & (p == best), lane, LANES), axis=1, keepdims=True)
        return best, where

    is_group = lane < g
    g_w, g_idx = top1(softmax_over(is_group), is_group)
    lo = g + g_idx * eg
    in_group = (lane >= lo) & (lane < lo + eg)
    pe = softmax_over(in_group)
    w1, i1 = top1(pe, in_group)
    w2, i2 = top1(pe, in_group & (lane != i1))
    den = w1 + w2
    wt1, wt2 = g_w * (w1 / den), g_w * (w2 / den)
    ids_ref[...] = jnp.where(lane == 0, i1 - g, jnp.where(lane == 1, i2 - g, 0))
    wts_ref[...] = jnp.where(lane == 0, wt1, jnp.where(lane == 1, wt2, 0.0))


def _router(x, w_r, b_r, *, tm):
    t, d = x.shape
    tm = min(tm, t)
    assert t % tm == 0
    row = lambda i: (i, 0)
    fixed = lambda i: (0, 0)
    return pl.pallas_call(
        _router_kernel,
        grid=(t // tm,),
        in_specs=[pl.BlockSpec((tm, d), row), pl.BlockSpec((d, LANES), fixed), pl.BlockSpec((1, LANES), fixed)],
        out_specs=[pl.BlockSpec((tm, LANES), row), pl.BlockSpec((tm, LANES), row)],
        out_shape=[jax.ShapeDtypeStruct((t, LANES), I32), jax.ShapeDtypeStruct((t, LANES), F32)],
        compiler_params=_params("parallel"),
        name="moe_router",
    )(x, w_r, b_r)


def _expert_kernel(blk_expert_ref, row_src_ref, row_dst_ref, n_used_ref, x_hbm, wg_ref, wu_ref, wd_ref, y_hbm,
                   xbuf, ybuf, in_sems, out_sems, fill_sem):
    del blk_expert_ref
    i = pl.program_id(0)
    n_blk = pl.num_programs(0)
    d = wg_ref.shape[0]
    n = d // LANES
    rows = xbuf.shape[1] // n
    n_in, n_out = xbuf.shape[0], ybuf.shape[0]
    n_used = n_used_ref[0]
    active = i < n_used

    def gather(block):
        s = block % n_in
        block = jnp.minimum(block, n_blk - 1)
        for r in range(rows):
            tok = pl.multiple_of(row_src_ref[block * rows + r] * n, n)
            pltpu.make_async_copy(x_hbm.at[pl.ds(tok, n), :], xbuf.at[s, pl.ds(r * n, n), :], in_sems.at[s]).start()

    def scatter(block):
        s = block % n_out
        for r in range(rows):
            dst = pl.multiple_of(row_dst_ref[block * rows + r] * n, n)
            pltpu.make_async_copy(ybuf.at[s, pl.ds(r * n, n), :], y_hbm.at[pl.ds(dst, n), :], out_sems.at[s]).start()

    def wait_gather(block):
        s = block % n_in
        pltpu.make_async_copy(x_hbm.at[pl.ds(0, rows * n), :], xbuf.at[s], in_sems.at[s]).wait()

    def wait_scatter(block):
        s = block % n_out
        pltpu.make_async_copy(ybuf.at[s], y_hbm.at[pl.ds(0, rows * n), :], out_sems.at[s]).wait()

    def drain(first_unused_block):
        wait_gather(first_unused_block)
        wait_gather(first_unused_block + 1)
        wait_scatter(first_unused_block - 1)

        @pl.when(first_unused_block >= 2)
        def _():
            wait_scatter(first_unused_block - 2)

    @pl.when(i == 0)
    def _():
        gather(0)
        gather(1)

    @pl.when(active)
    def _():
        wait_gather(i)

        @pl.when(i >= n_out)
        def _():
            wait_scatter(i - n_out)

        xb = _load_slabs(xbuf.at[i % n_in], rows, d).astype(BF16)
        hg = jnp.dot(xb, wg_ref[...], preferred_element_type=F32)
        hu = jnp.dot(xb, wu_ref[...], preferred_element_type=F32)
        h = (hg * _sigmoid(hg)) * hu
        _store_slabs(ybuf.at[i % n_out], jnp.dot(h.astype(BF16), wd_ref[...], preferred_element_type=F32))
        scatter(i)
        gather(i + 2)

        @pl.when(i == n_blk - 1)
        def _():
            drain(i + 1)

    @pl.when(i == n_used)
    def _():
        drain(i)

    @pl.when(jnp.logical_not(active))
    def _():
        ybuf[0] = jnp.zeros(ybuf.shape[1:], F32)
        fill = pltpu.make_async_copy(ybuf.at[0], y_hbm.at[pl.ds(pl.multiple_of(i * (rows * n), rows * n), rows * n), :],
                                     fill_sem)
        fill.start()
        fill.wait()


def _expert_mlp(x_slabs, blk_expert, row_src, row_dst, n_used, w_gate, w_up, w_down):
    n_e, d, f = w_gate.shape
    n_blk = blk_expert.shape[0]
    rows = MOE_ROWS
    n = d // LANES
    by_expert = lambda i, be, rs, rd, nu: (be[i], 0, 0)
    grid_spec = pltpu.PrefetchScalarGridSpec(
        num_scalar_prefetch=4,
        grid=(n_blk,),
        in_specs=[pl.BlockSpec(memory_space=pl.ANY),
                  pl.BlockSpec((None, d, f), by_expert), pl.BlockSpec((None, d, f), by_expert),
                  pl.BlockSpec((None, f, d), by_expert)],
        out_specs=pl.BlockSpec(memory_space=pl.ANY),
        scratch_shapes=[pltpu.VMEM((3, rows * n, LANES), F32), pltpu.VMEM((2, rows * n, LANES), F32),
                        pltpu.SemaphoreType.DMA((3,)), pltpu.SemaphoreType.DMA((2,)), pltpu.SemaphoreType.DMA(())],
    )
    return pl.pallas_call(
        _expert_kernel,
        grid_spec=grid_spec,
        out_shape=jax.ShapeDtypeStruct((n_blk * rows * n, LANES), F32),
        compiler_params=_params("arbitrary"),
        name="moe_experts",
    )(blk_expert, row_src, row_dst, n_used, x_slabs, w_gate, w_up, w_down)


def _combine_ln_kernel(y0_ref, y1_ref, x_ref, wts_ref, g_ref, b_ref, xo_ref, xb_ref, *, alpha):
    tm, d = x_ref.shape
    wts = wts_ref[...]
    m = wts[:, 0:1] * _load_slabs(y0_ref, tm, d) + wts[:, 1:2] * _load_slabs(y1_ref, tm, d)
    out = _layer_norm(alpha * x_ref[...] + m, g_ref[...], b_ref[...])
    xo_ref[...] = out
    xb_ref[...] = out.astype(BF16)


def _combine_ln(y_slabs, x, wts, g, b, *, alpha, tm):
    t, d = x.shape
    tm = min(tm, t)
    n = d // LANES
    assert t % tm == 0
    row = lambda i: (i, 0)
    fixed = lambda i: (0, 0)
    return pl.pallas_call(
        functools.partial(_combine_ln_kernel, alpha=alpha),
        grid=(t // tm,),
        in_specs=[pl.BlockSpec((tm * n, LANES), row), pl.BlockSpec((tm * n, LANES), lambda i: (t // tm + i, 0)),
                  pl.BlockSpec((tm, d), row), pl.BlockSpec((tm, LANES), row),
                  pl.BlockSpec((1, d), fixed), pl.BlockSpec((1, d), fixed)],
        out_specs=[pl.BlockSpec((tm, d), row), pl.BlockSpec((tm, d), row)],
        out_shape=[jax.ShapeDtypeStruct((t, d), F32), jax.ShapeDtypeStruct((t, d), BF16)],
        compiler_params=_params("parallel"),
        name="moe_combine_ln",
    )(y_slabs, y_slabs, x, wts, g.reshape(1, d), b.reshape(1, d))


def _dispatch_plan(ids, n_tokens):
    rows = MOE_ROWS
    n_assign = n_tokens * TOP_E
    eid = ids[:, :TOP_E].reshape(n_assign)
    onehot = (eid[:, None] == jnp.arange(N_EXPERTS, dtype=I32)[None, :]).astype(I32)
    csum = jnp.cumsum(onehot, axis=0)
    rank = jnp.sum(onehot * csum, axis=1) - 1
    counts = csum[-1]
    padded = (counts + rows - 1) // rows * rows
    pends = jnp.cumsum(padded)
    pstarts = pends - padded
    pos = (pstarts[eid] + rank).astype(I32)
    n_blk = n_assign // rows + N_EXPERTS
    assign = jnp.full((n_blk * rows,), -1, I32).at[pos].set(jnp.arange(n_assign, dtype=I32))
    is_pad = assign < 0
    row_src = jnp.where(is_pad, 0, assign // TOP_E)
    row_dst = jnp.where(is_pad, n_assign - 1 + jnp.cumsum(is_pad.astype(I32)),
                        (assign % TOP_E) * n_tokens + assign // TOP_E)
    blk_expert = jnp.minimum(jnp.searchsorted(pends, jnp.arange(n_blk, dtype=I32) * rows, side='right'),
                             N_EXPERTS - 1).astype(I32)
    n_used = (pends[-1] // rows).astype(I32).reshape(1)
    return row_src, row_dst, blk_expert, n_used


def _moe_ln(x, x_slabs, w_r, b_r, w_gate, w_up, w_down, g, b, *, alpha):
    ids, wts = _router(x, w_r, b_r, tm=256)
    row_src, row_dst, blk_expert, n_used = _dispatch_plan(ids, x.shape[0])
    y_slabs = _expert_mlp(x_slabs, blk_expert, row_src, row_dst, n_used, w_gate, w_up, w_down)
    return _combine_ln(y_slabs, x, wts, g, b, alpha=alpha, tm=256)


def kernel(x, moba_w_in, moba_w_o, hgrn_w_in, hgrn_g_norm, hgrn_lb_raw, hgrn_w_o, ln_mix_g, ln_mix_b,
           moe_w_rg, moe_b_rg, moe_w_re, moe_b_re, moe_w_gate, moe_w_up, moe_w_down, ln_ffn_g, ln_ffn_b):
    batch, seq, d = x.shape
    depth = ln_mix_g.shape[0]
    alpha = (2 * depth) ** 0.25
    t = batch * seq
    xf = x.reshape(t, d)
    xb = xf.astype(BF16)
    n_re = N_GROUPS * EXPERTS_PER_GROUP
    for layer in range(depth):
        j = layer // 2
        if layer % 2 == 0:
            qkv = _matmul(xb, moba_w_in[j].astype(BF16), BF16, tm=1024, tn=512,
                          scaled_cols=d, scale=HEAD_DIM ** -0.5)
            o = _moba_attention(qkv, batch, seq)
            w_o = moba_w_o[j]
        else:
            proj = _matmul(xb, hgrn_w_in[j].astype(BF16), F32, tm=1024, tn=512)
            o = _hgrn_mix(proj, hgrn_lb_raw, hgrn_g_norm[j], layer=j, batch=batch, seq=seq)
            w_o = hgrn_w_o[j]
        xf, xs = _proj_ln(o, w_o.astype(BF16), xf, ln_mix_g[layer], ln_mix_b[layer], alpha=alpha, tm=256)
        w_re = jnp.transpose(moe_w_re[layer], (1, 0, 2)).reshape(d, n_re)
        w_r = jnp.concatenate([moe_w_rg[layer], w_re, jnp.zeros((d, LANES - N_GROUPS - n_re), F32)], axis=1)
        b_r = jnp.concatenate([moe_b_rg[layer], moe_b_re[layer].reshape(n_re),
                               jnp.zeros((LANES - N_GROUPS - n_re,), F32)]).reshape(1, LANES)
        xf, xb = _moe_ln(xf, xs, w_r, b_r, moe_w_gate[layer].astype(BF16), moe_w_up[layer].astype(BF16),
                         moe_w_down[layer].astype(BF16), ln_ffn_g[layer], ln_ffn_b[layer], alpha=alpha)
    return xf.reshape(batch, seq, d)
```

```python
import functools

import numpy as np
import jax
import jax.numpy as jnp
from jax import lax
from jax.experimental import pallas as pl
from jax.experimental.pallas import tpu as pltpu

F32 = jnp.float32
BF16 = jnp.bfloat16
I32 = jnp.int32

LANES = 128
SUBLANES = 8
VMEM_LIMIT_BYTES = 56 << 20

HEAD_DIM = 128
MOBA_BLOCK = 256
MOBA_TOPK = 3
HGRN_CHUNK = 64
N_GROUPS = 4
EXPERTS_PER_GROUP = 8
N_EXPERTS = N_GROUPS * EXPERTS_PER_GROUP
TOP_E = 2
LN_EPS = 1e-5
RMS_EPS = 1e-6

MOE_ROWS = 256
NT_DIMS = (((1,), (1,)), ((), ()))
TN_DIMS = (((0,), (0,)), ((), ()))


def _params(*semantics):
    return pltpu.CompilerParams(dimension_semantics=semantics, vmem_limit_bytes=VMEM_LIMIT_BYTES)


def _sigmoid(x):
    return 1.0 / (1.0 + jnp.exp(-x))


def _layer_norm(y, g, b):
    mu = jnp.mean(y, axis=-1, keepdims=True)
    d = y - mu
    var = jnp.mean(d * d, axis=-1, keepdims=True)
    return d * lax.rsqrt(var + LN_EPS) * g + b


def _matmul_kernel(x_ref, w_ref, o_ref, *, n_scaled_blocks, scale):
    acc = jnp.dot(x_ref[...], w_ref[...].astype(BF16), preferred_element_type=F32)
    if n_scaled_blocks:
        acc = acc * jnp.where(pl.program_id(1) < n_scaled_blocks, scale, 1.0).astype(F32)
    o_ref[...] = acc.astype(o_ref.dtype)


def _matmul(x, w_stack, layer, out_dtype, *, tm, tn, scaled_cols=0, scale=1.0):
    m, k = x.shape
    n = w_stack.shape[2]
    tm, tn = min(tm, m), min(tn, n)
    assert m % tm == 0 and n % tn == 0 and scaled_cols % tn == 0
    kern = functools.partial(_matmul_kernel, n_scaled_blocks=scaled_cols // tn, scale=scale)
    return pl.pallas_call(
        kern,
        grid=(m // tm, n // tn),
        in_specs=[pl.BlockSpec((tm, k), lambda i, j: (i, 0)),
                  pl.BlockSpec((None, k, tn), lambda i, j: (layer, 0, j))],
        out_specs=pl.BlockSpec((tm, tn), lambda i, j: (i, j)),
        out_shape=jax.ShapeDtypeStruct((m, n), out_dtype),
        compiler_params=_params("parallel", "parallel"),
        name="proj_matmul",
    )(x, w_stack)


def _store_slabs(slab_ref, x):
    rows, d = x.shape
    n = d // LANES
    for c in range(n):
        slab_ref[pl.ds(c, rows, stride=n), :] = x[:, c * LANES:(c + 1) * LANES]


def _load_slabs(slab_ref, rows, d):
    n = d // LANES
    return jnp.concatenate([slab_ref[pl.ds(c, rows, stride=n), :] for c in range(n)], axis=1)


def _proj_ln_kernel(o_ref, w_ref, x_ref, g_ref, b_ref, xo_ref, xs_ref, *, alpha):
    h = jnp.dot(o_ref[...], w_ref[...], preferred_element_type=F32)
    out = _layer_norm(alpha * x_ref[...] + h, g_ref[...], b_ref[...])
    xo_ref[...] = out
    _store_slabs(xs_ref, out)


def _proj_ln(o, w_stack, layer, x, g, b, *, alpha, tm):
    t, d = x.shape
    tm = min(tm, t)
    n = d // LANES
    assert t % tm == 0
    row = lambda i: (i, 0)
    fixed = lambda i: (0, 0)
    return pl.pallas_call(
        functools.partial(_proj_ln_kernel, alpha=alpha),
        grid=(t // tm,),
        in_specs=[pl.BlockSpec((tm, d), row), pl.BlockSpec((None, d, d), lambda i: (layer, 0, 0)),
                  pl.BlockSpec((tm, d), row), pl.BlockSpec((1, d), fixed), pl.BlockSpec((1, d), fixed)],
        out_specs=[pl.BlockSpec((tm, d), row), pl.BlockSpec((tm * n, LANES), row)],
        out_shape=[jax.ShapeDtypeStruct((t, d), F32), jax.ShapeDtypeStruct((t * n, LANES), F32)],
        compiler_params=_params("parallel"),
        name="proj_ln",
    )(o, w_stack, x, g.reshape(1, d), b.reshape(1, d))


MOBA_HEADS_PER_STEP = 4
MOBA_TILE = 2 * MOBA_BLOCK


def _moba_kernel(q_ref, k_ref, v_ref, o_ref, kmean_ref, vt_ref, sel_ref, *, n_blocks):
    blk, tile, hd = MOBA_BLOCK, MOBA_TILE, HEAD_DIM
    heads = range(MOBA_HEADS_PER_STEP)
    i = pl.program_id(2)

    @pl.when(i == 0)
    def _():
        kmean_ref[...] = jnp.zeros_like(kmean_ref)
        for h in heads:
            cols = slice(h * hd, (h + 1) * hd)
            for j in range(n_blocks):
                kj = k_ref[pl.ds(j * blk, blk), cols].astype(F32)
                kmean_ref[h, pl.ds(j, 1), :] = jnp.mean(kj, axis=0, keepdims=True)
            for p in range(n_blocks // 2):
                vt_ref[h, p] = v_ref[pl.ds(p * tile, tile), cols].T

    q = [q_ref[:, h * hd:(h + 1) * hd] for h in heads]
    for h in heads:
        gate = lax.dot_general(kmean_ref[h], q[h].astype(F32), NT_DIMS,
                               precision=lax.Precision.HIGHEST, preferred_element_type=F32)
        blk_idx = lax.broadcasted_iota(I32, gate.shape, 0)
        gate = jnp.where(blk_idx < i, gate, -jnp.inf)
        rank = jnp.zeros(gate.shape, I32)
        for jp in range(n_blocks):
            row = gate[jp:jp + 1, :]
            beats = (row > gate) | ((row == gate) & (jp < blk_idx))
            rank = rank + jnp.where(beats, 1, 0)
        sel_ref[h] = jnp.where((gate > -jnp.inf) & (rank < MOBA_TOPK), 1.0, 0.0)

    def scores(h, p):
        kt = k_ref[pl.ds(pl.multiple_of(p * tile, tile), tile), h * hd:(h + 1) * hd]
        return lax.dot_general(kt, q[h], NT_DIMS, preferred_element_type=F32)

    def masked(s, keep_top, keep_bot):
        return jnp.concatenate([jnp.where(keep_top, s[:blk], -jnp.inf),
                                jnp.where(keep_bot, s[blk:], -jnp.inf)], axis=0)

    p_own = i // 2
    causal = lax.broadcasted_iota(I32, (blk, blk), 0) <= lax.broadcasted_iota(I32, (blk, blk), 1)
    i_odd = jnp.full((blk, blk), i % 2, I32) == 1
    s_all = [scores(h, p_own) for h in heads]
    state, p_all = [], []
    for h in heads:
        chosen_prev = sel_ref[h, pl.ds(2 * p_own, 1), :] > 0.0
        s = masked(s_all[h], (i_odd & chosen_prev) | (~i_odd & causal), i_odd & causal)
        m = jnp.max(s, axis=0, keepdims=True)
        p = jnp.exp(s - m)
        state += [m, jnp.sum(p, axis=0, keepdims=True)]
        p_all.append(p.astype(BF16))
    for h in heads:
        acc = jnp.dot(vt_ref[h, p_own], p_all[h], preferred_element_type=F32)
        state.insert(3 * h + 2, acc)

    def body(pt, carry):
        s_all = [scores(h, pt) for h in heads]
        out, p_all, a_all = [], [], []
        for h in heads:
            m, l, _ = carry[3 * h:3 * h + 3]
            s = masked(s_all[h], sel_ref[h, pl.ds(2 * pt, 1), :] > 0.0,
                       sel_ref[h, pl.ds(2 * pt + 1, 1), :] > 0.0)
            m_new = jnp.maximum(m, jnp.max(s, axis=0, keepdims=True))
            a = jnp.exp(m - m_new)
            p = jnp.exp(s - m_new)
            out += [m_new, a * l + jnp.sum(p, axis=0, keepdims=True)]
            p_all.append(p.astype(BF16))
            a_all.append(a)
        for h in heads:
            acc = a_all[h] * carry[3 * h + 2] + jnp.dot(vt_ref[h, pt], p_all[h], preferred_element_type=F32)
            out.insert(3 * h + 2, acc)
        return tuple(out)

    state = lax.fori_loop(0, p_own, body, tuple(state))
    for h in heads:
        _, l, acc = state[3 * h:3 * h + 3]
        o_ref[:, h * hd:(h + 1) * hd] = (acc / l).T.astype(o_ref.dtype)


def _moba_attention(qkv, batch, seq):
    t, d3 = qkv.shape
    d = d3 // 3
    hs = MOBA_HEADS_PER_STEP
    n_head_groups = d // (HEAD_DIM * hs)
    n_blocks = seq // MOBA_BLOCK
    assert seq % MOBA_TILE == 0 and d % (HEAD_DIM * hs) == 0
    nb_pad = -(-n_blocks // SUBLANES) * SUBLANES
    kv_spec = lambda off: pl.BlockSpec((seq, HEAD_DIM * hs), lambda b, h, i: (b, off + h))
    q_spec = pl.BlockSpec((MOBA_BLOCK, HEAD_DIM * hs), lambda b, h, i: (b * n_blocks + i, h))
    return pl.pallas_call(
        functools.partial(_moba_kernel, n_blocks=n_blocks),
        grid=(batch, n_head_groups, n_blocks),
        in_specs=[q_spec, kv_spec(n_head_groups), kv_spec(2 * n_head_groups)],
        out_specs=q_spec,
        out_shape=jax.ShapeDtypeStruct((t, d), BF16),
        scratch_shapes=[pltpu.VMEM((hs, nb_pad, HEAD_DIM), F32),
                        pltpu.VMEM((hs, n_blocks // 2, HEAD_DIM, MOBA_TILE), BF16),
                        pltpu.VMEM((hs, nb_pad, MOBA_BLOCK), F32)],
        compiler_params=_params("parallel", "parallel", "arbitrary"),
        name="moba_attention",
    )(qkv, qkv, qkv)


_HGRN_LEVELS = (32, 16, 8, 4, 2, 1)
HGRN_GROUP = 4


def _hgrn_constants():
    c = HGRN_CHUNK
    idx = np.arange(c)
    t, s = idx[:, None], idx[None, :]
    sums = [(s <= t)]
    upper, pair = [], []
    for m in _HGRN_LEVELS:
        same = (t // (2 * m)) == (s // (2 * m))
        t_up, s_up = (t % (2 * m)) >= m, (s % (2 * m)) >= m
        up_rows = same & t_up & s_up & (s <= t)
        lo_rows = same & ~t_up & ~s_up & (s > t)
        sums.append(up_rows | lo_rows)
        upper.append(np.broadcast_to(t_up, (c, LANES)))
        pair.append(same & t_up & ~s_up)
    return (np.concatenate(sums, 0).astype(np.float32),
            np.stack(upper).astype(np.float32), np.stack(pair).astype(np.float32))


def _hgrn_kernel(q_ref, f_ref, i_ref, g_ref, lbraw_ref, gn_ref, sums_ref, upper_ref, pair_ref, o_ref,
                 qg_ref, oin_ref, u_ref, dl_ref, *, layer, n_chunks):
    c = HGRN_CHUNK
    dk = HEAD_DIM
    raw = lbraw_ref[...]
    e = jnp.exp(raw - jnp.max(raw, axis=0, keepdims=True))
    sm = e / jnp.sum(e, axis=0, keepdims=True)
    cum = sm[0:1, :]
    for l in range(1, layer + 1):
        cum = cum + sm[l:l + 1, :]
    lb = cum - sm[0:1, :]
    gnorm = gn_ref[...]
    eye = (lax.broadcasted_iota(I32, (c, c), 0) == lax.broadcasted_iota(I32, (c, c), 1))

    group = range(HGRN_GROUP)
    chunk = lambda x, n: x[n * c:(n + 1) * c]
    n_levels = len(_HGRN_LEVELS)

    def local(gi, carry):
        rows = pl.ds(pl.multiple_of(gi * (HGRN_GROUP * c), HGRN_GROUP * c), HGRN_GROUP * c)
        qv, fv, iv = q_ref[rows, :], f_ref[rows, :], i_ref[rows, :]
        qs = qv * _sigmoid(qv)
        fg = lb + (1.0 - lb) * _sigmoid(fv)
        log_f = jnp.log(fg)
        kk = 1.0 - fg
        iv_b = iv.astype(BF16)
        hi = log_f.astype(BF16)
        rest = log_f - hi.astype(F32)
        mid = rest.astype(BF16)
        lo = (rest - mid.astype(F32)).astype(BF16)
        parts = jnp.concatenate([chunk(t, n) for n in group for t in (hi, mid, lo)], axis=1)
        s3 = jnp.dot(sums_ref[...], parts, preferred_element_type=F32)
        sums = [(s3[:, (3 * n) * dk:(3 * n + 1) * dk] + s3[:, (3 * n + 1) * dk:(3 * n + 2) * dk])
                + s3[:, (3 * n + 2) * dk:(3 * n + 3) * dk] for n in group]
        gcum = jnp.concatenate([s[0:c, :] for s in sums], axis=0)
        qg_ref[rows, :] = (qs * jnp.exp(gcum)).astype(BF16)
        qe, ke = [], []
        for n in group:
            for li in range(n_levels):
                dec = jnp.exp(sums[n][(li + 1) * c:(li + 2) * c, :])
                is_up = upper_ref[li] > 0.0
                qe.append(jnp.where(is_up, chunk(qs, n) * dec, 0.0).astype(BF16))
                ke.append(jnp.where(is_up, 0.0, chunk(kk, n) * dec).astype(BF16))
        prods = [lax.dot_general(a, b, NT_DIMS, preferred_element_type=F32) for a, b in zip(qe, ke)]
        diag = jnp.sum(qs * kk, axis=1, keepdims=True)
        a_all = []
        for n in group:
            a = jnp.where(eye, chunk(diag, n), 0.0)
            for li in range(n_levels):
                a = a + jnp.where(pair_ref[li] > 0.0, prods[n * n_levels + li], 0.0)
            a_all.append(a.astype(BF16))
        o_in = [jnp.dot(a_all[n], chunk(iv_b, n), preferred_element_type=F32) for n in group]
        oin_ref[rows, :] = jnp.concatenate(o_in, axis=0)
        g_last = [gcum[(n + 1) * c - 1:(n + 1) * c, :] for n in group]
        k_dec = [(chunk(kk, n) * jnp.exp(g_last[n] - chunk(gcum, n))).astype(BF16) for n in group]
        for n in group:
            u_ref[gi * HGRN_GROUP + n] = lax.dot_general(chunk(iv_b, n), k_dec[n], TN_DIMS,
                                                         preferred_element_type=F32)
        dl_ref[pl.ds(pl.multiple_of(gi * SUBLANES, SUBLANES), SUBLANES), :] = jnp.concatenate(
            [jnp.exp(g) for g in g_last] + [jnp.zeros((SUBLANES - HGRN_GROUP, dk), F32)], axis=0)
        return carry

    lax.fori_loop(0, n_chunks // HGRN_GROUP, local, 0)

    def recur(gi, state_t):
        rows = pl.ds(pl.multiple_of(gi * (HGRN_GROUP * c), HGRN_GROUP * c), HGRN_GROUP * c)
        decay = dl_ref[pl.ds(pl.multiple_of(gi * SUBLANES, SUBLANES), SUBLANES), :]
        states = []
        for n in group:
            states.append(state_t.astype(BF16))
            state_t = decay[n:n + 1, :] * state_t + u_ref[gi * HGRN_GROUP + n]
        qg = qg_ref[rows, :]
        o = oin_ref[rows, :] + jnp.concatenate(
            [lax.dot_general(chunk(qg, n), states[n], NT_DIMS, preferred_element_type=F32) for n in group], axis=0)
        gv = g_ref[rows, :]
        o = o * lax.rsqrt(jnp.mean(o * o, axis=1, keepdims=True) + RMS_EPS) * gnorm
        o_ref[rows, :] = (o * (gv * _sigmoid(gv))).astype(o_ref.dtype)
        return state_t

    lax.fori_loop(0, n_chunks // HGRN_GROUP, recur, jnp.zeros((HEAD_DIM, HEAD_DIM), F32))


def _hgrn_mix(proj, lb_raw, g_norm, *, layer, batch, seq):
    t, d4 = proj.shape
    d = d4 // 4
    n_heads = d // HEAD_DIM
    n_chunks = seq // HGRN_CHUNK
    assert seq % (HGRN_CHUNK * HGRN_GROUP) == 0 and HGRN_GROUP <= SUBLANES
    sums, upper, pair = _hgrn_constants()
    n_layers = lb_raw.shape[0]
    part = lambda off: pl.BlockSpec((seq, HEAD_DIM), lambda b, h: (b, off + h))
    whole = lambda a: pl.BlockSpec(a.shape, lambda b, h: (0,) * a.ndim)
    return pl.pallas_call(
        functools.partial(_hgrn_kernel, layer=layer, n_chunks=n_chunks),
        grid=(batch, n_heads),
        in_specs=[part(0), part(n_heads), part(2 * n_heads), part(3 * n_heads),
                  pl.BlockSpec((n_layers, HEAD_DIM), lambda b, h: (0, h)),
                  pl.BlockSpec((1, HEAD_DIM), lambda b, h: (0, 0)),
                  whole(sums), whole(upper), whole(pair)],
        out_specs=pl.BlockSpec((seq, HEAD_DIM), lambda b, h: (b, h)),
        out_shape=jax.ShapeDtypeStruct((t, d), BF16),
        scratch_shapes=[pltpu.VMEM((seq, HEAD_DIM), BF16),
                        pltpu.VMEM((seq, HEAD_DIM), F32),
                        pltpu.VMEM((n_chunks, HEAD_DIM, HEAD_DIM), F32),
                        pltpu.VMEM((n_chunks // HGRN_GROUP * SUBLANES, HEAD_DIM), F32)],
        compiler_params=_params("parallel", "parallel"),
        name="hgrn_recurrence",
    )(proj, proj, proj, proj, lb_raw, g_norm.reshape(1, HEAD_DIM),
      jnp.asarray(sums, BF16), jnp.asarray(upper), jnp.asarray(pair))


def _router_kernel(x_ref, w_ref, b_ref, ids_ref, wts_ref, counts_ref):
    g, eg = N_GROUPS, EXPERTS_PER_GROUP
    tm = x_ref.shape[0]

    @pl.when(pl.program_id(0) == 0)
    def _():
        counts_ref[...] = jnp.zeros_like(counts_ref)

    logits = jnp.dot(x_ref[...], w_ref[...], precision=lax.Precision.HIGHEST,
                     preferred_element_type=F32) + b_ref[...]
    lane = lax.broadcasted_iota(I32, logits.shape, 1)

    def softmax_over(mask):
        z = jnp.where(mask, logits, -jnp.inf)
        ez = jnp.exp(z - jnp.max(z, axis=1, keepdims=True))
        return ez / jnp.sum(ez, axis=1, keepdims=True)

    def top1(p, mask):
        best = jnp.max(jnp.where(mask, p, -1.0), axis=1, keepdims=True)
        where = jnp.min(jnp.where(mask & (p == best), lane, LANES), axis=1, keepdims=True)
        return best, where

    is_group = lane < g
    g_w, g_idx = top1(softmax_over(is_group), is_group)
    lo = g + g_idx * eg
    in_group = (lane >= lo) & (lane < lo + eg)
    pe = softmax_over(in_group)
    w1, i1 = top1(pe, in_group)
    w2, i2 = top1(pe, in_group & (lane != i1))
    den = w1 + w2
    wt1, wt2 = g_w * (w1 / den), g_w * (w2 / den)
    e1, e2 = i1 - g, i2 - g
    hot1, hot2 = lane == e1, lane == e2
    hot = jnp.where(hot1 | hot2, 1.0, 0.0)
    earlier = lax.broadcasted_iota(I32, (tm, tm), 1) < lax.broadcasted_iota(I32, (tm, tm), 0)
    before = counts_ref[...] + jnp.dot(jnp.where(earlier, 1.0, 0.0).astype(BF16), hot.astype(BF16),
                                       preferred_element_type=F32)
    r1 = jnp.sum(jnp.where(hot1, before, 0.0), axis=1, keepdims=True).astype(I32)
    r2 = jnp.sum(jnp.where(hot2, before, 0.0), axis=1, keepdims=True).astype(I32)
    counts_ref[...] += jnp.sum(hot, axis=0, keepdims=True)
    ids_ref[...] = jnp.where(lane == 0, e1, jnp.where(lane == 1, e2, jnp.where(lane == 2, r1,
                                                                               jnp.where(lane == 3, r2, 0))))
    wts_ref[...] = jnp.where(lane == 0, wt1, jnp.where(lane == 1, wt2, 0.0))


def _router(x, w_r, b_r, *, tm):
    t, d = x.shape
    tm = min(tm, t)
    assert t % tm == 0
    row = lambda i: (i, 0)
    fixed = lambda i: (0, 0)
    return pl.pallas_call(
        _router_kernel,
        grid=(t // tm,),
        in_specs=[pl.BlockSpec((tm, d), row), pl.BlockSpec((d, LANES), fixed), pl.BlockSpec((1, LANES), fixed)],
        out_specs=[pl.BlockSpec((tm, LANES), row), pl.BlockSpec((tm, LANES), row), pl.BlockSpec((1, LANES), fixed)],
        out_shape=[jax.ShapeDtypeStruct((t, LANES), I32), jax.ShapeDtypeStruct((t, LANES), F32),
                   jax.ShapeDtypeStruct((1, LANES), F32)],
        compiler_params=_params("arbitrary"),
        name="moe_router",
    )(x, w_r, b_r)


def _expert_kernel(blk_expert_ref, row_src_ref, row_dst_ref, n_used_ref, x_hbm, wg_ref, wu_ref, wd_ref, y_hbm,
                   xbuf, ybuf, wg_b, wu_b, wd_b, in_sems, out_sems, fill_sem):
    i = pl.program_id(0)
    n_blk = pl.num_programs(0)
    d = wg_ref.shape[0]
    n = d // LANES
    rows = xbuf.shape[1] // n
    n_in, n_out = xbuf.shape[0], ybuf.shape[0]
    n_used = n_used_ref[0]
    active = i < n_used

    def gather(block):
        s = block % n_in
        block = jnp.minimum(block, n_blk - 1)
        for r in range(rows):
            tok = pl.multiple_of(row_src_ref[block * rows + r] * n, n)
            pltpu.make_async_copy(x_hbm.at[pl.ds(tok, n), :], xbuf.at[s, pl.ds(r * n, n), :], in_sems.at[s]).start()

    def scatter(block):
        s = block % n_out
        for r in range(rows):
            dst = pl.multiple_of(row_dst_ref[block * rows + r] * n, n)
            pltpu.make_async_copy(ybuf.at[s, pl.ds(r * n, n), :], y_hbm.at[pl.ds(dst, n), :], out_sems.at[s]).start()

    def wait_gather(block):
        s = block % n_in
        pltpu.make_async_copy(x_hbm.at[pl.ds(0, rows * n), :], xbuf.at[s], in_sems.at[s]).wait()

    def wait_scatter(block):
        s = block % n_out
        pltpu.make_async_copy(ybuf.at[s], y_hbm.at[pl.ds(0, rows * n), :], out_sems.at[s]).wait()

    def drain(first_unused_block):
        wait_gather(first_unused_block)
        wait_gather(first_unused_block + 1)
        wait_scatter(first_unused_block - 1)

        @pl.when(first_unused_block >= 2)
        def _():
            wait_scatter(first_unused_block - 2)

    @pl.when(i == 0)
    def _():
        gather(0)
        gather(1)

    @pl.when(active)
    def _():
        wait_gather(i)

        @pl.when(i >= n_out)
        def _():
            wait_scatter(i - n_out)

        @pl.when((i == 0) | (blk_expert_ref[i] != blk_expert_ref[jnp.maximum(i - 1, 0)]))
        def _():
            wg_b[...] = wg_ref[...].astype(BF16)
            wu_b[...] = wu_ref[...].astype(BF16)
            wd_b[...] = wd_ref[...].astype(BF16)

        xb = _load_slabs(xbuf.at[i % n_in], rows, d).astype(BF16)
        hg = jnp.dot(xb, wg_b[...], preferred_element_type=F32)
        hu = jnp.dot(xb, wu_b[...], preferred_element_type=F32)
        h = (hg * _sigmoid(hg)) * hu
        _store_slabs(ybuf.at[i % n_out], jnp.dot(h.astype(BF16), wd_b[...], preferred_element_type=F32))
        scatter(i)
        gather(i + 2)

        @pl.when(i == n_blk - 1)
        def _():
            drain(i + 1)

    @pl.when(i == n_used)
    def _():
        drain(i)

    @pl.when(jnp.logical_not(active))
    def _():
        ybuf[0] = jnp.zeros(ybuf.shape[1:], F32)
        fill = pltpu.make_async_copy(ybuf.at[0], y_hbm.at[pl.ds(pl.multiple_of(i * (rows * n), rows * n), rows * n), :],
                                     fill_sem)
        fill.start()
        fill.wait()


def _expert_mlp(x_slabs, blk_expert, row_src, row_dst, n_used, w_gate, w_up, w_down, layer):
    _, n_e, d, f = w_gate.shape
    n_blk = blk_expert.shape[0]
    rows = MOE_ROWS
    n = d // LANES
    by_expert = lambda i, be, rs, rd, nu: (layer, be[i], 0, 0)
    grid_spec = pltpu.PrefetchScalarGridSpec(
        num_scalar_prefetch=4,
        grid=(n_blk,),
        in_specs=[pl.BlockSpec(memory_space=pl.ANY),
                  pl.BlockSpec((None, None, d, f), by_expert), pl.BlockSpec((None, None, d, f), by_expert),
                  pl.BlockSpec((None, None, f, d), by_expert)],
        out_specs=pl.BlockSpec(memory_space=pl.ANY),
        scratch_shapes=[pltpu.VMEM((3, rows * n, LANES), F32), pltpu.VMEM((2, rows * n, LANES), F32),
                        pltpu.VMEM((d, f), BF16), pltpu.VMEM((d, f), BF16), pltpu.VMEM((f, d), BF16),
                        pltpu.SemaphoreType.DMA((3,)), pltpu.SemaphoreType.DMA((2,)), pltpu.SemaphoreType.DMA(())],
    )
    return pl.pallas_call(
        _expert_kernel,
        grid_spec=grid_spec,
        out_shape=jax.ShapeDtypeStruct((n_blk * rows * n, LANES), F32),
        compiler_params=_params("arbitrary"),
        name="moe_experts",
    )(blk_expert, row_src, row_dst, n_used, x_slabs, w_gate, w_up, w_down)


def _combine_ln_kernel(y0_ref, y1_ref, x_ref, wts_ref, g_ref, b_ref, xo_ref, xb_ref, *, alpha):
    tm, d = x_ref.shape
    wts = wts_ref[...]
    m = wts[:, 0:1] * _load_slabs(y0_ref, tm, d) + wts[:, 1:2] * _load_slabs(y1_ref, tm, d)
    out = _layer_norm(alpha * x_ref[...] + m, g_ref[...], b_ref[...])
    xo_ref[...] = out
    xb_ref[...] = out.astype(BF16)


def _combine_ln(y_slabs, x, wts, g, b, *, alpha, tm):
    t, d = x.shape
    tm = min(tm, t)
    n = d // LANES
    assert t % tm == 0
    row = lambda i: (i, 0)
    fixed = lambda i: (0, 0)
    return pl.pallas_call(
        functools.partial(_combine_ln_kernel, alpha=alpha),
        grid=(t // tm,),
        in_specs=[pl.BlockSpec((tm * n, LANES), row), pl.BlockSpec((tm * n, LANES), lambda i: (t // tm + i, 0)),
                  pl.BlockSpec((tm, d), row), pl.BlockSpec((tm, LANES), row),
                  pl.BlockSpec((1, d), fixed), pl.BlockSpec((1, d), fixed)],
        out_specs=[pl.BlockSpec((tm, d), row), pl.BlockSpec((tm, d), row)],
        out_shape=[jax.ShapeDtypeStruct((t, d), F32), jax.ShapeDtypeStruct((t, d), BF16)],
        compiler_params=_params("parallel"),
        name="moe_combine_ln",
    )(y_slabs, y_slabs, x, wts, g.reshape(1, d), b.reshape(1, d))


def _dispatch_plan(ids, counts, n_tokens):
    rows = MOE_ROWS
    n_assign = n_tokens * TOP_E
    eid = ids[:, 0:TOP_E].reshape(n_assign)
    rank = ids[:, TOP_E:2 * TOP_E].reshape(n_assign)
    counts = counts[0, :N_EXPERTS].astype(I32)
    starts = jnp.cumsum(counts) - counts
    padded = (counts + rows - 1) // rows * rows
    pends = jnp.cumsum(padded)
    pstarts = pends - padded
    pos = jnp.take(pstarts, eid) + rank
    n_blk = n_assign // rows + N_EXPERTS
    assign = jnp.full((n_blk * rows,), -1, I32).at[pos].set(jnp.arange(n_assign, dtype=I32))
    blk_expert = jnp.minimum(jnp.searchsorted(pends, jnp.arange(n_blk, dtype=I32) * rows, side='right'),
                             N_EXPERTS - 1).astype(I32)
    real_before = jnp.repeat(jnp.take(starts + counts, blk_expert), rows)
    pad_rank = jnp.arange(n_blk * rows, dtype=I32) - real_before
    is_pad = assign < 0
    row_src = jnp.where(is_pad, 0, assign // TOP_E)
    row_dst = jnp.where(is_pad, n_assign + pad_rank, (assign % TOP_E) * n_tokens + assign // TOP_E)
    n_used = (pends[-1] // rows).astype(I32).reshape(1)
    return row_src, row_dst, blk_expert, n_used


def _moe_ln(x, x_slabs, w_r, b_r, w_gate, w_up, w_down, layer, g, b, *, alpha):
    ids, wts, counts = _router(x, w_r, b_r, tm=256)
    row_src, row_dst, blk_expert, n_used = _dispatch_plan(ids, counts, x.shape[0])
    y_slabs = _expert_mlp(x_slabs, blk_expert, row_src, row_dst, n_used, w_gate, w_up, w_down, layer)
    return _combine_ln(y_slabs, x, wts, g, b, alpha=alpha, tm=256)


def kernel(x, moba_w_in, moba_w_o, hgrn_w_in, hgrn_g_norm, hgrn_lb_raw, hgrn_w_o, ln_mix_g, ln_mix_b,
           moe_w_rg, moe_b_rg, moe_w_re, moe_b_re, moe_w_gate, moe_w_up, moe_w_down, ln_ffn_g, ln_ffn_b):
    batch, seq, d = x.shape
    depth = ln_mix_g.shape[0]
    alpha = (2 * depth) ** 0.25
    t = batch * seq
    xf = x.reshape(t, d)
    xb = xf.astype(BF16)
    n_re = N_GROUPS * EXPERTS_PER_GROUP
    moba_w_o_b, hgrn_w_o_b = moba_w_o.astype(BF16), hgrn_w_o.astype(BF16)
    for layer in range(depth):
        j = layer // 2
        if layer % 2 == 0:
            qkv = _matmul(xb, moba_w_in, j, BF16, tm=1024, tn=512, scaled_cols=d, scale=HEAD_DIM ** -0.5)
            o = _moba_attention(qkv, batch, seq)
            w_o = moba_w_o_b
        else:
            proj = _matmul(xb, hgrn_w_in, j, F32, tm=1024, tn=512)
            o = _hgrn_mix(proj, hgrn_lb_raw, hgrn_g_norm[j], layer=j, batch=batch, seq=seq)
            w_o = hgrn_w_o_b
        xf, xs = _proj_ln(o, w_o, j, xf, ln_mix_g[layer], ln_mix_b[layer], alpha=alpha, tm=256)
        w_re = jnp.transpose(moe_w_re[layer], (1, 0, 2)).reshape(d, n_re)
        w_r = jnp.concatenate([moe_w_rg[layer], w_re, jnp.zeros((d, LANES - N_GROUPS - n_re), F32)], axis=1)
        b_r = jnp.concatenate([moe_b_rg[layer], moe_b_re[layer].reshape(n_re),
                               jnp.zeros((LANES - N_GROUPS - n_re,), F32)]).reshape(1, LANES)
        xf, xb = _moe_ln(xf, xs, w_r, b_r, moe_w_gate, moe_w_up, moe_w_down, layer,
                         ln_ffn_g[layer], ln_ffn_b[layer], alpha=alpha)
    return xf.reshape(batch, seq, d)
```

```python
import functools

import numpy as np
import jax
import jax.numpy as jnp
from jax import lax
from jax.experimental import pallas as pl
from jax.experimental.pallas import tpu as pltpu

F32 = jnp.float32
BF16 = jnp.bfloat16
I32 = jnp.int32

LANES = 128
SUBLANES = 8
VMEM_LIMIT_BYTES = 56 << 20

HEAD_DIM = 128
MOBA_BLOCK = 256
MOBA_TOPK = 3
HGRN_CHUNK = 64
N_GROUPS = 4
EXPERTS_PER_GROUP = 8
N_EXPERTS = N_GROUPS * EXPERTS_PER_GROUP
TOP_E = 2
LN_EPS = 1e-5
RMS_EPS = 1e-6

MOE_ROWS = 256
NT_DIMS = (((1,), (1,)), ((), ()))
TN_DIMS = (((0,), (0,)), ((), ()))


def _params(*semantics):
    return pltpu.CompilerParams(dimension_semantics=semantics, vmem_limit_bytes=VMEM_LIMIT_BYTES)


def _sigmoid(x):
    return 1.0 / (1.0 + jnp.exp(-x))


def _layer_norm(y, g, b):
    mu = jnp.mean(y, axis=-1, keepdims=True)
    d = y - mu
    var = jnp.mean(d * d, axis=-1, keepdims=True)
    return d * lax.rsqrt(var + LN_EPS) * g + b


def _matmul_kernel(x_ref, w_ref, o_ref, *, n_scaled_blocks, scale):
    acc = jnp.dot(x_ref[...], w_ref[...].astype(BF16), preferred_element_type=F32)
    if n_scaled_blocks:
        acc = acc * jnp.where(pl.program_id(1) < n_scaled_blocks, scale, 1.0).astype(F32)
    o_ref[...] = acc.astype(o_ref.dtype)


def _matmul(x, w_stack, layer, out_dtype, *, tm, tn, scaled_cols=0, scale=1.0):
    m, k = x.shape
    n = w_stack.shape[2]
    tm, tn = min(tm, m), min(tn, n)
    assert m % tm == 0 and n % tn == 0 and scaled_cols % tn == 0
    kern = functools.partial(_matmul_kernel, n_scaled_blocks=scaled_cols // tn, scale=scale)
    return pl.pallas_call(
        kern,
        grid=(m // tm, n // tn),
        in_specs=[pl.BlockSpec((tm, k), lambda i, j: (i, 0)),
                  pl.BlockSpec((None, k, tn), lambda i, j: (layer, 0, j))],
        out_specs=pl.BlockSpec((tm, tn), lambda i, j: (i, j)),
        out_shape=jax.ShapeDtypeStruct((m, n), out_dtype),
        compiler_params=_params("parallel", "parallel"),
        name="proj_matmul",
    )(x, w_stack)


def _store_slabs(slab_ref, x):
    rows, d = x.shape
    n = d // LANES
    for c in range(n):
        slab_ref[pl.ds(c, rows, stride=n), :] = x[:, c * LANES:(c + 1) * LANES]


def _load_slabs(slab_ref, rows, d):
    n = d // LANES
    return jnp.concatenate([slab_ref[pl.ds(c, rows, stride=n), :] for c in range(n)], axis=1)


def _proj_ln_kernel(o_ref, w_ref, x_ref, g_ref, b_ref, xo_ref, xs_ref, *, alpha):
    h = jnp.dot(o_ref[...], w_ref[...], preferred_element_type=F32)
    out = _layer_norm(alpha * x_ref[...] + h, g_ref[...], b_ref[...])
    xo_ref[...] = out
    _store_slabs(xs_ref, out)


def _proj_ln(o, w_stack, layer, x, g, b, *, alpha, tm):
    t, d = x.shape
    tm = min(tm, t)
    n = d // LANES
    assert t % tm == 0
    row = lambda i: (i, 0)
    fixed = lambda i: (0, 0)
    return pl.pallas_call(
        functools.partial(_proj_ln_kernel, alpha=alpha),
        grid=(t // tm,),
        in_specs=[pl.BlockSpec((tm, d), row), pl.BlockSpec((None, d, d), lambda i: (layer, 0, 0)),
                  pl.BlockSpec((tm, d), row), pl.BlockSpec((1, d), fixed), pl.BlockSpec((1, d), fixed)],
        out_specs=[pl.BlockSpec((tm, d), row), pl.BlockSpec((tm * n, LANES), row)],
        out_shape=[jax.ShapeDtypeStruct((t, d), F32), jax.ShapeDtypeStruct((t * n, LANES), F32)],
        compiler_params=_params("parallel"),
        name="proj_ln",
    )(o, w_stack, x, g.reshape(1, d), b.reshape(1, d))


MOBA_HEADS_PER_STEP = 4
MOBA_TILE = 2 * MOBA_BLOCK


def _moba_kernel(q_ref, k_ref, v_ref, o_ref, kmean_ref, vt_ref, sel_ref, *, n_blocks):
    blk, tile, hd = MOBA_BLOCK, MOBA_TILE, HEAD_DIM
    heads = range(MOBA_HEADS_PER_STEP)
    i = pl.program_id(2)

    @pl.when(i == 0)
    def _():
        kmean_ref[...] = jnp.zeros_like(kmean_ref)
        for h in heads:
            cols = slice(h * hd, (h + 1) * hd)
            for j in range(n_blocks):
                kj = k_ref[pl.ds(j * blk, blk), cols].astype(F32)
                kmean_ref[h, pl.ds(j, 1), :] = jnp.mean(kj, axis=0, keepdims=True)
            for p in range(n_blocks // 2):
                vt_ref[h, p] = v_ref[pl.ds(p * tile, tile), cols].T

    q = [q_ref[:, h * hd:(h + 1) * hd] for h in heads]
    for h in heads:
        gate = lax.dot_general(kmean_ref[h], q[h].astype(F32), NT_DIMS,
                               precision=lax.Precision.HIGHEST, preferred_element_type=F32)
        blk_idx = lax.broadcasted_iota(I32, gate.shape, 0)
        gate = jnp.where(blk_idx < i, gate, -jnp.inf)
        rank = jnp.zeros(gate.shape, I32)
        for jp in range(n_blocks):
            row = gate[jp:jp + 1, :]
            beats = (row > gate) | ((row == gate) & (jp < blk_idx))
            rank = rank + jnp.where(beats, 1, 0)
        sel_ref[h] = jnp.where((gate > -jnp.inf) & (rank < MOBA_TOPK), 1.0, 0.0)

    def scores(h, p):
        kt = k_ref[pl.ds(pl.multiple_of(p * tile, tile), tile), h * hd:(h + 1) * hd]
        return lax.dot_general(kt, q[h], NT_DIMS, preferred_element_type=F32)

    def masked(s, keep_top, keep_bot):
        return jnp.concatenate([jnp.where(keep_top, s[:blk], -jnp.inf),
                                jnp.where(keep_bot, s[blk:], -jnp.inf)], axis=0)

    p_own = i // 2
    causal = lax.broadcasted_iota(I32, (blk, blk), 0) <= lax.broadcasted_iota(I32, (blk, blk), 1)
    i_odd = jnp.full((blk, blk), i % 2, I32) == 1
    s_all = [scores(h, p_own) for h in heads]
    state, p_all = [], []
    for h in heads:
        chosen_prev = sel_ref[h, pl.ds(2 * p_own, 1), :] > 0.0
        s = masked(s_all[h], (i_odd & chosen_prev) | (~i_odd & causal), i_odd & causal)
        m = jnp.max(s, axis=0, keepdims=True)
        p = jnp.exp(s - m)
        state += [m, jnp.sum(p, axis=0, keepdims=True)]
        p_all.append(p.astype(BF16))
    for h in heads:
        acc = jnp.dot(vt_ref[h, p_own], p_all[h], preferred_element_type=F32)
        state.insert(3 * h + 2, acc)

    def body(pt, carry):
        s_all = [scores(h, pt) for h in heads]
        out, p_all, a_all = [], [], []
        for h in heads:
            m, l, _ = carry[3 * h:3 * h + 3]
            s = masked(s_all[h], sel_ref[h, pl.ds(2 * pt, 1), :] > 0.0,
                       sel_ref[h, pl.ds(2 * pt + 1, 1), :] > 0.0)
            m_new = jnp.maximum(m, jnp.max(s, axis=0, keepdims=True))
            a = jnp.exp(m - m_new)
            p = jnp.exp(s - m_new)
            out += [m_new, a * l + jnp.sum(p, axis=0, keepdims=True)]
            p_all.append(p.astype(BF16))
            a_all.append(a)
        for h in heads:
            acc = a_all[h] * carry[3 * h + 2] + jnp.dot(vt_ref[h, pt], p_all[h], preferred_element_type=F32)
            out.insert(3 * h + 2, acc)
        return tuple(out)

    state = lax.fori_loop(0, p_own, body, tuple(state))
    for h in heads:
        _, l, acc = state[3 * h:3 * h + 3]
        o_ref[:, h * hd:(h + 1) * hd] = (acc / l).T.astype(o_ref.dtype)


def _moba_attention(qkv, batch, seq):
    t, d3 = qkv.shape
    d = d3 // 3
    hs = MOBA_HEADS_PER_STEP
    n_head_groups = d // (HEAD_DIM * hs)
    n_blocks = seq // MOBA_BLOCK
    assert seq % MOBA_TILE == 0 and d % (HEAD_DIM * hs) == 0
    nb_pad = -(-n_blocks // SUBLANES) * SUBLANES
    kv_spec = lambda off: pl.BlockSpec((seq, HEAD_DIM * hs), lambda b, h, i: (b, off + h))
    q_spec = pl.BlockSpec((MOBA_BLOCK, HEAD_DIM * hs), lambda b, h, i: (b * n_blocks + i, h))
    return pl.pallas_call(
        functools.partial(_moba_kernel, n_blocks=n_blocks),
        grid=(batch, n_head_groups, n_blocks),
        in_specs=[q_spec, kv_spec(n_head_groups), kv_spec(2 * n_head_groups)],
        out_specs=q_spec,
        out_shape=jax.ShapeDtypeStruct((t, d), BF16),
        scratch_shapes=[pltpu.VMEM((hs, nb_pad, HEAD_DIM), F32),
                        pltpu.VMEM((hs, n_blocks // 2, HEAD_DIM, MOBA_TILE), BF16),
                        pltpu.VMEM((hs, nb_pad, MOBA_BLOCK), F32)],
        compiler_params=_params("parallel", "parallel", "arbitrary"),
        name="moba_attention",
    )(qkv, qkv, qkv)


_HGRN_LEVELS = (32, 16, 8, 4, 2, 1)
HGRN_GROUP = 4


def _hgrn_constants():
    c = HGRN_CHUNK
    idx = np.arange(c)
    t, s = idx[:, None], idx[None, :]
    sums = [(s <= t)]
    upper, pair = [], []
    for m in _HGRN_LEVELS:
        same = (t // (2 * m)) == (s // (2 * m))
        t_up, s_up = (t % (2 * m)) >= m, (s % (2 * m)) >= m
        up_rows = same & t_up & s_up & (s <= t)
        lo_rows = same & ~t_up & ~s_up & (s > t)
        if m > 1:
            sums.append(up_rows | lo_rows)
        upper.append(np.broadcast_to(t_up, (c, LANES)))
        pair.append(same & t_up & ~s_up)
    return (np.concatenate(sums, 0).astype(np.float32),
            np.stack(upper).astype(np.float32), np.stack(pair).astype(np.float32))


def _hgrn_kernel(q_ref, f_ref, i_ref, g_ref, lbraw_ref, gn_ref, sums_ref, upper_ref, pair_ref, o_ref,
                 qg_ref, oin_ref, u_ref, dl_ref, *, layer, n_chunks):
    c = HGRN_CHUNK
    dk = HEAD_DIM
    raw = lbraw_ref[...]
    e = jnp.exp(raw - jnp.max(raw, axis=0, keepdims=True))
    sm = e / jnp.sum(e, axis=0, keepdims=True)
    cum = sm[0:1, :]
    for l in range(1, layer + 1):
        cum = cum + sm[l:l + 1, :]
    lb = cum - sm[0:1, :]
    gnorm = gn_ref[...]
    eye = (lax.broadcasted_iota(I32, (c, c), 0) == lax.broadcasted_iota(I32, (c, c), 1))

    group = range(HGRN_GROUP)
    chunk = lambda x, n: x[n * c:(n + 1) * c]
    n_levels = len(_HGRN_LEVELS)

    def local(gi, carry):
        rows = pl.ds(pl.multiple_of(gi * (HGRN_GROUP * c), HGRN_GROUP * c), HGRN_GROUP * c)
        qv, fv, iv = q_ref[rows, :], f_ref[rows, :], i_ref[rows, :]
        qs = qv * _sigmoid(qv)
        fg = lb + (1.0 - lb) * _sigmoid(fv)
        log_f = jnp.log(fg)
        kk = 1.0 - fg
        iv_b = iv.astype(BF16)
        hi = log_f.astype(BF16)
        rest = log_f - hi.astype(F32)
        mid = rest.astype(BF16)
        lo = (rest - mid.astype(F32)).astype(BF16)
        parts3 = jnp.concatenate([chunk(t, n) for n in group for t in (hi, mid, lo)], axis=1)
        parts2 = jnp.concatenate([chunk(t, n) for n in group for t in (hi, mid)], axis=1)
        s3 = jnp.dot(sums_ref[0:c, :], parts3, preferred_element_type=F32)
        s2 = jnp.dot(sums_ref[c:, :], parts2, preferred_element_type=F32)
        gcum = jnp.concatenate([(s3[:, (3 * n) * dk:(3 * n + 1) * dk] + s3[:, (3 * n + 1) * dk:(3 * n + 2) * dk])
                                + s3[:, (3 * n + 2) * dk:(3 * n + 3) * dk] for n in group], axis=0)
        level_sums = [s2[:, (2 * n) * dk:(2 * n + 1) * dk] + s2[:, (2 * n + 1) * dk:(2 * n + 2) * dk] for n in group]
        qg_ref[rows, :] = (qs * jnp.exp(gcum)).astype(BF16)
        qe, ke = [], []
        for n in group:
            for li in range(n_levels):
                is_up = upper_ref[li] > 0.0
                if _HGRN_LEVELS[li] == 1:
                    dec = jnp.exp(jnp.where(is_up, chunk(log_f, n), 0.0))
                else:
                    dec = jnp.exp(level_sums[n][li * c:(li + 1) * c, :])
                qe.append(jnp.where(is_up, chunk(qs, n) * dec, 0.0).astype(BF16))
                ke.append(jnp.where(is_up, 0.0, chunk(kk, n) * dec).astype(BF16))
        prods = [lax.dot_general(a, b, NT_DIMS, preferred_element_type=F32) for a, b in zip(qe, ke)]
        diag = jnp.sum(qs * kk, axis=1, keepdims=True)
        a_all = []
        for n in group:
            a = jnp.where(eye, chunk(diag, n), 0.0)
            for li in range(n_levels):
                a = a + jnp.where(pair_ref[li] > 0.0, prods[n * n_levels + li], 0.0)
            a_all.append(a.astype(BF16))
        o_in = [jnp.dot(a_all[n], chunk(iv_b, n), preferred_element_type=F32) for n in group]
        oin_ref[rows, :] = jnp.concatenate(o_in, axis=0)
        g_last = [gcum[(n + 1) * c - 1:(n + 1) * c, :] for n in group]
        k_dec = [(chunk(kk, n) * jnp.exp(g_last[n] - chunk(gcum, n))).astype(BF16) for n in group]
        for n in group:
            u_ref[gi * HGRN_GROUP + n] = lax.dot_general(chunk(iv_b, n), k_dec[n], TN_DIMS,
                                                         preferred_element_type=F32)
        dl_ref[pl.ds(pl.multiple_of(gi * SUBLANES, SUBLANES), SUBLANES), :] = jnp.concatenate(
            [jnp.exp(g) for g in g_last] + [jnp.zeros((SUBLANES - HGRN_GROUP, dk), F32)], axis=0)
        return carry

    lax.fori_loop(0, n_chunks // HGRN_GROUP, local, 0)

    def recur(gi, state_t):
        rows = pl.ds(pl.multiple_of(gi * (HGRN_GROUP * c), HGRN_GROUP * c), HGRN_GROUP * c)
        decay = dl_ref[pl.ds(pl.multiple_of(gi * SUBLANES, SUBLANES), SUBLANES), :]
        states = []
        for n in group:
            states.append(state_t.astype(BF16))
            state_t = decay[n:n + 1, :] * state_t + u_ref[gi * HGRN_GROUP + n]
        qg = qg_ref[rows, :]
        o = oin_ref[rows, :] + jnp.concatenate(
            [lax.dot_general(chunk(qg, n), states[n], NT_DIMS, preferred_element_type=F32) for n in group], axis=0)
        gv = g_ref[rows, :]
        o = o * lax.rsqrt(jnp.mean(o * o, axis=1, keepdims=True) + RMS_EPS) * gnorm
        o_ref[rows, :] = (o * (gv * _sigmoid(gv))).astype(o_ref.dtype)
        return state_t

    lax.fori_loop(0, n_chunks // HGRN_GROUP, recur, jnp.zeros((HEAD_DIM, HEAD_DIM), F32))


def _hgrn_mix(proj, lb_raw, g_norm, *, layer, batch, seq):
    t, d4 = proj.shape
    d = d4 // 4
    n_heads = d // HEAD_DIM
    n_chunks = seq // HGRN_CHUNK
    assert seq % (HGRN_CHUNK * HGRN_GROUP) == 0 and HGRN_GROUP <= SUBLANES
    sums, upper, pair = _hgrn_constants()
    n_layers = lb_raw.shape[0]
    part = lambda off: pl.BlockSpec((seq, HEAD_DIM), lambda b, h: (b, off + h))
    whole = lambda a: pl.BlockSpec(a.shape, lambda b, h: (0,) * a.ndim)
    return pl.pallas_call(
        functools.partial(_hgrn_kernel, layer=layer, n_chunks=n_chunks),
        grid=(batch, n_heads),
        in_specs=[part(0), part(n_heads), part(2 * n_heads), part(3 * n_heads),
                  pl.BlockSpec((n_layers, HEAD_DIM), lambda b, h: (0, h)),
                  pl.BlockSpec((1, HEAD_DIM), lambda b, h: (0, 0)),
                  whole(sums), whole(upper), whole(pair)],
        out_specs=pl.BlockSpec((seq, HEAD_DIM), lambda b, h: (b, h)),
        out_shape=jax.ShapeDtypeStruct((t, d), BF16),
        scratch_shapes=[pltpu.VMEM((seq, HEAD_DIM), BF16),
                        pltpu.VMEM((seq, HEAD_DIM), F32),
                        pltpu.VMEM((n_chunks, HEAD_DIM, HEAD_DIM), F32),
                        pltpu.VMEM((n_chunks // HGRN_GROUP * SUBLANES, HEAD_DIM), F32)],
        compiler_params=_params("parallel", "parallel"),
        name="hgrn_recurrence",
    )(proj, proj, proj, proj, lb_raw, g_norm.reshape(1, HEAD_DIM),
      jnp.asarray(sums, BF16), jnp.asarray(upper), jnp.asarray(pair))


def _router_kernel(x_ref, w_ref, b_ref, ids_ref, wts_ref, counts_ref):
    g, eg = N_GROUPS, EXPERTS_PER_GROUP
    tm = x_ref.shape[0]

    @pl.when(pl.program_id(0) == 0)
    def _():
        counts_ref[...] = jnp.zeros_like(counts_ref)

    logits = jnp.dot(x_ref[...], w_ref[...], precision=lax.Precision.HIGHEST,
                     preferred_element_type=F32) + b_ref[...]
    lane = lax.broadcasted_iota(I32, logits.shape, 1)

    def softmax_over(mask):
        z = jnp.where(mask, logits, -jnp.inf)
        ez = jnp.exp(z - jnp.max(z, axis=1, keepdims=True))
        return ez / jnp.sum(ez, axis=1, keepdims=True)

    def top1(p, mask):
        best = jnp.max(jnp.where(mask, p, -1.0), axis=1, keepdims=True)
        where = jnp.min(jnp.where(mask & (p == best), lane, LANES), axis=1, keepdims=True)
        return best, where

    is_group = lane < g
    g_w, g_idx = top1(softmax_over(is_group), is_group)
    lo = g + g_idx * eg
    in_group = (lane >= lo) & (lane < lo + eg)
    pe = softmax_over(in_group)
    w1, i1 = top1(pe, in_group)
    w2, i2 = top1(pe, in_group & (lane != i1))
    den = w1 + w2
    wt1, wt2 = g_w * (w1 / den), g_w * (w2 / den)
    e1, e2 = i1 - g, i2 - g
    hot1, hot2 = lane == e1, lane == e2
    hot = jnp.where(hot1 | hot2, 1.0, 0.0)
    earlier = lax.broadcasted_iota(I32, (tm, tm), 1) < lax.broadcasted_iota(I32, (tm, tm), 0)
    before = counts_ref[...] + jnp.dot(jnp.where(earlier, 1.0, 0.0).astype(BF16), hot.astype(BF16),
                                       preferred_element_type=F32)
    r1 = jnp.sum(jnp.where(hot1, before, 0.0), axis=1, keepdims=True).astype(I32)
    r2 = jnp.sum(jnp.where(hot2, before, 0.0), axis=1, keepdims=True).astype(I32)
    counts_ref[...] += jnp.sum(hot, axis=0, keepdims=True)
    ids_ref[...] = jnp.where(lane == 0, e1, jnp.where(lane == 1, e2, jnp.where(lane == 2, r1,
                                                                               jnp.where(lane == 3, r2, 0))))
    wts_ref[...] = jnp.where(lane == 0, wt1, jnp.where(lane == 1, wt2, 0.0))


def _router(x, w_r, b_r, *, tm):
    t, d = x.shape
    tm = min(tm, t)
    assert t % tm == 0
    row = lambda i: (i, 0)
    fixed = lambda i: (0, 0)
    return pl.pallas_call(
        _router_kernel,
        grid=(t // tm,),
        in_specs=[pl.BlockSpec((tm, d), row), pl.BlockSpec((d, LANES), fixed), pl.BlockSpec((1, LANES), fixed)],
        out_specs=[pl.BlockSpec((tm, LANES), row), pl.BlockSpec((tm, LANES), row), pl.BlockSpec((1, LANES), fixed)],
        out_shape=[jax.ShapeDtypeStruct((t, LANES), I32), jax.ShapeDtypeStruct((t, LANES), F32),
                   jax.ShapeDtypeStruct((1, LANES), F32)],
        compiler_params=_params("arbitrary"),
        name="moe_router",
    )(x, w_r, b_r)


def _expert_kernel(blk_expert_ref, row_src_ref, row_dst_ref, n_used_ref, x_hbm, wg_ref, wu_ref, wd_ref, y_hbm,
                   xbuf, ybuf, wg_b, wu_b, wd_b, in_sems, out_sems, fill_sem):
    i = pl.program_id(0)
    n_blk = pl.num_programs(0)
    d = wg_ref.shape[0]
    n = d // LANES
    rows = xbuf.shape[1] // n
    n_in, n_out = xbuf.shape[0], ybuf.shape[0]
    n_used = n_used_ref[0]
    active = i < n_used

    def gather(block):
        s = block % n_in
        block = jnp.minimum(block, n_blk - 1)
        for r in range(rows):
            tok = pl.multiple_of(row_src_ref[block * rows + r] * n, n)
            pltpu.make_async_copy(x_hbm.at[pl.ds(tok, n), :], xbuf.at[s, pl.ds(r * n, n), :], in_sems.at[s]).start()

    def scatter(block):
        s = block % n_out
        for r in range(rows):
            dst = pl.multiple_of(row_dst_ref[block * rows + r] * n, n)
            pltpu.make_async_copy(ybuf.at[s, pl.ds(r * n, n), :], y_hbm.at[pl.ds(dst, n), :], out_sems.at[s]).start()

    def wait_gather(block):
        s = block % n_in
        pltpu.make_async_copy(x_hbm.at[pl.ds(0, rows * n), :], xbuf.at[s], in_sems.at[s]).wait()

    def wait_scatter(block):
        s = block % n_out
        pltpu.make_async_copy(ybuf.at[s], y_hbm.at[pl.ds(0, rows * n), :], out_sems.at[s]).wait()

    def drain(first_unused_block):
        wait_gather(first_unused_block)
        wait_gather(first_unused_block + 1)
        wait_scatter(first_unused_block - 1)

        @pl.when(first_unused_block >= 2)
        def _():
            wait_scatter(first_unused_block - 2)

    @pl.when(i == 0)
    def _():
        gather(0)
        gather(1)

    @pl.when(active)
    def _():
        wait_gather(i)

        @pl.when(i >= n_out)
        def _():
            wait_scatter(i - n_out)

        @pl.when((i == 0) | (blk_expert_ref[i] != blk_expert_ref[jnp.maximum(i - 1, 0)]))
        def _():
            wg_b[...] = wg_ref[...].astype(BF16)
            wu_b[...] = wu_ref[...].astype(BF16)
            wd_b[...] = wd_ref[...].astype(BF16)

        xb = _load_slabs(xbuf.at[i % n_in], rows, d).astype(BF16)
        hg = jnp.dot(xb, wg_b[...], preferred_element_type=F32)
        hu = jnp.dot(xb, wu_b[...], preferred_element_type=F32)
        h = (hg * _sigmoid(hg)) * hu
        _store_slabs(ybuf.at[i % n_out], jnp.dot(h.astype(BF16), wd_b[...], preferred_element_type=F32))
        scatter(i)
        gather(i + 2)

        @pl.when(i == n_blk - 1)
        def _():
            drain(i + 1)

    @pl.when(i == n_used)
    def _():
        drain(i)

    @pl.when(jnp.logical_not(active))
    def _():
        ybuf[0] = jnp.zeros(ybuf.shape[1:], F32)
        fill = pltpu.make_async_copy(ybuf.at[0], y_hbm.at[pl.ds(pl.multiple_of(i * (rows * n), rows * n), rows * n), :],
                                     fill_sem)
        fill.start()
        fill.wait()


def _expert_mlp(x_slabs, blk_expert, row_src, row_dst, n_used, w_gate, w_up, w_down, layer):
    _, n_e, d, f = w_gate.shape
    n_blk = blk_expert.shape[0]
    rows = MOE_ROWS
    n = d // LANES
    by_expert = lambda i, be, rs, rd, nu: (layer, be[i], 0, 0)
    grid_spec = pltpu.PrefetchScalarGridSpec(
        num_scalar_prefetch=4,
        grid=(n_blk,),
        in_specs=[pl.BlockSpec(memory_space=pl.ANY),
                  pl.BlockSpec((None, None, d, f), by_expert), pl.BlockSpec((None, None, d, f), by_expert),
                  pl.BlockSpec((None, None, f, d), by_expert)],
        out_specs=pl.BlockSpec(memory_space=pl.ANY),
        scratch_shapes=[pltpu.VMEM((3, rows * n, LANES), F32), pltpu.VMEM((2, rows * n, LANES), F32),
                        pltpu.VMEM((d, f), BF16), pltpu.VMEM((d, f), BF16), pltpu.VMEM((f, d), BF16),
                        pltpu.SemaphoreType.DMA((3,)), pltpu.SemaphoreType.DMA((2,)), pltpu.SemaphoreType.DMA(())],
    )
    return pl.pallas_call(
        _expert_kernel,
        grid_spec=grid_spec,
        out_shape=jax.ShapeDtypeStruct((n_blk * rows * n, LANES), F32),
        compiler_params=_params("arbitrary"),
        name="moe_experts",
    )(blk_expert, row_src, row_dst, n_used, x_slabs, w_gate, w_up, w_down)


def _combine_ln_kernel(y0_ref, y1_ref, x_ref, wts_ref, g_ref, b_ref, xo_ref, xb_ref, *, alpha):
    tm, d = x_ref.shape
    wts = wts_ref[...]
    m = wts[:, 0:1] * _load_slabs(y0_ref, tm, d) + wts[:, 1:2] * _load_slabs(y1_ref, tm, d)
    out = _layer_norm(alpha * x_ref[...] + m, g_ref[...], b_ref[...])
    xo_ref[...] = out
    xb_ref[...] = out.astype(BF16)


def _combine_ln(y_slabs, x, wts, g, b, *, alpha, tm):
    t, d = x.shape
    tm = min(tm, t)
    n = d // LANES
    assert t % tm == 0
    row = lambda i: (i, 0)
    fixed = lambda i: (0, 0)
    return pl.pallas_call(
        functools.partial(_combine_ln_kernel, alpha=alpha),
        grid=(t // tm,),
        in_specs=[pl.BlockSpec((tm * n, LANES), row), pl.BlockSpec((tm * n, LANES), lambda i: (t // tm + i, 0)),
                  pl.BlockSpec((tm, d), row), pl.BlockSpec((tm, LANES), row),
                  pl.BlockSpec((1, d), fixed), pl.BlockSpec((1, d), fixed)],
        out_specs=[pl.BlockSpec((tm, d), row), pl.BlockSpec((tm, d), row)],
        out_shape=[jax.ShapeDtypeStruct((t, d), F32), jax.ShapeDtypeStruct((t, d), BF16)],
        compiler_params=_params("parallel"),
        name="moe_combine_ln",
    )(y_slabs, y_slabs, x, wts, g.reshape(1, d), b.reshape(1, d))


def _dispatch_plan(ids, counts, n_tokens):
    rows = MOE_ROWS
    n_assign = n_tokens * TOP_E
    eid = ids[:, 0:TOP_E].reshape(n_assign)
    rank = ids[:, TOP_E:2 * TOP_E].reshape(n_assign)
    counts = counts[0, :N_EXPERTS].astype(I32)
    starts = jnp.cumsum(counts) - counts
    padded = (counts + rows - 1) // rows * rows
    pends = jnp.cumsum(padded)
    pstarts = pends - padded
    pos = jnp.take(pstarts, eid) + rank
    n_blk = n_assign // rows + N_EXPERTS
    assign = jnp.full((n_blk * rows,), -1, I32).at[pos].set(jnp.arange(n_assign, dtype=I32))
    blk_first = jnp.arange(n_blk, dtype=I32) * rows
    blk_expert = jnp.minimum(jnp.sum((pends[None, :] <= blk_first[:, None]).astype(I32), axis=1), N_EXPERTS - 1)
    real_before = jnp.repeat(jnp.take(starts + counts, blk_expert), rows)
    pad_rank = jnp.arange(n_blk * rows, dtype=I32) - real_before
    is_pad = assign < 0
    row_src = jnp.where(is_pad, 0, assign // TOP_E)
    row_dst = jnp.where(is_pad, n_assign + pad_rank, (assign % TOP_E) * n_tokens + assign // TOP_E)
    n_used = (pends[-1] // rows).astype(I32).reshape(1)
    return row_src, row_dst, blk_expert, n_used


def _moe_ln(x, x_slabs, w_r, b_r, w_gate, w_up, w_down, layer, g, b, *, alpha):
    ids, wts, counts = _router(x, w_r, b_r, tm=256)
    row_src, row_dst, blk_expert, n_used = _dispatch_plan(ids, counts, x.shape[0])
    y_slabs = _expert_mlp(x_slabs, blk_expert, row_src, row_dst, n_used, w_gate, w_up, w_down, layer)
    return _combine_ln(y_slabs, x, wts, g, b, alpha=alpha, tm=256)


def kernel(x, moba_w_in, moba_w_o, hgrn_w_in, hgrn_g_norm, hgrn_lb_raw, hgrn_w_o, ln_mix_g, ln_mix_b,
           moe_w_rg, moe_b_rg, moe_w_re, moe_b_re, moe_w_gate, moe_w_up, moe_w_down, ln_ffn_g, ln_ffn_b):
    batch, seq, d = x.shape
    depth = ln_mix_g.shape[0]
    alpha = (2 * depth) ** 0.25
    t = batch * seq
    xf = x.reshape(t, d)
    xb = xf.astype(BF16)
    n_re = N_GROUPS * EXPERTS_PER_GROUP
    moba_w_o_b, hgrn_w_o_b = moba_w_o.astype(BF16), hgrn_w_o.astype(BF16)
    for layer in range(depth):
        j = layer // 2
        if layer % 2 == 0:
            qkv = _matmul(xb, moba_w_in, j, BF16, tm=2048, tn=512, scaled_cols=d, scale=HEAD_DIM ** -0.5)
            o = _moba_attention(qkv, batch, seq)
            w_o = moba_w_o_b
        else:
            proj = _matmul(xb, hgrn_w_in, j, F32, tm=2048, tn=512)
            o = _hgrn_mix(proj, hgrn_lb_raw, hgrn_g_norm[j], layer=j, batch=batch, seq=seq)
            w_o = hgrn_w_o_b
        xf, xs = _proj_ln(o, w_o, j, xf, ln_mix_g[layer], ln_mix_b[layer], alpha=alpha, tm=256)
        w_re = jnp.transpose(moe_w_re[layer], (1, 0, 2)).reshape(d, n_re)
        w_r = jnp.concatenate([moe_w_rg[layer], w_re, jnp.zeros((d, LANES - N_GROUPS - n_re), F32)], axis=1)
        b_r = jnp.concatenate([moe_b_rg[layer], moe_b_re[layer].reshape(n_re),
                               jnp.zeros((LANES - N_GROUPS - n_re,), F32)]).reshape(1, LANES)
        xf, xb = _moe_ln(xf, xs, w_r, b_r, moe_w_gate, moe_w_up, moe_w_down, layer,
                         ln_ffn_g[layer], ln_ffn_b[layer], alpha=alpha)
    return xf.reshape(batch, seq, d)
```

```python
import functools

import numpy as np
import jax
import jax.numpy as jnp
from jax import lax
from jax.experimental import pallas as pl
from jax.experimental.pallas import tpu as pltpu

F32 = jnp.float32
BF16 = jnp.bfloat16
I32 = jnp.int32

LANES = 128
SUBLANES = 8
VMEM_LIMIT_BYTES = 56 << 20

HEAD_DIM = 128
MOBA_BLOCK = 256
MOBA_TOPK = 3
HGRN_CHUNK = 64
N_GROUPS = 4
EXPERTS_PER_GROUP = 8
N_EXPERTS = N_GROUPS * EXPERTS_PER_GROUP
TOP_E = 2
LN_EPS = 1e-5
RMS_EPS = 1e-6

MOE_ROWS = 256
NT_DIMS = (((1,), (1,)), ((), ()))
TN_DIMS = (((0,), (0,)), ((), ()))


def _params(*semantics):
    return pltpu.CompilerParams(dimension_semantics=semantics, vmem_limit_bytes=VMEM_LIMIT_BYTES)


def _sigmoid(x):
    return 1.0 / (1.0 + jnp.exp(-x))


def _layer_norm(y, g, b):
    mu = jnp.mean(y, axis=-1, keepdims=True)
    d = y - mu
    var = jnp.mean(d * d, axis=-1, keepdims=True)
    return d * lax.rsqrt(var + LN_EPS) * g + b


def _matmul_kernel(x_ref, w_ref, o_ref, *, n_scaled_blocks, scale):
    acc = jnp.dot(x_ref[...], w_ref[...].astype(BF16), preferred_element_type=F32)
    if n_scaled_blocks:
        acc = acc * jnp.where(pl.program_id(1) < n_scaled_blocks, scale, 1.0).astype(F32)
    o_ref[...] = acc.astype(o_ref.dtype)


def _matmul(x, w_stack, layer, out_dtype, *, tm, tn, scaled_cols=0, scale=1.0):
    m, k = x.shape
    n = w_stack.shape[2]
    tm, tn = min(tm, m), min(tn, n)
    assert m % tm == 0 and n % tn == 0 and scaled_cols % tn == 0
    kern = functools.partial(_matmul_kernel, n_scaled_blocks=scaled_cols // tn, scale=scale)
    return pl.pallas_call(
        kern,
        grid=(m // tm, n // tn),
        in_specs=[pl.BlockSpec((tm, k), lambda i, j: (i, 0)),
                  pl.BlockSpec((None, k, tn), lambda i, j: (layer, 0, j))],
        out_specs=pl.BlockSpec((tm, tn), lambda i, j: (i, j)),
        out_shape=jax.ShapeDtypeStruct((m, n), out_dtype),
        compiler_params=_params("parallel", "parallel"),
        name="proj_matmul",
    )(x, w_stack)


def _store_slabs(slab_ref, x):
    rows, d = x.shape
    n = d // LANES
    for c in range(n):
        slab_ref[pl.ds(c, rows, stride=n), :] = x[:, c * LANES:(c + 1) * LANES]


def _load_slabs(slab_ref, rows, d):
    n = d // LANES
    return jnp.concatenate([slab_ref[pl.ds(c, rows, stride=n), :] for c in range(n)], axis=1)


def _proj_ln_kernel(o_ref, w_ref, x_ref, g_ref, b_ref, xo_ref, xs_ref, *, alpha):
    h = jnp.dot(o_ref[...], w_ref[...], preferred_element_type=F32)
    out = _layer_norm(alpha * x_ref[...] + h, g_ref[...], b_ref[...])
    xo_ref[...] = out
    _store_slabs(xs_ref, out)


def _proj_ln(o, w_stack, layer, x, g, b, *, alpha, tm):
    t, d = x.shape
    tm = min(tm, t)
    n = d // LANES
    assert t % tm == 0
    row = lambda i: (i, 0)
    fixed = lambda i: (0, 0)
    return pl.pallas_call(
        functools.partial(_proj_ln_kernel, alpha=alpha),
        grid=(t // tm,),
        in_specs=[pl.BlockSpec((tm, d), row), pl.BlockSpec((None, d, d), lambda i: (layer, 0, 0)),
                  pl.BlockSpec((tm, d), row), pl.BlockSpec((1, d), fixed), pl.BlockSpec((1, d), fixed)],
        out_specs=[pl.BlockSpec((tm, d), row), pl.BlockSpec((tm * n, LANES), row)],
        out_shape=[jax.ShapeDtypeStruct((t, d), F32), jax.ShapeDtypeStruct((t * n, LANES), F32)],
        compiler_params=_params("parallel"),
        name="proj_ln",
    )(o, w_stack, x, g.reshape(1, d), b.reshape(1, d))


MOBA_HEADS_PER_STEP = 4
MOBA_TILE = 2 * MOBA_BLOCK


def _moba_kernel(q_ref, k_ref, v_ref, o_ref, kmean_ref, vt_ref, sel_ref, *, n_blocks):
    blk, tile, hd = MOBA_BLOCK, MOBA_TILE, HEAD_DIM
    heads = range(MOBA_HEADS_PER_STEP)
    i = pl.program_id(2)

    @pl.when(i == 0)
    def _():
        kmean_ref[...] = jnp.zeros_like(kmean_ref)
        for h in heads:
            cols = slice(h * hd, (h + 1) * hd)
            for j in range(n_blocks):
                kj = k_ref[pl.ds(j * blk, blk), cols].astype(F32)
                kmean_ref[h, pl.ds(j, 1), :] = jnp.mean(kj, axis=0, keepdims=True)
            for p in range(n_blocks // 2):
                vt_ref[h, p] = v_ref[pl.ds(p * tile, tile), cols].T

    q = [q_ref[:, h * hd:(h + 1) * hd] for h in heads]
    for h in heads:
        gate = lax.dot_general(kmean_ref[h], q[h].astype(F32), NT_DIMS,
                               precision=lax.Precision.HIGHEST, preferred_element_type=F32)
        blk_idx = lax.broadcasted_iota(I32, gate.shape, 0)
        gate = jnp.where(blk_idx < i, gate, -jnp.inf)
        rank = jnp.zeros(gate.shape, I32)
        for jp in range(n_blocks):
            row = gate[jp:jp + 1, :]
            beats = (row > gate) | ((row == gate) & (jp < blk_idx))
            rank = rank + jnp.where(beats, 1, 0)
        sel_ref[h] = jnp.where((gate > -jnp.inf) & (rank < MOBA_TOPK), 1.0, 0.0)

    def scores(h, p):
        kt = k_ref[pl.ds(pl.multiple_of(p * tile, tile), tile), h * hd:(h + 1) * hd]
        return lax.dot_general(kt, q[h], NT_DIMS, preferred_element_type=F32)

    def masked(s, keep_top, keep_bot):
        return jnp.concatenate([jnp.where(keep_top, s[:blk], -jnp.inf),
                                jnp.where(keep_bot, s[blk:], -jnp.inf)], axis=0)

    p_own = i // 2
    causal = lax.broadcasted_iota(I32, (blk, blk), 0) <= lax.broadcasted_iota(I32, (blk, blk), 1)
    i_odd = jnp.full((blk, blk), i % 2, I32) == 1
    s_all = [scores(h, p_own) for h in heads]
    state, p_all = [], []
    for h in heads:
        chosen_prev = sel_ref[h, pl.ds(2 * p_own, 1), :] > 0.0
        s = masked(s_all[h], (i_odd & chosen_prev) | (~i_odd & causal), i_odd & causal)
        m = jnp.max(s, axis=0, keepdims=True)
        p = jnp.exp(s - m)
        state += [m, jnp.sum(p, axis=0, keepdims=True)]
        p_all.append(p.astype(BF16))
    for h in heads:
        acc = jnp.dot(vt_ref[h, p_own], p_all[h], preferred_element_type=F32)
        state.insert(3 * h + 2, acc)

    def body(pt, carry):
        s_all = [scores(h, pt) for h in heads]
        out, p_all, a_all = [], [], []
        for h in heads:
            m, l, _ = carry[3 * h:3 * h + 3]
            s = masked(s_all[h], sel_ref[h, pl.ds(2 * pt, 1), :] > 0.0,
                       sel_ref[h, pl.ds(2 * pt + 1, 1), :] > 0.0)
            m_new = jnp.maximum(m, jnp.max(s, axis=0, keepdims=True))
            a = jnp.exp(m - m_new)
            p = jnp.exp(s - m_new)
            out += [m_new, a * l + jnp.sum(p, axis=0, keepdims=True)]
            p_all.append(p.astype(BF16))
            a_all.append(a)
        for h in heads:
            acc = a_all[h] * carry[3 * h + 2] + jnp.dot(vt_ref[h, pt], p_all[h], preferred_element_type=F32)
            out.insert(3 * h + 2, acc)
        return tuple(out)

    state = lax.fori_loop(0, p_own, body, tuple(state))
    for h in heads:
        _, l, acc = state[3 * h:3 * h + 3]
        o_ref[:, h * hd:(h + 1) * hd] = (acc / l).T.astype(o_ref.dtype)


def _moba_attention(qkv, batch, seq):
    t, d3 = qkv.shape
    d = d3 // 3
    hs = MOBA_HEADS_PER_STEP
    n_head_groups = d // (HEAD_DIM * hs)
    n_blocks = seq // MOBA_BLOCK
    assert seq % MOBA_TILE == 0 and d % (HEAD_DIM * hs) == 0
    nb_pad = -(-n_blocks // SUBLANES) * SUBLANES
    kv_spec = lambda off: pl.BlockSpec((seq, HEAD_DIM * hs), lambda b, h, i: (b, off + h))
    q_spec = pl.BlockSpec((MOBA_BLOCK, HEAD_DIM * hs), lambda b, h, i: (b * n_blocks + i, h))
    return pl.pallas_call(
        functools.partial(_moba_kernel, n_blocks=n_blocks),
        grid=(batch, n_head_groups, n_blocks),
        in_specs=[q_spec, kv_spec(n_head_groups), kv_spec(2 * n_head_groups)],
        out_specs=q_spec,
        out_shape=jax.ShapeDtypeStruct((t, d), BF16),
        scratch_shapes=[pltpu.VMEM((hs, nb_pad, HEAD_DIM), F32),
                        pltpu.VMEM((hs, n_blocks // 2, HEAD_DIM, MOBA_TILE), BF16),
                        pltpu.VMEM((hs, nb_pad, MOBA_BLOCK), F32)],
        compiler_params=_params("parallel", "parallel", "arbitrary"),
        name="moba_attention",
    )(qkv, qkv, qkv)


_HGRN_LEVELS = (32, 16, 8, 4, 2, 1)
HGRN_GROUP = 8


def _hgrn_constants():
    c = HGRN_CHUNK
    idx = np.arange(c)
    t, s = idx[:, None], idx[None, :]
    sums = [(s <= t)]
    upper, pair = [], []
    for m in _HGRN_LEVELS:
        same = (t // (2 * m)) == (s // (2 * m))
        t_up, s_up = (t % (2 * m)) >= m, (s % (2 * m)) >= m
        up_rows = same & t_up & s_up & (s <= t)
        lo_rows = same & ~t_up & ~s_up & (s > t)
        if m > 1:
            sums.append(up_rows | lo_rows)
        upper.append(np.broadcast_to(t_up, (c, LANES)))
        pair.append(same & t_up & ~s_up)
    return (np.concatenate(sums, 0).astype(np.float32),
            np.stack(upper).astype(np.float32), np.stack(pair).astype(np.float32))


def _hgrn_kernel(q_ref, f_ref, i_ref, g_ref, lbraw_ref, gn_ref, sums_ref, upper_ref, pair_ref, o_ref,
                 qg_ref, oin_ref, u_ref, dl_ref, *, layer, n_chunks):
    c = HGRN_CHUNK
    dk = HEAD_DIM
    raw = lbraw_ref[...]
    e = jnp.exp(raw - jnp.max(raw, axis=0, keepdims=True))
    sm = e / jnp.sum(e, axis=0, keepdims=True)
    cum = sm[0:1, :]
    for l in range(1, layer + 1):
        cum = cum + sm[l:l + 1, :]
    lb = cum - sm[0:1, :]
    gnorm = gn_ref[...]
    eye = (lax.broadcasted_iota(I32, (c, c), 0) == lax.broadcasted_iota(I32, (c, c), 1))

    group = range(HGRN_GROUP)
    chunk = lambda x, n: x[n * c:(n + 1) * c]
    n_levels = len(_HGRN_LEVELS)

    def local(gi, carry):
        rows = pl.ds(pl.multiple_of(gi * (HGRN_GROUP * c), HGRN_GROUP * c), HGRN_GROUP * c)
        qv, fv, iv = q_ref[rows, :], f_ref[rows, :], i_ref[rows, :]
        qs = qv * _sigmoid(qv)
        fg = lb + (1.0 - lb) * _sigmoid(fv)
        log_f = jnp.log(fg)
        kk = 1.0 - fg
        iv_b = iv.astype(BF16)
        hi = log_f.astype(BF16)
        rest = log_f - hi.astype(F32)
        mid = rest.astype(BF16)
        lo = (rest - mid.astype(F32)).astype(BF16)
        parts3 = jnp.concatenate([chunk(t, n) for n in group for t in (hi, mid, lo)], axis=1)
        parts2 = jnp.concatenate([chunk(t, n) for n in group for t in (hi, mid)], axis=1)
        s3 = jnp.dot(sums_ref[0:c, :], parts3, preferred_element_type=F32)
        s2 = jnp.dot(sums_ref[c:, :], parts2, preferred_element_type=F32)
        gcum = jnp.concatenate([(s3[:, (3 * n) * dk:(3 * n + 1) * dk] + s3[:, (3 * n + 1) * dk:(3 * n + 2) * dk])
                                + s3[:, (3 * n + 2) * dk:(3 * n + 3) * dk] for n in group], axis=0)
        level_sums = [s2[:, (2 * n) * dk:(2 * n + 1) * dk] + s2[:, (2 * n + 1) * dk:(2 * n + 2) * dk] for n in group]
        qg_ref[rows, :] = (qs * jnp.exp(gcum)).astype(BF16)
        qe, ke = [], []
        for n in group:
            for li in range(n_levels):
                is_up = upper_ref[li] > 0.0
                if _HGRN_LEVELS[li] == 1:
                    dec = jnp.exp(jnp.where(is_up, chunk(log_f, n), 0.0))
                else:
                    dec = jnp.exp(level_sums[n][li * c:(li + 1) * c, :])
                qe.append(jnp.where(is_up, chunk(qs, n) * dec, 0.0).astype(BF16))
                ke.append(jnp.where(is_up, 0.0, chunk(kk, n) * dec).astype(BF16))
        prods = [lax.dot_general(a, b, NT_DIMS, preferred_element_type=F32) for a, b in zip(qe, ke)]
        diag = jnp.sum(qs * kk, axis=1, keepdims=True)
        a_all = []
        for n in group:
            a = jnp.where(eye, chunk(diag, n), 0.0)
            for li in range(n_levels):
                a = a + jnp.where(pair_ref[li] > 0.0, prods[n * n_levels + li], 0.0)
            a_all.append(a.astype(BF16))
        o_in = [jnp.dot(a_all[n], chunk(iv_b, n), preferred_element_type=F32) for n in group]
        oin_ref[rows, :] = jnp.concatenate(o_in, axis=0)
        g_last = [gcum[(n + 1) * c - 1:(n + 1) * c, :] for n in group]
        k_dec = [(chunk(kk, n) * jnp.exp(g_last[n] - chunk(gcum, n))).astype(BF16) for n in group]
        for n in group:
            u_ref[gi * HGRN_GROUP + n] = lax.dot_general(chunk(iv_b, n), k_dec[n], TN_DIMS,
                                                         preferred_element_type=F32)
        dl_ref[pl.ds(pl.multiple_of(gi * SUBLANES, SUBLANES), SUBLANES), :] = jnp.concatenate(
            [jnp.exp(g) for g in g_last] + [jnp.zeros((1, dk), F32)] * (SUBLANES - HGRN_GROUP), axis=0)
        return carry

    lax.fori_loop(0, n_chunks // HGRN_GROUP, local, 0)

    def recur(gi, state_t):
        rows = pl.ds(pl.multiple_of(gi * (HGRN_GROUP * c), HGRN_GROUP * c), HGRN_GROUP * c)
        decay = dl_ref[pl.ds(pl.multiple_of(gi * SUBLANES, SUBLANES), SUBLANES), :]
        states = []
        for n in group:
            states.append(state_t.astype(BF16))
            state_t = decay[n:n + 1, :] * state_t + u_ref[gi * HGRN_GROUP + n]
        qg = qg_ref[rows, :]
        o = oin_ref[rows, :] + jnp.concatenate(
            [lax.dot_general(chunk(qg, n), states[n], NT_DIMS, preferred_element_type=F32) for n in group], axis=0)
        gv = g_ref[rows, :]
        o = o * lax.rsqrt(jnp.mean(o * o, axis=1, keepdims=True) + RMS_EPS) * gnorm
        o_ref[rows, :] = (o * (gv * _sigmoid(gv))).astype(o_ref.dtype)
        return state_t

    lax.fori_loop(0, n_chunks // HGRN_GROUP, recur, jnp.zeros((HEAD_DIM, HEAD_DIM), F32))


def _hgrn_mix(proj, lb_raw, g_norm, *, layer, batch, seq):
    t, d4 = proj.shape
    d = d4 // 4
    n_heads = d // HEAD_DIM
    n_chunks = seq // HGRN_CHUNK
    assert seq % (HGRN_CHUNK * HGRN_GROUP) == 0 and HGRN_GROUP <= SUBLANES
    sums, upper, pair = _hgrn_constants()
    n_layers = lb_raw.shape[0]
    part = lambda off: pl.BlockSpec((seq, HEAD_DIM), lambda b, h: (b, off + h))
    whole = lambda a: pl.BlockSpec(a.shape, lambda b, h: (0,) * a.ndim)
    return pl.pallas_call(
        functools.partial(_hgrn_kernel, layer=layer, n_chunks=n_chunks),
        grid=(batch, n_heads),
        in_specs=[part(0), part(n_heads), part(2 * n_heads), part(3 * n_heads),
                  pl.BlockSpec((n_layers, HEAD_DIM), lambda b, h: (0, h)),
                  pl.BlockSpec((1, HEAD_DIM), lambda b, h: (0, 0)),
                  whole(sums), whole(upper), whole(pair)],
        out_specs=pl.BlockSpec((seq, HEAD_DIM), lambda b, h: (b, h)),
        out_shape=jax.ShapeDtypeStruct((t, d), BF16),
        scratch_shapes=[pltpu.VMEM((seq, HEAD_DIM), BF16),
                        pltpu.VMEM((seq, HEAD_DIM), F32),
                        pltpu.VMEM((n_chunks, HEAD_DIM, HEAD_DIM), F32),
                        pltpu.VMEM((n_chunks // HGRN_GROUP * SUBLANES, HEAD_DIM), F32)],
        compiler_params=_params("parallel", "parallel"),
        name="hgrn_recurrence",
    )(proj, proj, proj, proj, lb_raw, g_norm.reshape(1, HEAD_DIM),
      jnp.asarray(sums, BF16), jnp.asarray(upper), jnp.asarray(pair))


def _router_kernel(x_ref, w_ref, b_ref, ids_ref, wts_ref, counts_ref):
    g, eg = N_GROUPS, EXPERTS_PER_GROUP
    tm = x_ref.shape[0]

    @pl.when(pl.program_id(0) == 0)
    def _():
        counts_ref[...] = jnp.zeros_like(counts_ref)

    x, w = x_ref[...], w_ref[...]
    x_hi, w_hi = x.astype(BF16), w.astype(BF16)
    x_lo, w_lo = (x - x_hi.astype(F32)).astype(BF16), (w - w_hi.astype(F32)).astype(BF16)
    logits = ((jnp.dot(x_hi, w_hi, preferred_element_type=F32) + jnp.dot(x_hi, w_lo, preferred_element_type=F32))
              + jnp.dot(x_lo, w_hi, preferred_element_type=F32)) + b_ref[...]
    lane = lax.broadcasted_iota(I32, logits.shape, 1)

    def softmax_over(mask):
        z = jnp.where(mask, logits, -jnp.inf)
        ez = jnp.exp(z - jnp.max(z, axis=1, keepdims=True))
        return ez / jnp.sum(ez, axis=1, keepdims=True)

    def top1(p, mask):
        best = jnp.max(jnp.where(mask, p, -1.0), axis=1, keepdims=True)
        where = jnp.min(jnp.where(mask & (p == best), lane, LANES), axis=1, keepdims=True)
        return best, where

    is_group = lane < g
    g_w, g_idx = top1(softmax_over(is_group), is_group)
    lo = g + g_idx * eg
    in_group = (lane >= lo) & (lane < lo + eg)
    pe = softmax_over(in_group)
    w1, i1 = top1(pe, in_group)
    w2, i2 = top1(pe, in_group & (lane != i1))
    den = w1 + w2
    wt1, wt2 = g_w * (w1 / den), g_w * (w2 / den)
    e1, e2 = i1 - g, i2 - g
    hot1, hot2 = lane == e1, lane == e2
    hot = jnp.where(hot1 | hot2, 1.0, 0.0)
    earlier = lax.broadcasted_iota(I32, (tm, tm), 1) < lax.broadcasted_iota(I32, (tm, tm), 0)
    before = counts_ref[...] + jnp.dot(jnp.where(earlier, 1.0, 0.0).astype(BF16), hot.astype(BF16),
                                       preferred_element_type=F32)
    r1 = jnp.sum(jnp.where(hot1, before, 0.0), axis=1, keepdims=True).astype(I32)
    r2 = jnp.sum(jnp.where(hot2, before, 0.0), axis=1, keepdims=True).astype(I32)
    counts_ref[...] += jnp.sum(hot, axis=0, keepdims=True)
    ids_ref[...] = jnp.where(lane == 0, e1, jnp.where(lane == 1, e2, jnp.where(lane == 2, r1,
                                                                               jnp.where(lane == 3, r2, 0))))
    wts_ref[...] = jnp.where(lane == 0, wt1, jnp.where(lane == 1, wt2, 0.0))


def _router(x, w_r, b_r, *, tm):
    t, d = x.shape
    tm = min(tm, t)
    assert t % tm == 0
    row = lambda i: (i, 0)
    fixed = lambda i: (0, 0)
    return pl.pallas_call(
        _router_kernel,
        grid=(t // tm,),
        in_specs=[pl.BlockSpec((tm, d), row), pl.BlockSpec((d, LANES), fixed), pl.BlockSpec((1, LANES), fixed)],
        out_specs=[pl.BlockSpec((tm, LANES), row), pl.BlockSpec((tm, LANES), row), pl.BlockSpec((1, LANES), fixed)],
        out_shape=[jax.ShapeDtypeStruct((t, LANES), I32), jax.ShapeDtypeStruct((t, LANES), F32),
                   jax.ShapeDtypeStruct((1, LANES), F32)],
        compiler_params=_params("arbitrary"),
        name="moe_router",
    )(x, w_r, b_r)


def _expert_kernel(blk_expert_ref, row_src_ref, row_dst_ref, n_used_ref, x_hbm, wg_ref, wu_ref, wd_ref, y_hbm,
                   xbuf, ybuf, wg_b, wu_b, wd_b, in_sems, out_sems, fill_sem):
    i = pl.program_id(0)
    n_blk = pl.num_programs(0)
    d = wg_ref.shape[0]
    n = d // LANES
    rows = xbuf.shape[1] // n
    n_in, n_out = xbuf.shape[0], ybuf.shape[0]
    n_used = n_used_ref[0]
    active = i < n_used

    def gather(block):
        s = block % n_in
        block = jnp.minimum(block, n_blk - 1)
        for r in range(rows):
            tok = pl.multiple_of(row_src_ref[block * rows + r] * n, n)
            pltpu.make_async_copy(x_hbm.at[pl.ds(tok, n), :], xbuf.at[s, pl.ds(r * n, n), :], in_sems.at[s]).start()

    def scatter(block):
        s = block % n_out
        for r in range(rows):
            dst = pl.multiple_of(row_dst_ref[block * rows + r] * n, n)
            pltpu.make_async_copy(ybuf.at[s, pl.ds(r * n, n), :], y_hbm.at[pl.ds(dst, n), :], out_sems.at[s]).start()

    def wait_gather(block):
        s = block % n_in
        pltpu.make_async_copy(x_hbm.at[pl.ds(0, rows * n), :], xbuf.at[s], in_sems.at[s]).wait()

    def wait_scatter(block):
        s = block % n_out
        pltpu.make_async_copy(ybuf.at[s], y_hbm.at[pl.ds(0, rows * n), :], out_sems.at[s]).wait()

    def drain(first_unused_block):
        wait_gather(first_unused_block)
        wait_gather(first_unused_block + 1)
        wait_scatter(first_unused_block - 1)

        @pl.when(first_unused_block >= 2)
        def _():
            wait_scatter(first_unused_block - 2)

    @pl.when(i == 0)
    def _():
        gather(0)
        gather(1)

    @pl.when(active)
    def _():
        wait_gather(i)

        @pl.when(i >= n_out)
        def _():
            wait_scatter(i - n_out)

        @pl.when((i == 0) | (blk_expert_ref[i] != blk_expert_ref[jnp.maximum(i - 1, 0)]))
        def _():
            wg_b[...] = wg_ref[...].astype(BF16)
            wu_b[...] = wu_ref[...].astype(BF16)
            wd_b[...] = wd_ref[...].astype(BF16)

        xb = _load_slabs(xbuf.at[i % n_in], rows, d).astype(BF16)
        hg = jnp.dot(xb, wg_b[...], preferred_element_type=F32)
        hu = jnp.dot(xb, wu_b[...], preferred_element_type=F32)
        h = (hg * _sigmoid(hg)) * hu
        _store_slabs(ybuf.at[i % n_out], jnp.dot(h.astype(BF16), wd_b[...], preferred_element_type=F32))
        scatter(i)
        gather(i + 2)

        @pl.when(i == n_blk - 1)
        def _():
            drain(i + 1)

    @pl.when(i == n_used)
    def _():
        drain(i)

    @pl.when(jnp.logical_not(active))
    def _():
        ybuf[0] = jnp.zeros(ybuf.shape[1:], F32)
        fill = pltpu.make_async_copy(ybuf.at[0], y_hbm.at[pl.ds(pl.multiple_of(i * (rows * n), rows * n), rows * n), :],
                                     fill_sem)
        fill.start()
        fill.wait()


def _expert_mlp(x_slabs, blk_expert, row_src, row_dst, n_used, w_gate, w_up, w_down, layer):
    _, n_e, d, f = w_gate.shape
    n_blk = blk_expert.shape[0]
    rows = MOE_ROWS
    n = d // LANES
    by_expert = lambda i, be, rs, rd, nu: (layer, be[i], 0, 0)
    grid_spec = pltpu.PrefetchScalarGridSpec(
        num_scalar_prefetch=4,
        grid=(n_blk,),
        in_specs=[pl.BlockSpec(memory_space=pl.ANY),
                  pl.BlockSpec((None, None, d, f), by_expert), pl.BlockSpec((None, None, d, f), by_expert),
                  pl.BlockSpec((None, None, f, d), by_expert)],
        out_specs=pl.BlockSpec(memory_space=pl.ANY),
        scratch_shapes=[pltpu.VMEM((3, rows * n, LANES), F32), pltpu.VMEM((2, rows * n, LANES), F32),
                        pltpu.VMEM((d, f), BF16), pltpu.VMEM((d, f), BF16), pltpu.VMEM((f, d), BF16),
                        pltpu.SemaphoreType.DMA((3,)), pltpu.SemaphoreType.DMA((2,)), pltpu.SemaphoreType.DMA(())],
    )
    return pl.pallas_call(
        _expert_kernel,
        grid_spec=grid_spec,
        out_shape=jax.ShapeDtypeStruct((n_blk * rows * n, LANES), F32),
        compiler_params=_params("arbitrary"),
        name="moe_experts",
    )(blk_expert, row_src, row_dst, n_used, x_slabs, w_gate, w_up, w_down)


def _combine_ln_kernel(y0_ref, y1_ref, x_ref, wts_ref, g_ref, b_ref, xo_ref, xb_ref, *, alpha):
    tm, d = x_ref.shape
    wts = wts_ref[...]
    m = wts[:, 0:1] * _load_slabs(y0_ref, tm, d) + wts[:, 1:2] * _load_slabs(y1_ref, tm, d)
    out = _layer_norm(alpha * x_ref[...] + m, g_ref[...], b_ref[...])
    xo_ref[...] = out
    xb_ref[...] = out.astype(BF16)


def _combine_ln(y_slabs, x, wts, g, b, *, alpha, tm):
    t, d = x.shape
    tm = min(tm, t)
    n = d // LANES
    assert t % tm == 0
    row = lambda i: (i, 0)
    fixed = lambda i: (0, 0)
    return pl.pallas_call(
        functools.partial(_combine_ln_kernel, alpha=alpha),
        grid=(t // tm,),
        in_specs=[pl.BlockSpec((tm * n, LANES), row), pl.BlockSpec((tm * n, LANES), lambda i: (t // tm + i, 0)),
                  pl.BlockSpec((tm, d), row), pl.BlockSpec((tm, LANES), row),
                  pl.BlockSpec((1, d), fixed), pl.BlockSpec((1, d), fixed)],
        out_specs=[pl.BlockSpec((tm, d), row), pl.BlockSpec((tm, d), row)],
        out_shape=[jax.ShapeDtypeStruct((t, d), F32), jax.ShapeDtypeStruct((t, d), BF16)],
        compiler_params=_params("parallel"),
        name="moe_combine_ln",
    )(y_slabs, y_slabs, x, wts, g.reshape(1, d), b.reshape(1, d))


def _dispatch_plan(ids, counts, n_tokens):
    rows = MOE_ROWS
    n_assign = n_tokens * TOP_E
    eid = ids[:, 0:TOP_E].reshape(n_assign)
    rank = ids[:, TOP_E:2 * TOP_E].reshape(n_assign)
    counts = counts[0, :N_EXPERTS].astype(I32)
    starts = jnp.cumsum(counts) - counts
    padded = (counts + rows - 1) // rows * rows
    pends = jnp.cumsum(padded)
    pstarts = pends - padded
    pos = jnp.take(pstarts, eid) + rank
    n_blk = n_assign // rows + N_EXPERTS
    assign = jnp.full((n_blk * rows,), -1, I32).at[pos].set(jnp.arange(n_assign, dtype=I32))
    blk_first = jnp.arange(n_blk, dtype=I32) * rows
    blk_expert = jnp.minimum(jnp.sum((pends[None, :] <= blk_first[:, None]).astype(I32), axis=1), N_EXPERTS - 1)
    real_before = jnp.repeat(jnp.take(starts + counts, blk_expert), rows)
    pad_rank = jnp.arange(n_blk * rows, dtype=I32) - real_before
    is_pad = assign < 0
    row_src = jnp.where(is_pad, 0, assign // TOP_E)
    row_dst = jnp.where(is_pad, n_assign + pad_rank, (assign % TOP_E) * n_tokens + assign // TOP_E)
    n_used = (pends[-1] // rows).astype(I32).reshape(1)
    return row_src, row_dst, blk_expert, n_used


def _moe_ln(x, x_slabs, w_r, b_r, w_gate, w_up, w_down, layer, g, b, *, alpha):
    ids, wts, counts = _router(x, w_r, b_r, tm=256)
    row_src, row_dst, blk_expert, n_used = _dispatch_plan(ids, counts, x.shape[0])
    y_slabs = _expert_mlp(x_slabs, blk_expert, row_src, row_dst, n_used, w_gate, w_up, w_down, layer)
    return _combine_ln(y_slabs, x, wts, g, b, alpha=alpha, tm=256)


def kernel(x, moba_w_in, moba_w_o, hgrn_w_in, hgrn_g_norm, hgrn_lb_raw, hgrn_w_o, ln_mix_g, ln_mix_b,
           moe_w_rg, moe_b_rg, moe_w_re, moe_b_re, moe_w_gate, moe_w_up, moe_w_down, ln_ffn_g, ln_ffn_b):
    batch, seq, d = x.shape
    depth = ln_mix_g.shape[0]
    alpha = (2 * depth) ** 0.25
    t = batch * seq
    xf = x.reshape(t, d)
    xb = xf.astype(BF16)
    n_re = N_GROUPS * EXPERTS_PER_GROUP
    moba_w_o_b, hgrn_w_o_b = moba_w_o.astype(BF16), hgrn_w_o.astype(BF16)
    for layer in range(depth):
        j = layer // 2
        if layer % 2 == 0:
            qkv = _matmul(xb, moba_w_in, j, BF16, tm=2048, tn=512, scaled_cols=d, scale=HEAD_DIM ** -0.5)
            o = _moba_attention(qkv, batch, seq)
            w_o = moba_w_o_b
        else:
            proj = _matmul(xb, hgrn_w_in, j, F32, tm=2048, tn=512)
            o = _hgrn_mix(proj, hgrn_lb_raw, hgrn_g_norm[j], layer=j, batch=batch, seq=seq)
            w_o = hgrn_w_o_b
        w_re = jnp.transpose(moe_w_re[layer], (1, 0, 2)).reshape(d, n_re)
        w_r = jnp.concatenate([moe_w_rg[layer], w_re, jnp.zeros((d, LANES - N_GROUPS - n_re), F32)], axis=1)
        b_r = jnp.concatenate([moe_b_rg[layer], moe_b_re[layer].reshape(n_re),
                               jnp.zeros((LANES - N_GROUPS - n_re,), F32)]).reshape(1, LANES)
        xf, xs = _proj_ln(o, w_o, j, xf, ln_mix_g[layer], ln_mix_b[layer], alpha=alpha, tm=256)
        xf, xb = _moe_ln(xf, xs, w_r, b_r, moe_w_gate, moe_w_up, moe_w_down, layer,
                         ln_ffn_g[layer], ln_ffn_b[layer], alpha=alpha)
    return xf.reshape(batch, seq, d)
```

```python
import functools

import numpy as np
import jax
import jax.numpy as jnp
from jax import lax
from jax.experimental import pallas as pl
from jax.experimental.pallas import tpu as pltpu

F32 = jnp.float32
BF16 = jnp.bfloat16
I32 = jnp.int32

LANES = 128
SUBLANES = 8
VMEM_LIMIT_BYTES = 56 << 20

HEAD_DIM = 128
MOBA_BLOCK = 256
MOBA_TOPK = 3
HGRN_CHUNK = 64
N_GROUPS = 4
EXPERTS_PER_GROUP = 8
N_EXPERTS = N_GROUPS * EXPERTS_PER_GROUP
TOP_E = 2
LN_EPS = 1e-5
RMS_EPS = 1e-6

MOE_ROWS = 256
NT_DIMS = (((1,), (1,)), ((), ()))
TN_DIMS = (((0,), (0,)), ((), ()))


def _params(*semantics):
    return pltpu.CompilerParams(dimension_semantics=semantics, vmem_limit_bytes=VMEM_LIMIT_BYTES)


def _sigmoid(x):
    return 1.0 / (1.0 + jnp.exp(-x))


def _layer_norm(y, g, b):
    mu = jnp.mean(y, axis=-1, keepdims=True)
    d = y - mu
    var = jnp.mean(d * d, axis=-1, keepdims=True)
    return d * lax.rsqrt(var + LN_EPS) * g + b


def _matmul_kernel(x_ref, w_ref, o_ref, *, n_scaled_blocks, scale):
    acc = jnp.dot(x_ref[...], w_ref[...].astype(BF16), preferred_element_type=F32)
    if n_scaled_blocks:
        acc = acc * jnp.where(pl.program_id(1) < n_scaled_blocks, scale, 1.0).astype(F32)
    o_ref[...] = acc.astype(o_ref.dtype)


def _matmul(x, w_stack, layer, out_dtype, *, tm, tn, scaled_cols=0, scale=1.0):
    m, k = x.shape
    n = w_stack.shape[2]
    tm, tn = min(tm, m), min(tn, n)
    assert m % tm == 0 and n % tn == 0 and scaled_cols % tn == 0
    kern = functools.partial(_matmul_kernel, n_scaled_blocks=scaled_cols // tn, scale=scale)
    return pl.pallas_call(
        kern,
        grid=(m // tm, n // tn),
        in_specs=[pl.BlockSpec((tm, k), lambda i, j: (i, 0)),
                  pl.BlockSpec((None, k, tn), lambda i, j: (layer, 0, j))],
        out_specs=pl.BlockSpec((tm, tn), lambda i, j: (i, j)),
        out_shape=jax.ShapeDtypeStruct((m, n), out_dtype),
        compiler_params=_params("parallel", "parallel"),
        name="proj_matmul",
    )(x, w_stack)


def _store_slabs(slab_ref, x):
    rows, d = x.shape
    n = d // LANES
    for c in range(n):
        slab_ref[pl.ds(c, rows, stride=n), :] = x[:, c * LANES:(c + 1) * LANES]


def _load_slabs(slab_ref, rows, d):
    n = d // LANES
    return jnp.concatenate([slab_ref[pl.ds(c, rows, stride=n), :] for c in range(n)], axis=1)


def _proj_ln_kernel(o_ref, w_ref, x_ref, g_ref, b_ref, xo_ref, xs_ref, *, alpha):
    h = jnp.dot(o_ref[...], w_ref[...], preferred_element_type=F32)
    out = _layer_norm(alpha * x_ref[...] + h, g_ref[...], b_ref[...])
    xo_ref[...] = out
    _store_slabs(xs_ref, out)


def _proj_ln(o, w_stack, layer, x, g, b, *, alpha, tm):
    t, d = x.shape
    tm = min(tm, t)
    n = d // LANES
    assert t % tm == 0
    row = lambda i: (i, 0)
    fixed = lambda i: (0, 0)
    return pl.pallas_call(
        functools.partial(_proj_ln_kernel, alpha=alpha),
        grid=(t // tm,),
        in_specs=[pl.BlockSpec((tm, d), row), pl.BlockSpec((None, d, d), lambda i: (layer, 0, 0)),
                  pl.BlockSpec((tm, d), row), pl.BlockSpec((1, d), fixed), pl.BlockSpec((1, d), fixed)],
        out_specs=[pl.BlockSpec((tm, d), row), pl.BlockSpec((tm * n, LANES), row)],
        out_shape=[jax.ShapeDtypeStruct((t, d), F32), jax.ShapeDtypeStruct((t * n, LANES), F32)],
        compiler_params=_params("parallel"),
        name="proj_ln",
    )(o, w_stack, x, g.reshape(1, d), b.reshape(1, d))


MOBA_HEADS_PER_STEP = 4
MOBA_TILE = 2 * MOBA_BLOCK
MOBA_SUM_ROWS = 2 * SUBLANES
LOG2_E = 1.4426950408889634


def _moba_kernel(q_ref, k_ref, v_ref, o_ref, kmean_ref, vt_ref, sel_ref, *, n_blocks):
    blk, tile, hd = MOBA_BLOCK, MOBA_TILE, HEAD_DIM
    heads = range(MOBA_HEADS_PER_STEP)
    i = pl.program_id(2)

    @pl.when(i == 0)
    def _():
        kmean_ref[...] = jnp.zeros_like(kmean_ref)
        for h in heads:
            cols = slice(h * hd, (h + 1) * hd)
            for j in range(n_blocks):
                kj = k_ref[pl.ds(j * blk, blk), cols].astype(F32)
                kmean_ref[h, pl.ds(j, 1), :] = jnp.mean(kj, axis=0, keepdims=True)
            for p in range(n_blocks // 2):
                vt_ref[h, p, 0:hd, :] = v_ref[pl.ds(p * tile, tile), cols].T
                vt_ref[h, p, hd:, :] = jnp.ones((MOBA_SUM_ROWS, tile), BF16)

    q = [q_ref[:, h * hd:(h + 1) * hd] for h in heads]
    for h in heads:
        gate = lax.dot_general(kmean_ref[h], q[h].astype(F32), NT_DIMS,
                               precision=lax.Precision.HIGHEST, preferred_element_type=F32)
        blk_idx = lax.broadcasted_iota(I32, gate.shape, 0)
        gate = jnp.where(blk_idx < i, gate, -jnp.inf)
        rank = jnp.zeros(gate.shape, I32)
        for jp in range(n_blocks):
            row = gate[jp:jp + 1, :]
            beats = (row > gate) | ((row == gate) & (jp < blk_idx))
            rank = rank + jnp.where(beats, 1, 0)
        sel_ref[h] = jnp.where((gate > -jnp.inf) & (rank < MOBA_TOPK), 1.0, 0.0)

    def scores(h, p):
        kt = k_ref[pl.ds(pl.multiple_of(p * tile, tile), tile), h * hd:(h + 1) * hd]
        return lax.dot_general(kt, q[h], NT_DIMS, preferred_element_type=F32)

    def masked(s, keep_top, keep_bot):
        return jnp.concatenate([jnp.where(keep_top, s[:blk], -jnp.inf),
                                jnp.where(keep_bot, s[blk:], -jnp.inf)], axis=0)

    p_own = i // 2
    causal = lax.broadcasted_iota(I32, (blk, blk), 0) <= lax.broadcasted_iota(I32, (blk, blk), 1)
    i_odd = jnp.full((blk, blk), i % 2, I32) == 1
    s_all = [scores(h, p_own) for h in heads]
    m_all, p_all = [], []
    for h in heads:
        chosen_prev = sel_ref[h, pl.ds(2 * p_own, 1), :] > 0.0
        s = masked(s_all[h], (i_odd & chosen_prev) | (~i_odd & causal), i_odd & causal)
        m = jnp.max(s, axis=0, keepdims=True)
        m_all.append(m)
        p_all.append(jnp.exp2(s - m).astype(BF16))
    state = []
    for h in heads:
        state += [m_all[h], jnp.dot(vt_ref[h, p_own], p_all[h], preferred_element_type=F32)]

    def body(pt, carry):
        s_all = [scores(h, pt) for h in heads]
        m_all, p_all, a_all = [], [], []
        for h in heads:
            m = carry[2 * h]
            s = masked(s_all[h], sel_ref[h, pl.ds(2 * pt, 1), :] > 0.0,
                       sel_ref[h, pl.ds(2 * pt + 1, 1), :] > 0.0)
            m_new = jnp.maximum(m, jnp.max(s, axis=0, keepdims=True))
            a_all.append(jnp.exp2(m - m_new))
            p_all.append(jnp.exp2(s - m_new).astype(BF16))
            m_all.append(m_new)
        out = []
        for h in heads:
            out += [m_all[h], a_all[h] * carry[2 * h + 1]
                    + jnp.dot(vt_ref[h, pt], p_all[h], preferred_element_type=F32)]
        return tuple(out)

    state = lax.fori_loop(0, p_own, body, tuple(state))
    for h in heads:
        acc = state[2 * h + 1]
        o_ref[:, h * hd:(h + 1) * hd] = (acc[0:hd, :] / acc[hd:hd + 1, :]).T.astype(o_ref.dtype)


def _moba_attention(qkv, batch, seq):
    t, d3 = qkv.shape
    d = d3 // 3
    hs = MOBA_HEADS_PER_STEP
    n_head_groups = d // (HEAD_DIM * hs)
    n_blocks = seq // MOBA_BLOCK
    assert seq % MOBA_TILE == 0 and d % (HEAD_DIM * hs) == 0
    nb_pad = -(-n_blocks // SUBLANES) * SUBLANES
    kv_spec = lambda off: pl.BlockSpec((seq, HEAD_DIM * hs), lambda b, h, i: (b, off + h))
    q_spec = pl.BlockSpec((MOBA_BLOCK, HEAD_DIM * hs), lambda b, h, i: (b * n_blocks + i, h))
    return pl.pallas_call(
        functools.partial(_moba_kernel, n_blocks=n_blocks),
        grid=(batch, n_head_groups, n_blocks),
        in_specs=[q_spec, kv_spec(n_head_groups), kv_spec(2 * n_head_groups)],
        out_specs=q_spec,
        out_shape=jax.ShapeDtypeStruct((t, d), BF16),
        scratch_shapes=[pltpu.VMEM((hs, nb_pad, HEAD_DIM), F32),
                        pltpu.VMEM((hs, n_blocks // 2, HEAD_DIM + MOBA_SUM_ROWS, MOBA_TILE), BF16),
                        pltpu.VMEM((hs, nb_pad, MOBA_BLOCK), F32)],
        compiler_params=_params("parallel", "parallel", "arbitrary"),
        name="moba_attention",
    )(qkv, qkv, qkv)


_HGRN_LEVELS = (32, 16, 8, 4, 2, 1)
HGRN_GROUP = 8


def _hgrn_constants():
    c = HGRN_CHUNK
    idx = np.arange(c)
    t, s = idx[:, None], idx[None, :]
    sums = [(s <= t)]
    upper, pair = [], []
    for m in _HGRN_LEVELS:
        same = (t // (2 * m)) == (s // (2 * m))
        t_up, s_up = (t % (2 * m)) >= m, (s % (2 * m)) >= m
        up_rows = same & t_up & s_up & (s <= t)
        lo_rows = same & ~t_up & ~s_up & (s > t)
        if m > 1:
            sums.append(up_rows | lo_rows)
        upper.append(np.broadcast_to(t_up, (c, LANES)))
        pair.append(same & t_up & ~s_up)
    return (np.concatenate(sums, 0).astype(np.float32),
            np.stack(upper).astype(np.float32), np.stack(pair).astype(np.float32))


def _hgrn_kernel(q_ref, f_ref, i_ref, g_ref, lbraw_ref, gn_ref, sums_ref, upper_ref, pair_ref, o_ref,
                 qg_ref, oin_ref, u_ref, dl_ref, *, layer, n_chunks):
    c = HGRN_CHUNK
    dk = HEAD_DIM
    raw = lbraw_ref[...]
    e = jnp.exp(raw - jnp.max(raw, axis=0, keepdims=True))
    sm = e / jnp.sum(e, axis=0, keepdims=True)
    cum = sm[0:1, :]
    for l in range(1, layer + 1):
        cum = cum + sm[l:l + 1, :]
    lb = cum - sm[0:1, :]
    gnorm = gn_ref[...]
    eye = (lax.broadcasted_iota(I32, (c, c), 0) == lax.broadcasted_iota(I32, (c, c), 1))

    group = range(HGRN_GROUP)
    chunk = lambda x, n: x[n * c:(n + 1) * c]
    n_levels = len(_HGRN_LEVELS)

    def local(gi, carry):
        rows = pl.ds(pl.multiple_of(gi * (HGRN_GROUP * c), HGRN_GROUP * c), HGRN_GROUP * c)
        qv, fv, iv = q_ref[rows, :], f_ref[rows, :], i_ref[rows, :]
        qs = qv * _sigmoid(qv)
        fg = lb + (1.0 - lb) * _sigmoid(fv)
        log_f = jnp.log(fg)
        kk = 1.0 - fg
        iv_b = iv.astype(BF16)
        hi = log_f.astype(BF16)
        rest = log_f - hi.astype(F32)
        mid = rest.astype(BF16)
        lo = (rest - mid.astype(F32)).astype(BF16)
        parts3 = jnp.concatenate([chunk(t, n) for n in group for t in (hi, mid, lo)], axis=1)
        parts2 = jnp.concatenate([chunk(t, n) for n in group for t in (hi, mid)], axis=1)
        s3 = jnp.dot(sums_ref[0:c, :], parts3, preferred_element_type=F32)
        s2 = jnp.dot(sums_ref[c:, :], parts2, preferred_element_type=F32)
        gcum = jnp.concatenate([(s3[:, (3 * n) * dk:(3 * n + 1) * dk] + s3[:, (3 * n + 1) * dk:(3 * n + 2) * dk])
                                + s3[:, (3 * n + 2) * dk:(3 * n + 3) * dk] for n in group], axis=0)
        level_sums = [s2[:, (2 * n) * dk:(2 * n + 1) * dk] + s2[:, (2 * n + 1) * dk:(2 * n + 2) * dk] for n in group]
        qg_ref[rows, :] = (qs * jnp.exp(gcum)).astype(BF16)
        qe, ke = [], []
        for n in group:
            for li in range(n_levels):
                is_up = upper_ref[li] > 0.0
                if _HGRN_LEVELS[li] == 1:
                    dec = jnp.exp(jnp.where(is_up, chunk(log_f, n), 0.0))
                else:
                    dec = jnp.exp(level_sums[n][li * c:(li + 1) * c, :])
                qe.append(jnp.where(is_up, chunk(qs, n) * dec, 0.0).astype(BF16))
                ke.append(jnp.where(is_up, 0.0, chunk(kk, n) * dec).astype(BF16))
        prods = [lax.dot_general(a, b, NT_DIMS, preferred_element_type=F32) for a, b in zip(qe, ke)]
        diag = jnp.sum(qs * kk, axis=1, keepdims=True)
        a_all = []
        for n in group:
            a = jnp.where(eye, chunk(diag, n), 0.0)
            for li in range(n_levels):
                a = a + jnp.where(pair_ref[li] > 0.0, prods[n * n_levels + li], 0.0)
            a_all.append(a.astype(BF16))
        o_in = [jnp.dot(a_all[n], chunk(iv_b, n), preferred_element_type=F32) for n in group]
        oin_ref[rows, :] = jnp.concatenate(o_in, axis=0)
        g_last = [gcum[(n + 1) * c - 1:(n + 1) * c, :] for n in group]
        k_dec = [(chunk(kk, n) * jnp.exp(g_last[n] - chunk(gcum, n))).astype(BF16) for n in group]
        for n in group:
            u_ref[gi * HGRN_GROUP + n] = lax.dot_general(chunk(iv_b, n), k_dec[n], TN_DIMS,
                                                         preferred_element_type=F32)
        dl_ref[pl.ds(pl.multiple_of(gi * SUBLANES, SUBLANES), SUBLANES), :] = jnp.concatenate(
            [jnp.exp(g) for g in g_last] + [jnp.zeros((1, dk), F32)] * (SUBLANES - HGRN_GROUP), axis=0)
        return carry

    lax.fori_loop(0, n_chunks // HGRN_GROUP, local, 0)

    def recur(gi, state_t):
        rows = pl.ds(pl.multiple_of(gi * (HGRN_GROUP * c), HGRN_GROUP * c), HGRN_GROUP * c)
        decay = dl_ref[pl.ds(pl.multiple_of(gi * SUBLANES, SUBLANES), SUBLANES), :]
        states = []
        for n in group:
            states.append(state_t.astype(BF16))
            state_t = decay[n:n + 1, :] * state_t + u_ref[gi * HGRN_GROUP + n]
        qg = qg_ref[rows, :]
        o = oin_ref[rows, :] + jnp.concatenate(
            [lax.dot_general(chunk(qg, n), states[n], NT_DIMS, preferred_element_type=F32) for n in group], axis=0)
        gv = g_ref[rows, :]
        o = o * lax.rsqrt(jnp.mean(o * o, axis=1, keepdims=True) + RMS_EPS) * gnorm
        o_ref[rows, :] = (o * (gv * _sigmoid(gv))).astype(o_ref.dtype)
        return state_t

    lax.fori_loop(0, n_chunks // HGRN_GROUP, recur, jnp.zeros((HEAD_DIM, HEAD_DIM), F32))


def _hgrn_mix(proj, lb_raw, g_norm, *, layer, batch, seq):
    t, d4 = proj.shape
    d = d4 // 4
    n_heads = d // HEAD_DIM
    n_chunks = seq // HGRN_CHUNK
    assert seq % (HGRN_CHUNK * HGRN_GROUP) == 0 and HGRN_GROUP <= SUBLANES
    sums, upper, pair = _hgrn_constants()
    n_layers = lb_raw.shape[0]
    part = lambda off: pl.BlockSpec((seq, HEAD_DIM), lambda b, h: (b, off + h))
    whole = lambda a: pl.BlockSpec(a.shape, lambda b, h: (0,) * a.ndim)
    return pl.pallas_call(
        functools.partial(_hgrn_kernel, layer=layer, n_chunks=n_chunks),
        grid=(batch, n_heads),
        in_specs=[part(0), part(n_heads), part(2 * n_heads), part(3 * n_heads),
                  pl.BlockSpec((n_layers, HEAD_DIM), lambda b, h: (0, h)),
                  pl.BlockSpec((1, HEAD_DIM), lambda b, h: (0, 0)),
                  whole(sums), whole(upper), whole(pair)],
        out_specs=pl.BlockSpec((seq, HEAD_DIM), lambda b, h: (b, h)),
        out_shape=jax.ShapeDtypeStruct((t, d), BF16),
        scratch_shapes=[pltpu.VMEM((seq, HEAD_DIM), BF16),
                        pltpu.VMEM((seq, HEAD_DIM), F32),
                        pltpu.VMEM((n_chunks, HEAD_DIM, HEAD_DIM), F32),
                        pltpu.VMEM((n_chunks // HGRN_GROUP * SUBLANES, HEAD_DIM), F32)],
        compiler_params=_params("parallel", "parallel"),
        name="hgrn_recurrence",
    )(proj, proj, proj, proj, lb_raw, g_norm.reshape(1, HEAD_DIM),
      jnp.asarray(sums, BF16), jnp.asarray(upper), jnp.asarray(pair))


def _router_kernel(x_ref, w_ref, b_ref, ids_ref, wts_ref, counts_ref):
    g, eg = N_GROUPS, EXPERTS_PER_GROUP
    tm = x_ref.shape[0]

    @pl.when(pl.program_id(0) == 0)
    def _():
        counts_ref[...] = jnp.zeros_like(counts_ref)

    x, w = x_ref[...], w_ref[...]
    x_hi, w_hi = x.astype(BF16), w.astype(BF16)
    x_lo, w_lo = (x - x_hi.astype(F32)).astype(BF16), (w - w_hi.astype(F32)).astype(BF16)
    logits = ((jnp.dot(x_hi, w_hi, preferred_element_type=F32) + jnp.dot(x_hi, w_lo, preferred_element_type=F32))
              + jnp.dot(x_lo, w_hi, preferred_element_type=F32)) + b_ref[...]
    lane = lax.broadcasted_iota(I32, logits.shape, 1)

    def softmax_over(mask):
        z = jnp.where(mask, logits, -jnp.inf)
        ez = jnp.exp(z - jnp.max(z, axis=1, keepdims=True))
        return ez / jnp.sum(ez, axis=1, keepdims=True)

    def top1(p, mask):
        best = jnp.max(jnp.where(mask, p, -1.0), axis=1, keepdims=True)
        where = jnp.min(jnp.where(mask & (p == best), lane, LANES), axis=1, keepdims=True)
        return best, where

    is_group = lane < g
    g_w, g_idx = top1(softmax_over(is_group), is_group)
    lo = g + g_idx * eg
    in_group = (lane >= lo) & (lane < lo + eg)
    pe = softmax_over(in_group)
    w1, i1 = top1(pe, in_group)
    w2, i2 = top1(pe, in_group & (lane != i1))
    den = w1 + w2
    wt1, wt2 = g_w * (w1 / den), g_w * (w2 / den)
    e1, e2 = i1 - g, i2 - g
    hot1, hot2 = lane == e1, lane == e2
    hot = jnp.where(hot1 | hot2, 1.0, 0.0)
    earlier = lax.broadcasted_iota(I32, (tm, tm), 1) < lax.broadcasted_iota(I32, (tm, tm), 0)
    before = counts_ref[...] + jnp.dot(jnp.where(earlier, 1.0, 0.0).astype(BF16), hot.astype(BF16),
                                       preferred_element_type=F32)
    r1 = jnp.sum(jnp.where(hot1, before, 0.0), axis=1, keepdims=True).astype(I32)
    r2 = jnp.sum(jnp.where(hot2, before, 0.0), axis=1, keepdims=True).astype(I32)
    counts_ref[...] += jnp.sum(hot, axis=0, keepdims=True)
    ids_ref[...] = jnp.where(lane == 0, e1, jnp.where(lane == 1, e2, jnp.where(lane == 2, r1,
                                                                               jnp.where(lane == 3, r2, 0))))
    wts_ref[...] = jnp.where(lane == 0, wt1, jnp.where(lane == 1, wt2, 0.0))


def _router(x, w_r, b_r, *, tm):
    t, d = x.shape
    tm = min(tm, t)
    assert t % tm == 0
    row = lambda i: (i, 0)
    fixed = lambda i: (0, 0)
    return pl.pallas_call(
        _router_kernel,
        grid=(t // tm,),
        in_specs=[pl.BlockSpec((tm, d), row), pl.BlockSpec((d, LANES), fixed), pl.BlockSpec((1, LANES), fixed)],
        out_specs=[pl.BlockSpec((tm, LANES), row), pl.BlockSpec((tm, LANES), row), pl.BlockSpec((1, LANES), fixed)],
        out_shape=[jax.ShapeDtypeStruct((t, LANES), I32), jax.ShapeDtypeStruct((t, LANES), F32),
                   jax.ShapeDtypeStruct((1, LANES), F32)],
        compiler_params=_params("arbitrary"),
        name="moe_router",
    )(x, w_r, b_r)


def _expert_kernel(blk_expert_ref, row_src_ref, row_dst_ref, n_used_ref, x_hbm, wg_ref, wu_ref, wd_ref, y_hbm,
                   xbuf, ybuf, wg_b, wu_b, wd_b, in_sems, out_sems, fill_sem):
    i = pl.program_id(0)
    n_blk = pl.num_programs(0)
    d = wg_ref.shape[0]
    n = d // LANES
    rows = xbuf.shape[1] // n
    n_in, n_out = xbuf.shape[0], ybuf.shape[0]
    n_used = n_used_ref[0]
    active = i < n_used

    def gather(block):
        s = block % n_in
        block = jnp.minimum(block, n_blk - 1)
        for r in range(rows):
            tok = pl.multiple_of(row_src_ref[block * rows + r] * n, n)
            pltpu.make_async_copy(x_hbm.at[pl.ds(tok, n), :], xbuf.at[s, pl.ds(r * n, n), :], in_sems.at[s]).start()

    def scatter(block):
        s = block % n_out
        for r in range(rows):
            dst = pl.multiple_of(row_dst_ref[block * rows + r] * n, n)
            pltpu.make_async_copy(ybuf.at[s, pl.ds(r * n, n), :], y_hbm.at[pl.ds(dst, n), :], out_sems.at[s]).start()

    def wait_gather(block):
        s = block % n_in
        pltpu.make_async_copy(x_hbm.at[pl.ds(0, rows * n), :], xbuf.at[s], in_sems.at[s]).wait()

    def wait_scatter(block):
        s = block % n_out
        pltpu.make_async_copy(ybuf.at[s], y_hbm.at[pl.ds(0, rows * n), :], out_sems.at[s]).wait()

    def drain(first_unused_block):
        wait_gather(first_unused_block)
        wait_gather(first_unused_block + 1)
        wait_scatter(first_unused_block - 1)

        @pl.when(first_unused_block >= 2)
        def _():
            wait_scatter(first_unused_block - 2)

    @pl.when(i == 0)
    def _():
        gather(0)
        gather(1)

    @pl.when(active)
    def _():
        wait_gather(i)

        @pl.when(i >= n_out)
        def _():
            wait_scatter(i - n_out)

        @pl.when((i == 0) | (blk_expert_ref[i] != blk_expert_ref[jnp.maximum(i - 1, 0)]))
        def _():
            wg_b[...] = wg_ref[...].astype(BF16)
            wu_b[...] = wu_ref[...].astype(BF16)
            wd_b[...] = wd_ref[...].astype(BF16)

        xb = _load_slabs(xbuf.at[i % n_in], rows, d).astype(BF16)
        hg = jnp.dot(xb, wg_b[...], preferred_element_type=F32)
        hu = jnp.dot(xb, wu_b[...], preferred_element_type=F32)
        h = (hg * _sigmoid(hg)) * hu
        _store_slabs(ybuf.at[i % n_out], jnp.dot(h.astype(BF16), wd_b[...], preferred_element_type=F32))
        scatter(i)
        gather(i + 2)

        @pl.when(i == n_blk - 1)
        def _():
            drain(i + 1)

    @pl.when(i == n_used)
    def _():
        drain(i)

    @pl.when(jnp.logical_not(active))
    def _():
        ybuf[0] = jnp.zeros(ybuf.shape[1:], F32)
        fill = pltpu.make_async_copy(ybuf.at[0], y_hbm.at[pl.ds(pl.multiple_of(i * (rows * n), rows * n), rows * n), :],
                                     fill_sem)
        fill.start()
        fill.wait()


def _expert_mlp(x_slabs, blk_expert, row_src, row_dst, n_used, w_gate, w_up, w_down, layer):
    _, n_e, d, f = w_gate.shape
    n_blk = blk_expert.shape[0]
    rows = MOE_ROWS
    n = d // LANES
    by_expert = lambda i, be, rs, rd, nu: (layer, be[i], 0, 0)
    grid_spec = pltpu.PrefetchScalarGridSpec(
        num_scalar_prefetch=4,
        grid=(n_blk,),
        in_specs=[pl.BlockSpec(memory_space=pl.ANY),
                  pl.BlockSpec((None, None, d, f), by_expert), pl.BlockSpec((None, None, d, f), by_expert),
                  pl.BlockSpec((None, None, f, d), by_expert)],
        out_specs=pl.BlockSpec(memory_space=pl.ANY),
        scratch_shapes=[pltpu.VMEM((3, rows * n, LANES), F32), pltpu.VMEM((2, rows * n, LANES), F32),
                        pltpu.VMEM((d, f), BF16), pltpu.VMEM((d, f), BF16), pltpu.VMEM((f, d), BF16),
                        pltpu.SemaphoreType.DMA((3,)), pltpu.SemaphoreType.DMA((2,)), pltpu.SemaphoreType.DMA(())],
    )
    return pl.pallas_call(
        _expert_kernel,
        grid_spec=grid_spec,
        out_shape=jax.ShapeDtypeStruct((n_blk * rows * n, LANES), F32),
        compiler_params=_params("arbitrary"),
        name="moe_experts",
    )(blk_expert, row_src, row_dst, n_used, x_slabs, w_gate, w_up, w_down)


def _combine_ln_kernel(y0_ref, y1_ref, x_ref, wts_ref, g_ref, b_ref, xo_ref, xb_ref, *, alpha):
    tm, d = x_ref.shape
    wts = wts_ref[...]
    m = wts[:, 0:1] * _load_slabs(y0_ref, tm, d) + wts[:, 1:2] * _load_slabs(y1_ref, tm, d)
    out = _layer_norm(alpha * x_ref[...] + m, g_ref[...], b_ref[...])
    xo_ref[...] = out
    xb_ref[...] = out.astype(BF16)


def _combine_ln(y_slabs, x, wts, g, b, *, alpha, tm):
    t, d = x.shape
    tm = min(tm, t)
    n = d // LANES
    assert t % tm == 0
    row = lambda i: (i, 0)
    fixed = lambda i: (0, 0)
    return pl.pallas_call(
        functools.partial(_combine_ln_kernel, alpha=alpha),
        grid=(t // tm,),
        in_specs=[pl.BlockSpec((tm * n, LANES), row), pl.BlockSpec((tm * n, LANES), lambda i: (t // tm + i, 0)),
                  pl.BlockSpec((tm, d), row), pl.BlockSpec((tm, LANES), row),
                  pl.BlockSpec((1, d), fixed), pl.BlockSpec((1, d), fixed)],
        out_specs=[pl.BlockSpec((tm, d), row), pl.BlockSpec((tm, d), row)],
        out_shape=[jax.ShapeDtypeStruct((t, d), F32), jax.ShapeDtypeStruct((t, d), BF16)],
        compiler_params=_params("parallel"),
        name="moe_combine_ln",
    )(y_slabs, y_slabs, x, wts, g.reshape(1, d), b.reshape(1, d))


def _dispatch_plan(ids, counts, n_tokens):
    rows = MOE_ROWS
    n_assign = n_tokens * TOP_E
    eid = ids[:, 0:TOP_E].reshape(n_assign)
    rank = ids[:, TOP_E:2 * TOP_E].reshape(n_assign)
    counts = counts[0, :N_EXPERTS].astype(I32)
    starts = jnp.cumsum(counts) - counts
    padded = (counts + rows - 1) // rows * rows
    pends = jnp.cumsum(padded)
    pstarts = pends - padded
    pos = jnp.take(pstarts, eid) + rank
    n_blk = n_assign // rows + N_EXPERTS
    assign = jnp.full((n_blk * rows,), -1, I32).at[pos].set(jnp.arange(n_assign, dtype=I32))
    blk_first = jnp.arange(n_blk, dtype=I32) * rows
    blk_expert = jnp.minimum(jnp.sum((pends[None, :] <= blk_first[:, None]).astype(I32), axis=1), N_EXPERTS - 1)
    real_before = jnp.repeat(jnp.take(starts + counts, blk_expert), rows)
    pad_rank = jnp.arange(n_blk * rows, dtype=I32) - real_before
    is_pad = assign < 0
    row_src = jnp.where(is_pad, 0, assign // TOP_E)
    row_dst = jnp.where(is_pad, n_assign + pad_rank, (assign % TOP_E) * n_tokens + assign // TOP_E)
    n_used = (pends[-1] // rows).astype(I32).reshape(1)
    return row_src, row_dst, blk_expert, n_used


def _moe_ln(x, x_slabs, w_r, b_r, w_gate, w_up, w_down, layer, g, b, *, alpha):
    ids, wts, counts = _router(x, w_r, b_r, tm=256)
    row_src, row_dst, blk_expert, n_used = _dispatch_plan(ids, counts, x.shape[0])
    y_slabs = _expert_mlp(x_slabs, blk_expert, row_src, row_dst, n_used, w_gate, w_up, w_down, layer)
    return _combine_ln(y_slabs, x, wts, g, b, alpha=alpha, tm=256)


def kernel(x, moba_w_in, moba_w_o, hgrn_w_in, hgrn_g_norm, hgrn_lb_raw, hgrn_w_o, ln_mix_g, ln_mix_b,
           moe_w_rg, moe_b_rg, moe_w_re, moe_b_re, moe_w_gate, moe_w_up, moe_w_down, ln_ffn_g, ln_ffn_b):
    batch, seq, d = x.shape
    depth = ln_mix_g.shape[0]
    alpha = (2 * depth) ** 0.25
    t = batch * seq
    xf = x.reshape(t, d)
    xb = xf.astype(BF16)
    n_re = N_GROUPS * EXPERTS_PER_GROUP
    moba_w_o_b, hgrn_w_o_b = moba_w_o.astype(BF16), hgrn_w_o.astype(BF16)
    for layer in range(depth):
        j = layer // 2
        if layer % 2 == 0:
            qkv = _matmul(xb, moba_w_in, j, BF16, tm=2048, tn=512, scaled_cols=d, scale=HEAD_DIM ** -0.5 * LOG2_E)
            o = _moba_attention(qkv, batch, seq)
            w_o = moba_w_o_b
        else:
            proj = _matmul(xb, hgrn_w_in, j, F32, tm=2048, tn=512)
            o = _hgrn_mix(proj, hgrn_lb_raw, hgrn_g_norm[j], layer=j, batch=batch, seq=seq)
            w_o = hgrn_w_o_b
        w_re = jnp.transpose(moe_w_re[layer], (1, 0, 2)).reshape(d, n_re)
        w_r = jnp.concatenate([moe_w_rg[layer], w_re, jnp.zeros((d, LANES - N_GROUPS - n_re), F32)], axis=1)
        b_r = jnp.concatenate([moe_b_rg[layer], moe_b_re[layer].reshape(n_re),
                               jnp.zeros((LANES - N_GROUPS - n_re,), F32)]).reshape(1, LANES)
        xf, xs = _proj_ln(o, w_o, j, xf, ln_mix_g[layer], ln_mix_b[layer], alpha=alpha, tm=256)
        xf, xb = _moe_ln(xf, xs, w_r, b_r, moe_w_gate, moe_w_up, moe_w_down, layer,
                         ln_ffn_g[layer], ln_ffn_b[layer], alpha=alpha)
    return xf.reshape(batch, seq, d)
```

```python
import functools

import numpy as np
import jax
import jax.numpy as jnp
from jax import lax
from jax.experimental import pallas as pl
from jax.experimental.pallas import tpu as pltpu

F32 = jnp.float32
BF16 = jnp.bfloat16
I32 = jnp.int32

LANES = 128
SUBLANES = 8
VMEM_LIMIT_BYTES = 56 << 20

HEAD_DIM = 128
MOBA_BLOCK = 256
MOBA_TOPK = 3
HGRN_CHUNK = 64
N_GROUPS = 4
EXPERTS_PER_GROUP = 8
N_EXPERTS = N_GROUPS * EXPERTS_PER_GROUP
TOP_E = 2
LN_EPS = 1e-5
RMS_EPS = 1e-6

MOE_ROWS = 256
NT_DIMS = (((1,), (1,)), ((), ()))
TN_DIMS = (((0,), (0,)), ((), ()))


def _params(*semantics):
    return pltpu.CompilerParams(dimension_semantics=semantics, vmem_limit_bytes=VMEM_LIMIT_BYTES)


def _sigmoid(x):
    return 1.0 / (1.0 + jnp.exp(-x))


def _layer_norm(y, g, b):
    mu = jnp.mean(y, axis=-1, keepdims=True)
    d = y - mu
    var = jnp.mean(d * d, axis=-1, keepdims=True)
    return d * lax.rsqrt(var + LN_EPS) * g + b


def _matmul_kernel(x_ref, w_ref, o_ref, *, n_scaled_blocks, scale):
    acc = jnp.dot(x_ref[...], w_ref[...].astype(BF16), preferred_element_type=F32)
    if n_scaled_blocks:
        acc = acc * jnp.where(pl.program_id(1) < n_scaled_blocks, scale, 1.0).astype(F32)
    o_ref[...] = acc.astype(o_ref.dtype)


def _matmul(x, w_stack, layer, out_dtype, *, tm, tn, scaled_cols=0, scale=1.0):
    m, k = x.shape
    n = w_stack.shape[2]
    tm, tn = min(tm, m), min(tn, n)
    assert m % tm == 0 and n % tn == 0 and scaled_cols % tn == 0
    kern = functools.partial(_matmul_kernel, n_scaled_blocks=scaled_cols // tn, scale=scale)
    return pl.pallas_call(
        kern,
        grid=(m // tm, n // tn),
        in_specs=[pl.BlockSpec((tm, k), lambda i, j: (i, 0)),
                  pl.BlockSpec((None, k, tn), lambda i, j: (layer, 0, j))],
        out_specs=pl.BlockSpec((tm, tn), lambda i, j: (i, j)),
        out_shape=jax.ShapeDtypeStruct((m, n), out_dtype),
        compiler_params=_params("parallel", "parallel"),
        name="proj_matmul",
    )(x, w_stack)


def _store_slabs(slab_ref, x):
    rows, d = x.shape
    n = d // LANES
    for c in range(n):
        slab_ref[pl.ds(c, rows, stride=n), :] = x[:, c * LANES:(c + 1) * LANES]


def _load_slabs(slab_ref, rows, d):
    n = d // LANES
    return jnp.concatenate([slab_ref[pl.ds(c, rows, stride=n), :] for c in range(n)], axis=1)


def _proj_ln_kernel(o_ref, w_ref, x_ref, g_ref, b_ref, xo_ref, xs_ref, *, alpha):
    h = jnp.dot(o_ref[...], w_ref[...], preferred_element_type=F32)
    out = _layer_norm(alpha * x_ref[...] + h, g_ref[...], b_ref[...])
    xo_ref[...] = out
    _store_slabs(xs_ref, out)


def _proj_ln(o, w_stack, layer, x, g, b, *, alpha, tm):
    t, d = x.shape
    tm = min(tm, t)
    n = d // LANES
    assert t % tm == 0
    row = lambda i: (i, 0)
    fixed = lambda i: (0, 0)
    return pl.pallas_call(
        functools.partial(_proj_ln_kernel, alpha=alpha),
        grid=(t // tm,),
        in_specs=[pl.BlockSpec((tm, d), row), pl.BlockSpec((None, d, d), lambda i: (layer, 0, 0)),
                  pl.BlockSpec((tm, d), row), pl.BlockSpec((1, d), fixed), pl.BlockSpec((1, d), fixed)],
        out_specs=[pl.BlockSpec((tm, d), row), pl.BlockSpec((tm * n, LANES), row)],
        out_shape=[jax.ShapeDtypeStruct((t, d), F32), jax.ShapeDtypeStruct((t * n, LANES), F32)],
        compiler_params=_params("parallel"),
        name="proj_ln",
    )(o, w_stack, x, g.reshape(1, d), b.reshape(1, d))


MOBA_HEADS_PER_STEP = 4
MOBA_TILE = 2 * MOBA_BLOCK
MOBA_SUM_ROWS = 2 * SUBLANES
LOG2_E = 1.4426950408889634


def _moba_kernel(q_ref, k_ref, v_ref, o_ref, kmean_ref, vt_ref, sel_ref, *, n_blocks):
    blk, tile, hd = MOBA_BLOCK, MOBA_TILE, HEAD_DIM
    heads = range(MOBA_HEADS_PER_STEP)
    i = pl.program_id(2)

    @pl.when(i == 0)
    def _():
        kmean_ref[...] = jnp.zeros_like(kmean_ref)
        for h in heads:
            cols = slice(h * hd, (h + 1) * hd)
            for j in range(n_blocks):
                kj = k_ref[pl.ds(j * blk, blk), cols].astype(F32)
                kmean_ref[h, pl.ds(j, 1), :] = jnp.mean(kj, axis=0, keepdims=True)
            for p in range(n_blocks // 2):
                vt_ref[h, p, 0:hd, :] = v_ref[pl.ds(p * tile, tile), cols].T
                vt_ref[h, p, hd:, :] = jnp.ones((MOBA_SUM_ROWS, tile), BF16)

    q = [q_ref[:, h * hd:(h + 1) * hd] for h in heads]
    for h in heads:
        gate = lax.dot_general(kmean_ref[h], q[h].astype(F32), NT_DIMS,
                               precision=lax.Precision.HIGHEST, preferred_element_type=F32)
        blk_idx = lax.broadcasted_iota(I32, gate.shape, 0)
        gate = jnp.where(blk_idx < i, gate, -jnp.inf)
        rank = jnp.zeros(gate.shape, I32)
        for jp in range(n_blocks):
            row = gate[jp:jp + 1, :]
            beats = (row > gate) | ((row == gate) & (jp < blk_idx))
            rank = rank + jnp.where(beats, 1, 0)
        sel_ref[h] = jnp.where((gate > -jnp.inf) & (rank < MOBA_TOPK), 1.0, 0.0)

    def scores(h, p):
        kt = k_ref[pl.ds(pl.multiple_of(p * tile, tile), tile), h * hd:(h + 1) * hd]
        return lax.dot_general(kt, q[h], NT_DIMS, preferred_element_type=F32)

    def masked(s, keep_top, keep_bot):
        return jnp.concatenate([jnp.where(keep_top, s[:blk], -jnp.inf),
                                jnp.where(keep_bot, s[blk:], -jnp.inf)], axis=0)

    p_own = i // 2
    causal = lax.broadcasted_iota(I32, (blk, blk), 0) <= lax.broadcasted_iota(I32, (blk, blk), 1)
    i_odd = jnp.full((blk, blk), i % 2, I32) == 1
    s_all = [scores(h, p_own) for h in heads]
    m_all, p_all = [], []
    for h in heads:
        chosen_prev = sel_ref[h, pl.ds(2 * p_own, 1), :] > 0.0
        s = masked(s_all[h], (i_odd & chosen_prev) | (~i_odd & causal), i_odd & causal)
        m = jnp.max(s, axis=0, keepdims=True)
        m_all.append(m)
        p_all.append(jnp.exp2(s - m).astype(BF16))
    state = []
    for h in heads:
        state += [m_all[h], jnp.dot(vt_ref[h, p_own], p_all[h], preferred_element_type=F32)]

    def body(pt, carry):
        s_all = [scores(h, pt) for h in heads]
        m_all, p_all, a_all = [], [], []
        for h in heads:
            m = carry[2 * h]
            s = masked(s_all[h], sel_ref[h, pl.ds(2 * pt, 1), :] > 0.0,
                       sel_ref[h, pl.ds(2 * pt + 1, 1), :] > 0.0)
            m_new = jnp.maximum(m, jnp.max(s, axis=0, keepdims=True))
            a_all.append(jnp.exp2(m - m_new))
            p_all.append(jnp.exp2(s - m_new).astype(BF16))
            m_all.append(m_new)
        out = []
        for h in heads:
            out += [m_all[h], a_all[h] * carry[2 * h + 1]
                    + jnp.dot(vt_ref[h, pt], p_all[h], preferred_element_type=F32)]
        return tuple(out)

    state = lax.fori_loop(0, p_own, body, tuple(state))
    for h in heads:
        acc = state[2 * h + 1]
        o_ref[:, h * hd:(h + 1) * hd] = (acc[0:hd, :] / acc[hd:hd + 1, :]).T.astype(o_ref.dtype)


def _moba_attention(qkv, batch, seq):
    t, d3 = qkv.shape
    d = d3 // 3
    hs = MOBA_HEADS_PER_STEP
    n_head_groups = d // (HEAD_DIM * hs)
    n_blocks = seq // MOBA_BLOCK
    assert seq % MOBA_TILE == 0 and d % (HEAD_DIM * hs) == 0
    nb_pad = -(-n_blocks // SUBLANES) * SUBLANES
    kv_spec = lambda off: pl.BlockSpec((seq, HEAD_DIM * hs), lambda b, h, i: (b, off + h))
    q_spec = pl.BlockSpec((MOBA_BLOCK, HEAD_DIM * hs), lambda b, h, i: (b * n_blocks + i, h))
    return pl.pallas_call(
        functools.partial(_moba_kernel, n_blocks=n_blocks),
        grid=(batch, n_head_groups, n_blocks),
        in_specs=[q_spec, kv_spec(n_head_groups), kv_spec(2 * n_head_groups)],
        out_specs=q_spec,
        out_shape=jax.ShapeDtypeStruct((t, d), BF16),
        scratch_shapes=[pltpu.VMEM((hs, nb_pad, HEAD_DIM), F32),
                        pltpu.VMEM((hs, n_blocks // 2, HEAD_DIM + MOBA_SUM_ROWS, MOBA_TILE), BF16),
                        pltpu.VMEM((hs, nb_pad, MOBA_BLOCK), F32)],
        compiler_params=_params("parallel", "parallel", "arbitrary"),
        name="moba_attention",
    )(qkv, qkv, qkv)


_HGRN_LEVELS = (32, 16, 8, 4, 2, 1)
HGRN_GROUP = 8


def _hgrn_constants():
    c = HGRN_CHUNK
    idx = np.arange(c)
    t, s = idx[:, None], idx[None, :]
    sums = [(s <= t)]
    upper, pair = [], []
    for m in _HGRN_LEVELS:
        same = (t // (2 * m)) == (s // (2 * m))
        t_up, s_up = (t % (2 * m)) >= m, (s % (2 * m)) >= m
        up_rows = same & t_up & s_up & (s <= t)
        lo_rows = same & ~t_up & ~s_up & (s > t)
        if m > 1:
            sums.append(up_rows | lo_rows)
        upper.append(np.broadcast_to(t_up, (c, LANES)))
        pair.append(same & t_up & ~s_up)
    return (np.concatenate(sums, 0).astype(np.float32),
            np.stack(upper).astype(np.float32), np.stack(pair).astype(np.float32))


def _hgrn_kernel(q_ref, f_ref, i_ref, g_ref, lbraw_ref, gn_ref, sums_ref, upper_ref, pair_ref, o_ref,
                 qg_ref, oin_ref, u_ref, dl_ref, *, layer, n_chunks):
    c = HGRN_CHUNK
    dk = HEAD_DIM
    raw = lbraw_ref[...]
    e = jnp.exp(raw - jnp.max(raw, axis=0, keepdims=True))
    sm = e / jnp.sum(e, axis=0, keepdims=True)
    cum = sm[0:1, :]
    for l in range(1, layer + 1):
        cum = cum + sm[l:l + 1, :]
    lb = cum - sm[0:1, :]
    gnorm = gn_ref[...]
    eye = (lax.broadcasted_iota(I32, (c, c), 0) == lax.broadcasted_iota(I32, (c, c), 1))

    group = range(HGRN_GROUP)
    chunk = lambda x, n: x[n * c:(n + 1) * c]
    n_levels = len(_HGRN_LEVELS)

    def local(gi, carry):
        rows = pl.ds(pl.multiple_of(gi * (HGRN_GROUP * c), HGRN_GROUP * c), HGRN_GROUP * c)
        qv, fv, iv = q_ref[rows, :], f_ref[rows, :], i_ref[rows, :]
        qs = qv * _sigmoid(qv)
        fg = lb + (1.0 - lb) * _sigmoid(fv)
        log_f = jnp.log(fg)
        kk = 1.0 - fg
        iv_b = iv.astype(BF16)
        hi = log_f.astype(BF16)
        rest = log_f - hi.astype(F32)
        mid = rest.astype(BF16)
        lo = (rest - mid.astype(F32)).astype(BF16)
        parts3 = jnp.concatenate([chunk(t, n) for n in group for t in (hi, mid, lo)], axis=1)
        parts2 = jnp.concatenate([chunk(t, n) for n in group for t in (hi, mid)], axis=1)
        s3 = jnp.dot(sums_ref[0:c, :], parts3, preferred_element_type=F32)
        s2 = jnp.dot(sums_ref[c:, :], parts2, preferred_element_type=F32)
        gcum = jnp.concatenate([(s3[:, (3 * n) * dk:(3 * n + 1) * dk] + s3[:, (3 * n + 1) * dk:(3 * n + 2) * dk])
                                + s3[:, (3 * n + 2) * dk:(3 * n + 3) * dk] for n in group], axis=0)
        level_sums = [s2[:, (2 * n) * dk:(2 * n + 1) * dk] + s2[:, (2 * n + 1) * dk:(2 * n + 2) * dk] for n in group]
        qg_ref[rows, :] = (qs * jnp.exp(gcum)).astype(BF16)
        qe, ke = [], []
        for n in group:
            for li in range(n_levels):
                is_up = upper_ref[li] > 0.0
                if _HGRN_LEVELS[li] == 1:
                    dec = jnp.exp(jnp.where(is_up, chunk(log_f, n), 0.0))
                else:
                    dec = jnp.exp(level_sums[n][li * c:(li + 1) * c, :])
                qe.append(jnp.where(is_up, chunk(qs, n) * dec, 0.0).astype(BF16))
                ke.append(jnp.where(is_up, 0.0, chunk(kk, n) * dec).astype(BF16))
        prods = [lax.dot_general(a, b, NT_DIMS, preferred_element_type=F32) for a, b in zip(qe, ke)]
        diag = jnp.sum(qs * kk, axis=1, keepdims=True)
        a_all = []
        for n in group:
            a = jnp.where(eye, chunk(diag, n), 0.0)
            for li in range(n_levels):
                a = a + jnp.where(pair_ref[li] > 0.0, prods[n * n_levels + li], 0.0)
            a_all.append(a.astype(BF16))
        o_in = [jnp.dot(a_all[n], chunk(iv_b, n), preferred_element_type=F32) for n in group]
        oin_ref[rows, :] = jnp.concatenate(o_in, axis=0)
        g_last = [gcum[(n + 1) * c - 1:(n + 1) * c, :] for n in group]
        k_dec = [(chunk(kk, n) * jnp.exp(g_last[n] - chunk(gcum, n))).astype(BF16) for n in group]
        for n in group:
            u_ref[gi * HGRN_GROUP + n] = lax.dot_general(chunk(iv_b, n), k_dec[n], TN_DIMS,
                                                         preferred_element_type=F32)
        dl_ref[pl.ds(pl.multiple_of(gi * SUBLANES, SUBLANES), SUBLANES), :] = jnp.concatenate(
            [jnp.exp(g) for g in g_last] + [jnp.zeros((1, dk), F32)] * (SUBLANES - HGRN_GROUP), axis=0)
        return carry

    lax.fori_loop(0, n_chunks // HGRN_GROUP, local, 0)

    def recur(gi, state_t):
        rows = pl.ds(pl.multiple_of(gi * (HGRN_GROUP * c), HGRN_GROUP * c), HGRN_GROUP * c)
        decay = dl_ref[pl.ds(pl.multiple_of(gi * SUBLANES, SUBLANES), SUBLANES), :]
        states = []
        for n in group:
            states.append(state_t.astype(BF16))
            state_t = decay[n:n + 1, :] * state_t + u_ref[gi * HGRN_GROUP + n]
        qg = qg_ref[rows, :]
        o = oin_ref[rows, :] + jnp.concatenate(
            [lax.dot_general(chunk(qg, n), states[n], NT_DIMS, preferred_element_type=F32) for n in group], axis=0)
        gv = g_ref[rows, :]
        o = o * lax.rsqrt(jnp.mean(o * o, axis=1, keepdims=True) + RMS_EPS) * gnorm
        o_ref[rows, :] = (o * (gv * _sigmoid(gv))).astype(o_ref.dtype)
        return state_t

    lax.fori_loop(0, n_chunks // HGRN_GROUP, recur, jnp.zeros((HEAD_DIM, HEAD_DIM), F32))


def _hgrn_mix(proj, lb_raw, g_norm, *, layer, batch, seq):
    t, d4 = proj.shape
    d = d4 // 4
    n_heads = d // HEAD_DIM
    n_chunks = seq // HGRN_CHUNK
    assert seq % (HGRN_CHUNK * HGRN_GROUP) == 0 and HGRN_GROUP <= SUBLANES
    sums, upper, pair = _hgrn_constants()
    n_layers = lb_raw.shape[0]
    part = lambda off: pl.BlockSpec((seq, HEAD_DIM), lambda b, h: (b, off + h))
    whole = lambda a: pl.BlockSpec(a.shape, lambda b, h: (0,) * a.ndim)
    return pl.pallas_call(
        functools.partial(_hgrn_kernel, layer=layer, n_chunks=n_chunks),
        grid=(batch, n_heads),
        in_specs=[part(0), part(n_heads), part(2 * n_heads), part(3 * n_heads),
                  pl.BlockSpec((n_layers, HEAD_DIM), lambda b, h: (0, h)),
                  pl.BlockSpec((1, HEAD_DIM), lambda b, h: (0, 0)),
                  whole(sums), whole(upper), whole(pair)],
        out_specs=pl.BlockSpec((seq, HEAD_DIM), lambda b, h: (b, h)),
        out_shape=jax.ShapeDtypeStruct((t, d), BF16),
        scratch_shapes=[pltpu.VMEM((seq, HEAD_DIM), BF16),
                        pltpu.VMEM((seq, HEAD_DIM), F32),
                        pltpu.VMEM((n_chunks, HEAD_DIM, HEAD_DIM), F32),
                        pltpu.VMEM((n_chunks // HGRN_GROUP * SUBLANES, HEAD_DIM), F32)],
        compiler_params=_params("parallel", "parallel"),
        name="hgrn_recurrence",
    )(proj, proj, proj, proj, lb_raw, g_norm.reshape(1, HEAD_DIM),
      jnp.asarray(sums, BF16), jnp.asarray(upper), jnp.asarray(pair))


def _router_kernel(x_ref, w_ref, b_ref, ids_ref, wts_ref, counts_ref):
    g, eg = N_GROUPS, EXPERTS_PER_GROUP
    tm = x_ref.shape[0]

    @pl.when(pl.program_id(0) == 0)
    def _():
        counts_ref[...] = jnp.zeros_like(counts_ref)

    x, w = x_ref[...], w_ref[...]
    x_hi, w_hi = x.astype(BF16), w.astype(BF16)
    x_lo, w_lo = (x - x_hi.astype(F32)).astype(BF16), (w - w_hi.astype(F32)).astype(BF16)
    logits = ((jnp.dot(x_hi, w_hi, preferred_element_type=F32) + jnp.dot(x_hi, w_lo, preferred_element_type=F32))
              + jnp.dot(x_lo, w_hi, preferred_element_type=F32)) + b_ref[...]
    lane = lax.broadcasted_iota(I32, logits.shape, 1)

    def softmax_over(mask):
        z = jnp.where(mask, logits, -jnp.inf)
        ez = jnp.exp(z - jnp.max(z, axis=1, keepdims=True))
        return ez / jnp.sum(ez, axis=1, keepdims=True)

    def top1(p, mask):
        best = jnp.max(jnp.where(mask, p, -1.0), axis=1, keepdims=True)
        where = jnp.min(jnp.where(mask & (p == best), lane, LANES), axis=1, keepdims=True)
        return best, where

    is_group = lane < g
    g_w, g_idx = top1(softmax_over(is_group), is_group)
    lo = g + g_idx * eg
    in_group = (lane >= lo) & (lane < lo + eg)
    pe = softmax_over(in_group)
    w1, i1 = top1(pe, in_group)
    w2, i2 = top1(pe, in_group & (lane != i1))
    den = w1 + w2
    wt1, wt2 = g_w * (w1 / den), g_w * (w2 / den)
    e1, e2 = i1 - g, i2 - g
    hot1, hot2 = lane == e1, lane == e2
    hot = jnp.where(hot1 | hot2, 1.0, 0.0)
    earlier = lax.broadcasted_iota(I32, (tm, tm), 1) < lax.broadcasted_iota(I32, (tm, tm), 0)
    before = counts_ref[...] + jnp.dot(jnp.where(earlier, 1.0, 0.0).astype(BF16), hot.astype(BF16),
                                       preferred_element_type=F32)
    r1 = jnp.sum(jnp.where(hot1, before, 0.0), axis=1, keepdims=True).astype(I32)
    r2 = jnp.sum(jnp.where(hot2, before, 0.0), axis=1, keepdims=True).astype(I32)
    counts_ref[...] += jnp.sum(hot, axis=0, keepdims=True)
    ids_ref[...] = jnp.where(lane == 0, e1, jnp.where(lane == 1, e2, jnp.where(lane == 2, r1,
                                                                               jnp.where(lane == 3, r2, 0))))
    wts_ref[...] = jnp.where(lane == 0, wt1, jnp.where(lane == 1, wt2, 0.0))


def _router(x, w_r, b_r, *, tm):
    t, d = x.shape
    tm = min(tm, t)
    assert t % tm == 0
    row = lambda i: (i, 0)
    fixed = lambda i: (0, 0)
    return pl.pallas_call(
        _router_kernel,
        grid=(t // tm,),
        in_specs=[pl.BlockSpec((tm, d), row), pl.BlockSpec((d, LANES), fixed), pl.BlockSpec((1, LANES), fixed)],
        out_specs=[pl.BlockSpec((tm, LANES), row), pl.BlockSpec((tm, LANES), row), pl.BlockSpec((1, LANES), fixed)],
        out_shape=[jax.ShapeDtypeStruct((t, LANES), I32), jax.ShapeDtypeStruct((t, LANES), F32),
                   jax.ShapeDtypeStruct((1, LANES), F32)],
        compiler_params=_params("arbitrary"),
        name="moe_router",
    )(x, w_r, b_r)


def _expert_kernel(blk_expert_ref, row_src_ref, n_used_ref, x_hbm, wg_ref, wu_ref, wd_ref, y_ref,
                   xbuf, wg_b, wu_b, wd_b, in_sems):
    i = pl.program_id(0)
    n_blk = pl.num_programs(0)
    d = wg_ref.shape[0]
    n = d // LANES
    rows = xbuf.shape[1] // n
    n_in = xbuf.shape[0]
    n_used = n_used_ref[0]
    active = i < n_used

    def gather(block):
        s = block % n_in
        block = jnp.minimum(block, n_blk - 1)
        for r in range(rows):
            tok = pl.multiple_of(row_src_ref[block * rows + r] * n, n)
            pltpu.make_async_copy(x_hbm.at[pl.ds(tok, n), :], xbuf.at[s, pl.ds(r * n, n), :], in_sems.at[s]).start()

    def wait_gather(block):
        s = block % n_in
        pltpu.make_async_copy(x_hbm.at[pl.ds(0, rows * n), :], xbuf.at[s], in_sems.at[s]).wait()

    def drain(first_unused_block):
        wait_gather(first_unused_block)
        wait_gather(first_unused_block + 1)

    @pl.when(i == 0)
    def _():
        gather(0)
        gather(1)

    @pl.when(active)
    def _():
        wait_gather(i)

        @pl.when((i == 0) | (blk_expert_ref[i] != blk_expert_ref[jnp.maximum(i - 1, 0)]))
        def _():
            wg_b[...] = wg_ref[...].astype(BF16)
            wu_b[...] = wu_ref[...].astype(BF16)
            wd_b[...] = wd_ref[...].astype(BF16)

        xb = _load_slabs(xbuf.at[i % n_in], rows, d).astype(BF16)
        hg = jnp.dot(xb, wg_b[...], preferred_element_type=F32)
        hu = jnp.dot(xb, wu_b[...], preferred_element_type=F32)
        h = (hg * _sigmoid(hg)) * hu
        _store_slabs(y_ref, jnp.dot(h.astype(BF16), wd_b[...], preferred_element_type=F32))
        gather(i + 2)

        @pl.when(i == n_blk - 1)
        def _():
            drain(i + 1)

    @pl.when(i == n_used)
    def _():
        drain(i)

    @pl.when(jnp.logical_not(active))
    def _():
        y_ref[...] = jnp.zeros_like(y_ref)


def _expert_mlp(x_slabs, blk_expert, row_src, n_used, w_gate, w_up, w_down, layer):
    _, n_e, d, f = w_gate.shape
    n_blk = blk_expert.shape[0]
    rows = MOE_ROWS
    n = d // LANES
    by_expert = lambda i, be, rs, nu: (layer, be[i], 0, 0)
    grid_spec = pltpu.PrefetchScalarGridSpec(
        num_scalar_prefetch=3,
        grid=(n_blk,),
        in_specs=[pl.BlockSpec(memory_space=pl.ANY),
                  pl.BlockSpec((None, None, d, f), by_expert), pl.BlockSpec((None, None, d, f), by_expert),
                  pl.BlockSpec((None, None, f, d), by_expert)],
        out_specs=pl.BlockSpec((rows * n, LANES), lambda i, be, rs, nu: (i, 0)),
        scratch_shapes=[pltpu.VMEM((3, rows * n, LANES), F32),
                        pltpu.VMEM((d, f), BF16), pltpu.VMEM((d, f), BF16), pltpu.VMEM((f, d), BF16),
                        pltpu.SemaphoreType.DMA((3,))],
    )
    return pl.pallas_call(
        _expert_kernel,
        grid_spec=grid_spec,
        out_shape=jax.ShapeDtypeStruct((n_blk * rows * n, LANES), F32),
        compiler_params=_params("arbitrary"),
        name="moe_experts",
    )(blk_expert, row_src, n_used, x_slabs, w_gate, w_up, w_down)


def _combine_ln_kernel(pos_ref, y_hbm, x_ref, wts_ref, g_ref, b_ref, xo_ref, xb_ref, ybuf, sems, *, alpha):
    i = pl.program_id(0)
    n_steps = pl.num_programs(0)
    tm, d = x_ref.shape
    n = d // LANES
    n_buf = ybuf.shape[0]

    def gather(step):
        s = step % n_buf
        step = jnp.minimum(step, n_steps - 1)
        for r in range(tm):
            for k in range(TOP_E):
                p = pl.multiple_of(pos_ref[(step * tm + r) * TOP_E + k] * n, n)
                pltpu.make_async_copy(y_hbm.at[pl.ds(p, n), :], ybuf.at[s, k, pl.ds(r * n, n), :], sems.at[s]).start()

    def wait_gather(step):
        s = step % n_buf
        for k in range(TOP_E):
            pltpu.make_async_copy(y_hbm.at[pl.ds(0, tm * n), :], ybuf.at[s, k], sems.at[s]).wait()

    @pl.when(i == 0)
    def _():
        gather(0)
        gather(1)

    wait_gather(i)
    wts = wts_ref[...]
    slot = i % n_buf
    m = (wts[:, 0:1] * _load_slabs(ybuf.at[slot, 0], tm, d)) + (wts[:, 1:2] * _load_slabs(ybuf.at[slot, 1], tm, d))
    out = _layer_norm(alpha * x_ref[...] + m, g_ref[...], b_ref[...])
    xo_ref[...] = out
    xb_ref[...] = out.astype(BF16)
    gather(i + 2)

    @pl.when(i == n_steps - 1)
    def _():
        wait_gather(i + 1)
        wait_gather(i + 2)


def _combine_ln(pos, y_slabs, x, wts, g, b, *, alpha, tm):
    t, d = x.shape
    tm = min(tm, t)
    n = d // LANES
    assert t % tm == 0
    row = lambda i, p: (i, 0)
    fixed = lambda i, p: (0, 0)
    grid_spec = pltpu.PrefetchScalarGridSpec(
        num_scalar_prefetch=1,
        grid=(t // tm,),
        in_specs=[pl.BlockSpec(memory_space=pl.ANY), pl.BlockSpec((tm, d), row), pl.BlockSpec((tm, LANES), row),
                  pl.BlockSpec((1, d), fixed), pl.BlockSpec((1, d), fixed)],
        out_specs=[pl.BlockSpec((tm, d), row), pl.BlockSpec((tm, d), row)],
        scratch_shapes=[pltpu.VMEM((3, TOP_E, tm * n, LANES), F32), pltpu.SemaphoreType.DMA((3,))],
    )
    return pl.pallas_call(
        functools.partial(_combine_ln_kernel, alpha=alpha),
        grid_spec=grid_spec,
        out_shape=[jax.ShapeDtypeStruct((t, d), F32), jax.ShapeDtypeStruct((t, d), BF16)],
        compiler_params=_params("arbitrary"),
        name="moe_combine_ln",
    )(pos, y_slabs, x, wts, g.reshape(1, d), b.reshape(1, d))


def _dispatch_plan(ids, counts, n_tokens):
    rows = MOE_ROWS
    n_assign = n_tokens * TOP_E
    eid = ids[:, 0:TOP_E].reshape(n_assign)
    rank = ids[:, TOP_E:2 * TOP_E].reshape(n_assign)
    counts = counts[0, :N_EXPERTS].astype(I32)
    padded = (counts + rows - 1) // rows * rows
    pends = jnp.cumsum(padded)
    pstarts = pends - padded
    pos = jnp.take(pstarts, eid) + rank
    n_blk = n_assign // rows + N_EXPERTS
    row_src = jnp.zeros((n_blk * rows,), I32).at[pos].set(jnp.arange(n_assign, dtype=I32) // TOP_E)
    blk_first = jnp.arange(n_blk, dtype=I32) * rows
    blk_expert = jnp.minimum(jnp.sum((pends[None, :] <= blk_first[:, None]).astype(I32), axis=1), N_EXPERTS - 1)
    n_used = (pends[-1] // rows).astype(I32).reshape(1)
    return pos, row_src, blk_expert, n_used


def _moe_ln(x, x_slabs, w_r, b_r, w_gate, w_up, w_down, layer, g, b, *, alpha):
    ids, wts, counts = _router(x, w_r, b_r, tm=256)
    pos, row_src, blk_expert, n_used = _dispatch_plan(ids, counts, x.shape[0])
    y_slabs = _expert_mlp(x_slabs, blk_expert, row_src, n_used, w_gate, w_up, w_down, layer)
    return _combine_ln(pos, y_slabs, x, wts, g, b, alpha=alpha, tm=256)


def kernel(x, moba_w_in, moba_w_o, hgrn_w_in, hgrn_g_norm, hgrn_lb_raw, hgrn_w_o, ln_mix_g, ln_mix_b,
           moe_w_rg, moe_b_rg, moe_w_re, moe_b_re, moe_w_gate, moe_w_up, moe_w_down, ln_ffn_g, ln_ffn_b):
    batch, seq, d = x.shape
    depth = ln_mix_g.shape[0]
    alpha = (2 * depth) ** 0.25
    t = batch * seq
    xf = x.reshape(t, d)
    xb = xf.astype(BF16)
    n_re = N_GROUPS * EXPERTS_PER_GROUP
    moba_w_o_b, hgrn_w_o_b = moba_w_o.astype(BF16), hgrn_w_o.astype(BF16)
    for layer in range(depth):
        j = layer // 2
        if layer % 2 == 0:
            qkv = _matmul(xb, moba_w_in, j, BF16, tm=2048, tn=512, scaled_cols=d, scale=HEAD_DIM ** -0.5 * LOG2_E)
            o = _moba_attention(qkv, batch, seq)
            w_o = moba_w_o_b
        else:
            proj = _matmul(xb, hgrn_w_in, j, F32, tm=2048, tn=512)
            o = _hgrn_mix(proj, hgrn_lb_raw, hgrn_g_norm[j], layer=j, batch=batch, seq=seq)
            w_o = hgrn_w_o_b
        w_re = jnp.transpose(moe_w_re[layer], (1, 0, 2)).reshape(d, n_re)
        w_r = jnp.concatenate([moe_w_rg[layer], w_re, jnp.zeros((d, LANES - N_GROUPS - n_re), F32)], axis=1)
        b_r = jnp.concatenate([moe_b_rg[layer], moe_b_re[layer].reshape(n_re),
                               jnp.zeros((LANES - N_GROUPS - n_re,), F32)]).reshape(1, LANES)
        xf, xs = _proj_ln(o, w_o, j, xf, ln_mix_g[layer], ln_mix_b[layer], alpha=alpha, tm=256)
        xf, xb = _moe_ln(xf, xs, w_r, b_r, moe_w_gate, moe_w_up, moe_w_down, layer,
                         ln_ffn_g[layer], ln_ffn_b[layer], alpha=alpha)
    return xf.reshape(batch, seq, d)
```

```python
import functools

import numpy as np
import jax
import jax.numpy as jnp
from jax import lax
from jax.experimental import pallas as pl
from jax.experimental.pallas import tpu as pltpu

F32 = jnp.float32
BF16 = jnp.bfloat16
I32 = jnp.int32

LANES = 128
SUBLANES = 8
VMEM_LIMIT_BYTES = 56 << 20

HEAD_DIM = 128
MOBA_BLOCK = 256
MOBA_TOPK = 3
HGRN_CHUNK = 64
N_GROUPS = 4
EXPERTS_PER_GROUP = 8
N_EXPERTS = N_GROUPS * EXPERTS_PER_GROUP
TOP_E = 2
LN_EPS = 1e-5
RMS_EPS = 1e-6

MOE_ROWS = 256
COMBINE_ROWS = 64
NT_DIMS = (((1,), (1,)), ((), ()))
TN_DIMS = (((0,), (0,)), ((), ()))


def _params(*semantics):
    return pltpu.CompilerParams(dimension_semantics=semantics, vmem_limit_bytes=VMEM_LIMIT_BYTES)


def _sigmoid(x):
    return 1.0 / (1.0 + jnp.exp(-x))


def _layer_norm(y, g, b):
    mu = jnp.mean(y, axis=-1, keepdims=True)
    d = y - mu
    var = jnp.mean(d * d, axis=-1, keepdims=True)
    return d * lax.rsqrt(var + LN_EPS) * g + b


def _matmul_kernel(x_ref, w_ref, o_ref, *, n_scaled_blocks, scale):
    acc = jnp.dot(x_ref[...], w_ref[...].astype(BF16), preferred_element_type=F32)
    if n_scaled_blocks:
        acc = acc * jnp.where(pl.program_id(1) < n_scaled_blocks, scale, 1.0).astype(F32)
    o_ref[...] = acc.astype(o_ref.dtype)


def _matmul(x, w_stack, layer, out_dtype, *, tm, tn, scaled_cols=0, scale=1.0):
    m, k = x.shape
    n = w_stack.shape[2]
    tm, tn = min(tm, m), min(tn, n)
    assert m % tm == 0 and n % tn == 0 and scaled_cols % tn == 0
    kern = functools.partial(_matmul_kernel, n_scaled_blocks=scaled_cols // tn, scale=scale)
    return pl.pallas_call(
        kern,
        grid=(m // tm, n // tn),
        in_specs=[pl.BlockSpec((tm, k), lambda i, j: (i, 0)),
                  pl.BlockSpec((None, k, tn), lambda i, j: (layer, 0, j))],
        out_specs=pl.BlockSpec((tm, tn), lambda i, j: (i, j)),
        out_shape=jax.ShapeDtypeStruct((m, n), out_dtype),
        compiler_params=_params("parallel", "parallel"),
        name="proj_matmul",
    )(x, w_stack)


def _store_slabs(slab_ref, x):
    rows, d = x.shape
    n = d // LANES
    for c in range(n):
        slab_ref[pl.ds(c, rows, stride=n), :] = x[:, c * LANES:(c + 1) * LANES]


def _load_slabs(slab_ref, rows, d):
    n = d // LANES
    return jnp.concatenate([slab_ref[pl.ds(c, rows, stride=n), :] for c in range(n)], axis=1)


def _proj_ln_kernel(o_ref, w_ref, x_ref, g_ref, b_ref, xo_ref, xs_ref, *, alpha):
    h = jnp.dot(o_ref[...], w_ref[...], preferred_element_type=F32)
    out = _layer_norm(alpha * x_ref[...] + h, g_ref[...], b_ref[...])
    xo_ref[...] = out
    _store_slabs(xs_ref, out)


def _proj_ln(o, w_stack, layer, x, g, b, *, alpha, tm):
    t, d = x.shape
    tm = min(tm, t)
    n = d // LANES
    assert t % tm == 0
    row = lambda i: (i, 0)
    fixed = lambda i: (0, 0)
    return pl.pallas_call(
        functools.partial(_proj_ln_kernel, alpha=alpha),
        grid=(t // tm,),
        in_specs=[pl.BlockSpec((tm, d), row), pl.BlockSpec((None, d, d), lambda i: (layer, 0, 0)),
                  pl.BlockSpec((tm, d), row), pl.BlockSpec((1, d), fixed), pl.BlockSpec((1, d), fixed)],
        out_specs=[pl.BlockSpec((tm, d), row), pl.BlockSpec((tm * n, LANES), row)],
        out_shape=[jax.ShapeDtypeStruct((t, d), F32), jax.ShapeDtypeStruct((t * n, LANES), F32)],
        compiler_params=_params("parallel"),
        name="proj_ln",
    )(o, w_stack, x, g.reshape(1, d), b.reshape(1, d))


MOBA_HEADS_PER_STEP = 4
MOBA_TILE = 2 * MOBA_BLOCK
MOBA_SUM_ROWS = 2 * SUBLANES
LOG2_E = 1.4426950408889634


def _moba_kernel(q_ref, k_ref, v_ref, o_ref, kmean_ref, vt_ref, sel_ref, *, n_blocks):
    blk, tile, hd = MOBA_BLOCK, MOBA_TILE, HEAD_DIM
    heads = range(MOBA_HEADS_PER_STEP)
    i = pl.program_id(2)

    @pl.when(i == 0)
    def _():
        kmean_ref[...] = jnp.zeros_like(kmean_ref)
        for h in heads:
            cols = slice(h * hd, (h + 1) * hd)
            for j in range(n_blocks):
                kj = k_ref[pl.ds(j * blk, blk), cols].astype(F32)
                kmean_ref[h, pl.ds(j, 1), :] = jnp.mean(kj, axis=0, keepdims=True)
            for p in range(n_blocks // 2):
                vt_ref[h, p, 0:hd, :] = v_ref[pl.ds(p * tile, tile), cols].T
                vt_ref[h, p, hd:, :] = jnp.ones((MOBA_SUM_ROWS, tile), BF16)

    q = [q_ref[:, h * hd:(h + 1) * hd] for h in heads]
    for h in heads:
        gate = lax.dot_general(kmean_ref[h], q[h].astype(F32), NT_DIMS,
                               precision=lax.Precision.HIGHEST, preferred_element_type=F32)
        blk_idx = lax.broadcasted_iota(I32, gate.shape, 0)
        gate = jnp.where(blk_idx < i, gate, -jnp.inf)
        rank = jnp.zeros(gate.shape, I32)
        for jp in range(n_blocks):
            row = gate[jp:jp + 1, :]
            beats = (row > gate) | ((row == gate) & (jp < blk_idx))
            rank = rank + jnp.where(beats, 1, 0)
        sel_ref[h] = jnp.where((gate > -jnp.inf) & (rank < MOBA_TOPK), 1.0, 0.0)

    def scores(h, p):
        kt = k_ref[pl.ds(pl.multiple_of(p * tile, tile), tile), h * hd:(h + 1) * hd]
        return lax.dot_general(kt, q[h], NT_DIMS, preferred_element_type=F32)

    def masked(s, keep_top, keep_bot):
        return jnp.concatenate([jnp.where(keep_top, s[:blk], -jnp.inf),
                                jnp.where(keep_bot, s[blk:], -jnp.inf)], axis=0)

    p_own = i // 2
    causal = lax.broadcasted_iota(I32, (blk, blk), 0) <= lax.broadcasted_iota(I32, (blk, blk), 1)
    i_odd = jnp.full((blk, blk), i % 2, I32) == 1
    s_all = [scores(h, p_own) for h in heads]
    m_all, p_all = [], []
    for h in heads:
        chosen_prev = sel_ref[h, pl.ds(2 * p_own, 1), :] > 0.0
        s = masked(s_all[h], (i_odd & chosen_prev) | (~i_odd & causal), i_odd & causal)
        m = jnp.max(s, axis=0, keepdims=True)
        m_all.append(m)
        p_all.append(jnp.exp2(s - m).astype(BF16))
    state = []
    for h in heads:
        state += [m_all[h], jnp.dot(vt_ref[h, p_own], p_all[h], preferred_element_type=F32)]

    def body(pt, carry):
        s_all = [scores(h, pt) for h in heads]
        m_all, p_all, a_all = [], [], []
        for h in heads:
            m = carry[2 * h]
            s = masked(s_all[h], sel_ref[h, pl.ds(2 * pt, 1), :] > 0.0,
                       sel_ref[h, pl.ds(2 * pt + 1, 1), :] > 0.0)
            m_new = jnp.maximum(m, jnp.max(s, axis=0, keepdims=True))
            a_all.append(jnp.exp2(m - m_new))
            p_all.append(jnp.exp2(s - m_new).astype(BF16))
            m_all.append(m_new)
        out = []
        for h in heads:
            out += [m_all[h], a_all[h] * carry[2 * h + 1]
                    + jnp.dot(vt_ref[h, pt], p_all[h], preferred_element_type=F32)]
        return tuple(out)

    state = lax.fori_loop(0, p_own, body, tuple(state))
    for h in heads:
        acc = state[2 * h + 1]
        o_ref[:, h * hd:(h + 1) * hd] = (acc[0:hd, :] / acc[hd:hd + 1, :]).T.astype(o_ref.dtype)


def _moba_attention(qkv, batch, seq):
    t, d3 = qkv.shape
    d = d3 // 3
    hs = MOBA_HEADS_PER_STEP
    n_head_groups = d // (HEAD_DIM * hs)
    n_blocks = seq // MOBA_BLOCK
    assert seq % MOBA_TILE == 0 and d % (HEAD_DIM * hs) == 0
    nb_pad = -(-n_blocks // SUBLANES) * SUBLANES
    kv_spec = lambda off: pl.BlockSpec((seq, HEAD_DIM * hs), lambda b, h, i: (b, off + h))
    q_spec = pl.BlockSpec((MOBA_BLOCK, HEAD_DIM * hs), lambda b, h, i: (b * n_blocks + i, h))
    return pl.pallas_call(
        functools.partial(_moba_kernel, n_blocks=n_blocks),
        grid=(batch, n_head_groups, n_blocks),
        in_specs=[q_spec, kv_spec(n_head_groups), kv_spec(2 * n_head_groups)],
        out_specs=q_spec,
        out_shape=jax.ShapeDtypeStruct((t, d), BF16),
        scratch_shapes=[pltpu.VMEM((hs, nb_pad, HEAD_DIM), F32),
                        pltpu.VMEM((hs, n_blocks // 2, HEAD_DIM + MOBA_SUM_ROWS, MOBA_TILE), BF16),
                        pltpu.VMEM((hs, nb_pad, MOBA_BLOCK), F32)],
        compiler_params=_params("parallel", "parallel", "arbitrary"),
        name="moba_attention",
    )(qkv, qkv, qkv)


_HGRN_LEVELS = (32, 16, 8, 4, 2, 1)
HGRN_GROUP = 8


def _hgrn_constants():
    c = HGRN_CHUNK
    idx = np.arange(c)
    t, s = idx[:, None], idx[None, :]
    sums = [(s <= t)]
    upper, pair = [], []
    for m in _HGRN_LEVELS:
        same = (t // (2 * m)) == (s // (2 * m))
        t_up, s_up = (t % (2 * m)) >= m, (s % (2 * m)) >= m
        up_rows = same & t_up & s_up & (s <= t)
        lo_rows = same & ~t_up & ~s_up & (s > t)
        if m > 1:
            sums.append(up_rows | lo_rows)
        upper.append(np.broadcast_to(t_up, (c, LANES)))
        pair.append(same & t_up & ~s_up)
    return (np.concatenate(sums, 0).astype(np.float32),
            np.stack(upper).astype(np.float32), np.stack(pair).astype(np.float32))


def _hgrn_kernel(q_ref, f_ref, i_ref, g_ref, lbraw_ref, gn_ref, sums_ref, upper_ref, pair_ref, o_ref,
                 qg_ref, oin_ref, u_ref, dl_ref, *, layer, n_chunks):
    c = HGRN_CHUNK
    dk = HEAD_DIM
    raw = lbraw_ref[...]
    e = jnp.exp(raw - jnp.max(raw, axis=0, keepdims=True))
    sm = e / jnp.sum(e, axis=0, keepdims=True)
    cum = sm[0:1, :]
    for l in range(1, layer + 1):
        cum = cum + sm[l:l + 1, :]
    lb = cum - sm[0:1, :]
    gnorm = gn_ref[...]
    eye = (lax.broadcasted_iota(I32, (c, c), 0) == lax.broadcasted_iota(I32, (c, c), 1))

    group = range(HGRN_GROUP)
    chunk = lambda x, n: x[n * c:(n + 1) * c]
    n_levels = len(_HGRN_LEVELS)

    def local(gi, carry):
        rows = pl.ds(pl.multiple_of(gi * (HGRN_GROUP * c), HGRN_GROUP * c), HGRN_GROUP * c)
        qv, fv, iv = q_ref[rows, :], f_ref[rows, :], i_ref[rows, :]
        qs = qv * _sigmoid(qv)
        fg = lb + (1.0 - lb) * _sigmoid(fv)
        log_f = jnp.log(fg)
        kk = 1.0 - fg
        iv_b = iv.astype(BF16)
        hi = log_f.astype(BF16)
        rest = log_f - hi.astype(F32)
        mid = rest.astype(BF16)
        lo = (rest - mid.astype(F32)).astype(BF16)
        parts3 = jnp.concatenate([chunk(t, n) for n in group for t in (hi, mid, lo)], axis=1)
        parts2 = jnp.concatenate([chunk(t, n) for n in group for t in (hi, mid)], axis=1)
        s3 = jnp.dot(sums_ref[0:c, :], parts3, preferred_element_type=F32)
        s2 = jnp.dot(sums_ref[c:, :], parts2, preferred_element_type=F32)
        gcum = jnp.concatenate([(s3[:, (3 * n) * dk:(3 * n + 1) * dk] + s3[:, (3 * n + 1) * dk:(3 * n + 2) * dk])
                                + s3[:, (3 * n + 2) * dk:(3 * n + 3) * dk] for n in group], axis=0)
        level_sums = [s2[:, (2 * n) * dk:(2 * n + 1) * dk] + s2[:, (2 * n + 1) * dk:(2 * n + 2) * dk] for n in group]
        qg_ref[rows, :] = (qs * jnp.exp(gcum)).astype(BF16)
        qe, ke = [], []
        for n in group:
            for li in range(n_levels):
                is_up = upper_ref[li] > 0.0
                if _HGRN_LEVELS[li] == 1:
                    dec = jnp.exp(jnp.where(is_up, chunk(log_f, n), 0.0))
                else:
                    dec = jnp.exp(level_sums[n][li * c:(li + 1) * c, :])
                qe.append(jnp.where(is_up, chunk(qs, n) * dec, 0.0).astype(BF16))
                ke.append(jnp.where(is_up, 0.0, chunk(kk, n) * dec).astype(BF16))
        prods = [lax.dot_general(a, b, NT_DIMS, preferred_element_type=F32) for a, b in zip(qe, ke)]
        diag = jnp.sum(qs * kk, axis=1, keepdims=True)
        a_all = []
        for n in group:
            a = jnp.where(eye, chunk(diag, n), 0.0)
            for li in range(n_levels):
                a = a + jnp.where(pair_ref[li] > 0.0, prods[n * n_levels + li], 0.0)
            a_all.append(a.astype(BF16))
        o_in = [jnp.dot(a_all[n], chunk(iv_b, n), preferred_element_type=F32) for n in group]
        oin_ref[rows, :] = jnp.concatenate(o_in, axis=0)
        g_last = [gcum[(n + 1) * c - 1:(n + 1) * c, :] for n in group]
        k_dec = [(chunk(kk, n) * jnp.exp(g_last[n] - chunk(gcum, n))).astype(BF16) for n in group]
        for n in group:
            u_ref[gi * HGRN_GROUP + n] = lax.dot_general(chunk(iv_b, n), k_dec[n], TN_DIMS,
                                                         preferred_element_type=F32)
        dl_ref[pl.ds(pl.multiple_of(gi * SUBLANES, SUBLANES), SUBLANES), :] = jnp.concatenate(
            [jnp.exp(g) for g in g_last] + [jnp.zeros((1, dk), F32)] * (SUBLANES - HGRN_GROUP), axis=0)
        return carry

    lax.fori_loop(0, n_chunks // HGRN_GROUP, local, 0)

    def recur(gi, state_t):
        rows = pl.ds(pl.multiple_of(gi * (HGRN_GROUP * c), HGRN_GROUP * c), HGRN_GROUP * c)
        decay = dl_ref[pl.ds(pl.multiple_of(gi * SUBLANES, SUBLANES), SUBLANES), :]
        states = []
        for n in group:
            states.append(state_t.astype(BF16))
            state_t = decay[n:n + 1, :] * state_t + u_ref[gi * HGRN_GROUP + n]
        qg = qg_ref[rows, :]
        o = oin_ref[rows, :] + jnp.concatenate(
            [lax.dot_general(chunk(qg, n), states[n], NT_DIMS, preferred_element_type=F32) for n in group], axis=0)
        gv = g_ref[rows, :]
        o = o * lax.rsqrt(jnp.mean(o * o, axis=1, keepdims=True) + RMS_EPS) * gnorm
        o_ref[rows, :] = (o * (gv * _sigmoid(gv))).astype(o_ref.dtype)
        return state_t

    lax.fori_loop(0, n_chunks // HGRN_GROUP, recur, jnp.zeros((HEAD_DIM, HEAD_DIM), F32))


def _hgrn_mix(proj, lb_raw, g_norm, *, layer, batch, seq):
    t, d4 = proj.shape
    d = d4 // 4
    n_heads = d // HEAD_DIM
    n_chunks = seq // HGRN_CHUNK
    assert seq % (HGRN_CHUNK * HGRN_GROUP) == 0 and HGRN_GROUP <= SUBLANES
    sums, upper, pair = _hgrn_constants()
    n_layers = lb_raw.shape[0]
    part = lambda off: pl.BlockSpec((seq, HEAD_DIM), lambda b, h: (b, off + h))
    whole = lambda a: pl.BlockSpec(a.shape, lambda b, h: (0,) * a.ndim)
    return pl.pallas_call(
        functools.partial(_hgrn_kernel, layer=layer, n_chunks=n_chunks),
        grid=(batch, n_heads),
        in_specs=[part(0), part(n_heads), part(2 * n_heads), part(3 * n_heads),
                  pl.BlockSpec((n_layers, HEAD_DIM), lambda b, h: (0, h)),
                  pl.BlockSpec((1, HEAD_DIM), lambda b, h: (0, 0)),
                  whole(sums), whole(upper), whole(pair)],
        out_specs=pl.BlockSpec((seq, HEAD_DIM), lambda b, h: (b, h)),
        out_shape=jax.ShapeDtypeStruct((t, d), BF16),
        scratch_shapes=[pltpu.VMEM((seq, HEAD_DIM), BF16),
                        pltpu.VMEM((seq, HEAD_DIM), F32),
                        pltpu.VMEM((n_chunks, HEAD_DIM, HEAD_DIM), F32),
                        pltpu.VMEM((n_chunks // HGRN_GROUP * SUBLANES, HEAD_DIM), F32)],
        compiler_params=_params("parallel", "parallel"),
        name="hgrn_recurrence",
    )(proj, proj, proj, proj, lb_raw, g_norm.reshape(1, HEAD_DIM),
      jnp.asarray(sums, BF16), jnp.asarray(upper), jnp.asarray(pair))


def _router_kernel(x_ref, w_ref, b_ref, ids_ref, wts_ref, counts_ref):
    g, eg = N_GROUPS, EXPERTS_PER_GROUP
    tm = x_ref.shape[0]

    @pl.when(pl.program_id(0) == 0)
    def _():
        counts_ref[...] = jnp.zeros_like(counts_ref)

    x, w = x_ref[...], w_ref[...]
    x_hi, w_hi = x.astype(BF16), w.astype(BF16)
    x_lo, w_lo = (x - x_hi.astype(F32)).astype(BF16), (w - w_hi.astype(F32)).astype(BF16)
    logits = ((jnp.dot(x_hi, w_hi, preferred_element_type=F32) + jnp.dot(x_hi, w_lo, preferred_element_type=F32))
              + jnp.dot(x_lo, w_hi, preferred_element_type=F32)) + b_ref[...]
    lane = lax.broadcasted_iota(I32, logits.shape, 1)

    def softmax_over(mask):
        z = jnp.where(mask, logits, -jnp.inf)
        ez = jnp.exp(z - jnp.max(z, axis=1, keepdims=True))
        return ez / jnp.sum(ez, axis=1, keepdims=True)

    def top1(p, mask):
        best = jnp.max(jnp.where(mask, p, -1.0), axis=1, keepdims=True)
        where = jnp.min(jnp.where(mask & (p == best), lane, LANES), axis=1, keepdims=True)
        return best, where

    is_group = lane < g
    g_w, g_idx = top1(softmax_over(is_group), is_group)
    lo = g + g_idx * eg
    in_group = (lane >= lo) & (lane < lo + eg)
    pe = softmax_over(in_group)
    w1, i1 = top1(pe, in_group)
    w2, i2 = top1(pe, in_group & (lane != i1))
    den = w1 + w2
    wt1, wt2 = g_w * (w1 / den), g_w * (w2 / den)
    e1, e2 = i1 - g, i2 - g
    hot1, hot2 = lane == e1, lane == e2
    hot = jnp.where(hot1 | hot2, 1.0, 0.0)
    earlier = lax.broadcasted_iota(I32, (tm, tm), 1) < lax.broadcasted_iota(I32, (tm, tm), 0)
    before = counts_ref[...] + jnp.dot(jnp.where(earlier, 1.0, 0.0).astype(BF16), hot.astype(BF16),
                                       preferred_element_type=F32)
    r1 = jnp.sum(jnp.where(hot1, before, 0.0), axis=1, keepdims=True).astype(I32)
    r2 = jnp.sum(jnp.where(hot2, before, 0.0), axis=1, keepdims=True).astype(I32)
    counts_ref[...] += jnp.sum(hot, axis=0, keepdims=True)
    ids_ref[...] = jnp.where(lane == 0, e1, jnp.where(lane == 1, e2, jnp.where(lane == 2, r1,
                                                                               jnp.where(lane == 3, r2, 0))))
    wts_ref[...] = jnp.where(lane == 0, wt1, jnp.where(lane == 1, wt2, 0.0))


def _router(x, w_r, b_r, *, tm):
    t, d = x.shape
    tm = min(tm, t)
    assert t % tm == 0
    row = lambda i: (i, 0)
    fixed = lambda i: (0, 0)
    return pl.pallas_call(
        _router_kernel,
        grid=(t // tm,),
        in_specs=[pl.BlockSpec((tm, d), row), pl.BlockSpec((d, LANES), fixed), pl.BlockSpec((1, LANES), fixed)],
        out_specs=[pl.BlockSpec((tm, LANES), row), pl.BlockSpec((tm, LANES), row), pl.BlockSpec((1, LANES), fixed)],
        out_shape=[jax.ShapeDtypeStruct((t, LANES), I32), jax.ShapeDtypeStruct((t, LANES), F32),
                   jax.ShapeDtypeStruct((1, LANES), F32)],
        compiler_params=_params("arbitrary"),
        name="moe_router",
    )(x, w_r, b_r)


def _expert_kernel(blk_expert_ref, row_src_ref, n_used_ref, x_hbm, wg_ref, wu_ref, wd_ref, y_ref,
                   xbuf, wg_b, wu_b, wd_b, in_sems):
    i = pl.program_id(0)
    n_blk = pl.num_programs(0)
    d = wg_ref.shape[0]
    n = d // LANES
    rows = xbuf.shape[1] // n
    n_in = xbuf.shape[0]
    n_used = n_used_ref[0]
    active = i < n_used

    def gather(block):
        s = block % n_in
        block = jnp.minimum(block, n_blk - 1)
        for r in range(rows):
            tok = pl.multiple_of(row_src_ref[block * rows + r] * n, n)
            pltpu.make_async_copy(x_hbm.at[pl.ds(tok, n), :], xbuf.at[s, pl.ds(r * n, n), :], in_sems.at[s]).start()

    def wait_gather(block):
        s = block % n_in
        pltpu.make_async_copy(x_hbm.at[pl.ds(0, rows * n), :], xbuf.at[s], in_sems.at[s]).wait()

    def drain(first_unused_block):
        wait_gather(first_unused_block)
        wait_gather(first_unused_block + 1)

    @pl.when(i == 0)
    def _():
        gather(0)
        gather(1)

    @pl.when(active)
    def _():
        wait_gather(i)

        @pl.when((i == 0) | (blk_expert_ref[i] != blk_expert_ref[jnp.maximum(i - 1, 0)]))
        def _():
            wg_b[...] = wg_ref[...].astype(BF16)
            wu_b[...] = wu_ref[...].astype(BF16)
            wd_b[...] = wd_ref[...].astype(BF16)

        xb = _load_slabs(xbuf.at[i % n_in], rows, d).astype(BF16)
        hg = jnp.dot(xb, wg_b[...], preferred_element_type=F32)
        hu = jnp.dot(xb, wu_b[...], preferred_element_type=F32)
        h = (hg * _sigmoid(hg)) * hu
        _store_slabs(y_ref, jnp.dot(h.astype(BF16), wd_b[...], preferred_element_type=F32))
        gather(i + 2)

        @pl.when(i == n_blk - 1)
        def _():
            drain(i + 1)

    @pl.when(i == n_used)
    def _():
        drain(i)

    @pl.when(jnp.logical_not(active))
    def _():
        y_ref[...] = jnp.zeros_like(y_ref)


def _expert_mlp(x_slabs, blk_expert, row_src, n_used, w_gate, w_up, w_down, layer):
    _, n_e, d, f = w_gate.shape
    n_blk = blk_expert.shape[0]
    rows = MOE_ROWS
    n = d // LANES
    by_expert = lambda i, be, rs, nu: (layer, be[i], 0, 0)
    grid_spec = pltpu.PrefetchScalarGridSpec(
        num_scalar_prefetch=3,
        grid=(n_blk,),
        in_specs=[pl.BlockSpec(memory_space=pl.ANY),
                  pl.BlockSpec((None, None, d, f), by_expert), pl.BlockSpec((None, None, d, f), by_expert),
                  pl.BlockSpec((None, None, f, d), by_expert)],
        out_specs=pl.BlockSpec((rows * n, LANES), lambda i, be, rs, nu: (i, 0)),
        scratch_shapes=[pltpu.VMEM((3, rows * n, LANES), F32),
                        pltpu.VMEM((d, f), BF16), pltpu.VMEM((d, f), BF16), pltpu.VMEM((f, d), BF16),
                        pltpu.SemaphoreType.DMA((3,))],
    )
    return pl.pallas_call(
        _expert_kernel,
        grid_spec=grid_spec,
        out_shape=jax.ShapeDtypeStruct((n_blk * rows * n, LANES), F32),
        compiler_params=_params("arbitrary"),
        name="moe_experts",
    )(blk_expert, row_src, n_used, x_slabs, w_gate, w_up, w_down)


def _combine_ln_kernel(pos_ref, y_hbm, x_ref, wts_ref, g_ref, b_ref, xo_ref, xb_ref, ybuf, sems, *, alpha):
    i = pl.program_id(0)
    n_steps = pl.num_programs(0)
    tm, d = x_ref.shape
    n = d // LANES
    n_buf = ybuf.shape[0]

    def gather(step):
        s = step % n_buf
        step = jnp.minimum(step, n_steps - 1)
        for r in range(tm):
            for k in range(TOP_E):
                p = pl.multiple_of(pos_ref[(step * tm + r) * TOP_E + k] * n, n)
                pltpu.make_async_copy(y_hbm.at[pl.ds(p, n), :], ybuf.at[s, k, pl.ds(r * n, n), :], sems.at[s]).start()

    def wait_gather(step):
        s = step % n_buf
        for k in range(TOP_E):
            pltpu.make_async_copy(y_hbm.at[pl.ds(0, tm * n), :], ybuf.at[s, k], sems.at[s]).wait()

    @pl.when(i == 0)
    def _():
        gather(0)
        gather(1)

    wait_gather(i)
    slot = i % n_buf

    def rows_chunk(c, carry):
        rows = pl.ds(pl.multiple_of(c * COMBINE_ROWS, COMBINE_ROWS), COMBINE_ROWS)
        slabs = pl.ds(pl.multiple_of(c * (COMBINE_ROWS * n), COMBINE_ROWS * n), COMBINE_ROWS * n)
        wts = wts_ref[rows, :]
        m = (wts[:, 0:1] * _load_slabs(ybuf.at[slot, 0, slabs], COMBINE_ROWS, d)
             + wts[:, 1:2] * _load_slabs(ybuf.at[slot, 1, slabs], COMBINE_ROWS, d))
        out = _layer_norm(alpha * x_ref[rows, :] + m, g_ref[...], b_ref[...])
        xo_ref[rows, :] = out
        xb_ref[rows, :] = out.astype(BF16)
        return carry

    lax.fori_loop(0, tm // COMBINE_ROWS, rows_chunk, 0)
    gather(i + 2)

    @pl.when(i == n_steps - 1)
    def _():
        wait_gather(i + 1)
        wait_gather(i + 2)


def _combine_ln(pos, y_slabs, x, wts, g, b, *, alpha, tm):
    t, d = x.shape
    tm = min(tm, t)
    n = d // LANES
    assert t % tm == 0
    row = lambda i, p: (i, 0)
    fixed = lambda i, p: (0, 0)
    grid_spec = pltpu.PrefetchScalarGridSpec(
        num_scalar_prefetch=1,
        grid=(t // tm,),
        in_specs=[pl.BlockSpec(memory_space=pl.ANY), pl.BlockSpec((tm, d), row), pl.BlockSpec((tm, LANES), row),
                  pl.BlockSpec((1, d), fixed), pl.BlockSpec((1, d), fixed)],
        out_specs=[pl.BlockSpec((tm, d), row), pl.BlockSpec((tm, d), row)],
        scratch_shapes=[pltpu.VMEM((3, TOP_E, tm * n, LANES), F32), pltpu.SemaphoreType.DMA((3,))],
    )
    return pl.pallas_call(
        functools.partial(_combine_ln_kernel, alpha=alpha),
        grid_spec=grid_spec,
        out_shape=[jax.ShapeDtypeStruct((t, d), F32), jax.ShapeDtypeStruct((t, d), BF16)],
        compiler_params=_params("arbitrary"),
        name="moe_combine_ln",
    )(pos, y_slabs, x, wts, g.reshape(1, d), b.reshape(1, d))


def _dispatch_plan(ids, counts, n_tokens):
    rows = MOE_ROWS
    n_assign = n_tokens * TOP_E
    eid = ids[:, 0:TOP_E].reshape(n_assign)
    rank = ids[:, TOP_E:2 * TOP_E].reshape(n_assign)
    counts = counts[0, :N_EXPERTS].astype(I32)
    padded = (counts + rows - 1) // rows * rows
    pends = jnp.cumsum(padded)
    pstarts = pends - padded
    pos = jnp.take(pstarts, eid) + rank
    n_blk = n_assign // rows + N_EXPERTS
    row_src = jnp.zeros((n_blk * rows,), I32).at[pos].set(jnp.arange(n_assign, dtype=I32) // TOP_E)
    blk_first = jnp.arange(n_blk, dtype=I32) * rows
    blk_expert = jnp.minimum(jnp.sum((pends[None, :] <= blk_first[:, None]).astype(I32), axis=1), N_EXPERTS - 1)
    n_used = (pends[-1] // rows).astype(I32).reshape(1)
    return pos, row_src, blk_expert, n_used


def _moe_ln(x, x_slabs, w_r, b_r, w_gate, w_up, w_down, layer, g, b, *, alpha):
    ids, wts, counts = _router(x, w_r, b_r, tm=256)
    pos, row_src, blk_expert, n_used = _dispatch_plan(ids, counts, x.shape[0])
    y_slabs = _expert_mlp(x_slabs, blk_expert, row_src, n_used, w_gate, w_up, w_down, layer)
    return _combine_ln(pos, y_slabs, x, wts, g, b, alpha=alpha, tm=256)


def kernel(x, moba_w_in, moba_w_o, hgrn_w_in, hgrn_g_norm, hgrn_lb_raw, hgrn_w_o, ln_mix_g, ln_mix_b,
           moe_w_rg, moe_b_rg, moe_w_re, moe_b_re, moe_w_gate, moe_w_up, moe_w_down, ln_ffn_g, ln_ffn_b):
    batch, seq, d = x.shape
    depth = ln_mix_g.shape[0]
    alpha = (2 * depth) ** 0.25
    t = batch * seq
    xf = x.reshape(t, d)
    xb = xf.astype(BF16)
    n_re = N_GROUPS * EXPERTS_PER_GROUP
    moba_w_o_b, hgrn_w_o_b = moba_w_o.astype(BF16), hgrn_w_o.astype(BF16)
    for layer in range(depth):
        j = layer // 2
        if layer % 2 == 0:
            qkv = _matmul(xb, moba_w_in, j, BF16, tm=2048, tn=512, scaled_cols=d, scale=HEAD_DIM ** -0.5 * LOG2_E)
            o = _moba_attention(qkv, batch, seq)
            w_o = moba_w_o_b
        else:
            proj = _matmul(xb, hgrn_w_in, j, F32, tm=2048, tn=512)
            o = _hgrn_mix(proj, hgrn_lb_raw, hgrn_g_norm[j], layer=j, batch=batch, seq=seq)
            w_o = hgrn_w_o_b
        w_re = jnp.transpose(moe_w_re[layer], (1, 0, 2)).reshape(d, n_re)
        w_r = jnp.concatenate([moe_w_rg[layer], w_re, jnp.zeros((d, LANES - N_GROUPS - n_re), F32)], axis=1)
        b_r = jnp.concatenate([moe_b_rg[layer], moe_b_re[layer].reshape(n_re),
                               jnp.zeros((LANES - N_GROUPS - n_re,), F32)]).reshape(1, LANES)
        xf, xs = _proj_ln(o, w_o, j, xf, ln_mix_g[layer], ln_mix_b[layer], alpha=alpha, tm=256)
        xf, xb = _moe_ln(xf, xs, w_r, b_r, moe_w_gate, moe_w_up, moe_w_down, layer,
                         ln_ffn_g[layer], ln_ffn_b[layer], alpha=alpha)
    return xf.reshape(batch, seq, d)
```

```python
import functools

import numpy as np
import jax
import jax.numpy as jnp
from jax import lax
from jax.experimental import pallas as pl
from jax.experimental.pallas import tpu as pltpu

F32 = jnp.float32
BF16 = jnp.bfloat16
I32 = jnp.int32

LANES = 128
SUBLANES = 8
VMEM_LIMIT_BYTES = 56 << 20

HEAD_DIM = 128
MOBA_BLOCK = 256
MOBA_TOPK = 3
HGRN_CHUNK = 64
N_GROUPS = 4
EXPERTS_PER_GROUP = 8
N_EXPERTS = N_GROUPS * EXPERTS_PER_GROUP
TOP_E = 2
LN_EPS = 1e-5
RMS_EPS = 1e-6

MOE_ROWS = 256
COMBINE_ROWS = 64
NT_DIMS = (((1,), (1,)), ((), ()))
TN_DIMS = (((0,), (0,)), ((), ()))


def _params(*semantics):
    return pltpu.CompilerParams(dimension_semantics=semantics, vmem_limit_bytes=VMEM_LIMIT_BYTES)


def _sigmoid(x):
    return 1.0 / (1.0 + jnp.exp(-x))


def _layer_norm(y, g, b):
    mu = jnp.mean(y, axis=-1, keepdims=True)
    d = y - mu
    var = jnp.mean(d * d, axis=-1, keepdims=True)
    return d * lax.rsqrt(var + LN_EPS) * g + b


def _matmul_kernel(x_ref, w_ref, o_ref, *, n_scaled_blocks, scale):
    acc = jnp.dot(x_ref[...], w_ref[...].astype(BF16), preferred_element_type=F32)
    if n_scaled_blocks:
        acc = acc * jnp.where(pl.program_id(1) < n_scaled_blocks, scale, 1.0).astype(F32)
    o_ref[...] = acc.astype(o_ref.dtype)


def _matmul(x, w_stack, layer, out_dtype, *, tm, tn, scaled_cols=0, scale=1.0):
    m, k = x.shape
    n = w_stack.shape[2]
    tm, tn = min(tm, m), min(tn, n)
    assert m % tm == 0 and n % tn == 0 and scaled_cols % tn == 0
    kern = functools.partial(_matmul_kernel, n_scaled_blocks=scaled_cols // tn, scale=scale)
    return pl.pallas_call(
        kern,
        grid=(m // tm, n // tn),
        in_specs=[pl.BlockSpec((tm, k), lambda i, j: (i, 0)),
                  pl.BlockSpec((None, k, tn), lambda i, j: (layer, 0, j))],
        out_specs=pl.BlockSpec((tm, tn), lambda i, j: (i, j)),
        out_shape=jax.ShapeDtypeStruct((m, n), out_dtype),
        compiler_params=_params("parallel", "parallel"),
        name="proj_matmul",
    )(x, w_stack)


def _store_slabs(slab_ref, x):
    rows, d = x.shape
    n = d // LANES
    for c in range(n):
        slab_ref[pl.ds(c, rows, stride=n), :] = x[:, c * LANES:(c + 1) * LANES]


def _load_slabs(slab_ref, rows, d):
    n = d // LANES
    return jnp.concatenate([slab_ref[pl.ds(c, rows, stride=n), :] for c in range(n)], axis=1)


def _proj_ln_kernel(o_ref, w_ref, x_ref, g_ref, b_ref, xo_ref, xs_ref, *, alpha):
    h = jnp.dot(o_ref[...], w_ref[...], preferred_element_type=F32)
    out = _layer_norm(alpha * x_ref[...] + h, g_ref[...], b_ref[...])
    xo_ref[...] = out
    _store_slabs(xs_ref, out)


def _proj_ln(o, w_stack, layer, x, g, b, *, alpha, tm):
    t, d = x.shape
    tm = min(tm, t)
    n = d // LANES
    assert t % tm == 0
    row = lambda i: (i, 0)
    fixed = lambda i: (0, 0)
    return pl.pallas_call(
        functools.partial(_proj_ln_kernel, alpha=alpha),
        grid=(t // tm,),
        in_specs=[pl.BlockSpec((tm, d), row), pl.BlockSpec((None, d, d), lambda i: (layer, 0, 0)),
                  pl.BlockSpec((tm, d), row), pl.BlockSpec((1, d), fixed), pl.BlockSpec((1, d), fixed)],
        out_specs=[pl.BlockSpec((tm, d), row), pl.BlockSpec((tm * n, LANES), row)],
        out_shape=[jax.ShapeDtypeStruct((t, d), F32), jax.ShapeDtypeStruct((t * n, LANES), F32)],
        compiler_params=_params("parallel"),
        name="proj_ln",
    )(o, w_stack, x, g.reshape(1, d), b.reshape(1, d))


MOBA_HEADS_PER_STEP = 4
MOBA_TILE = 2 * MOBA_BLOCK
MOBA_SUM_ROWS = 2 * SUBLANES
LOG2_E = 1.4426950408889634


def _moba_kernel(q_ref, k_ref, v_ref, o_ref, kmean_ref, vt_ref, sel_ref, *, n_blocks):
    blk, tile, hd = MOBA_BLOCK, MOBA_TILE, HEAD_DIM
    heads = range(MOBA_HEADS_PER_STEP)
    i = pl.program_id(2)

    @pl.when(i == 0)
    def _():
        kmean_ref[...] = jnp.zeros_like(kmean_ref)
        for h in heads:
            cols = slice(h * hd, (h + 1) * hd)
            for j in range(n_blocks):
                kj = k_ref[pl.ds(j * blk, blk), cols].astype(F32)
                kmean_ref[h, pl.ds(j, 1), :] = jnp.mean(kj, axis=0, keepdims=True)
            for p in range(n_blocks // 2):
                vt_ref[h, p, 0:hd, :] = v_ref[pl.ds(p * tile, tile), cols].T
                vt_ref[h, p, hd:, :] = jnp.ones((MOBA_SUM_ROWS, tile), BF16)

    q = [q_ref[:, h * hd:(h + 1) * hd] for h in heads]
    for h in heads:
        gate = lax.dot_general(kmean_ref[h], q[h].astype(F32), NT_DIMS,
                               precision=lax.Precision.HIGHEST, preferred_element_type=F32)
        blk_idx = lax.broadcasted_iota(I32, gate.shape, 0)
        gate = jnp.where(blk_idx < i, gate, -jnp.inf)
        rank = jnp.zeros(gate.shape, I32)
        for jp in range(n_blocks):
            row = gate[jp:jp + 1, :]
            beats = (row > gate) | ((row == gate) & (jp < blk_idx))
            rank = rank + jnp.where(beats, 1, 0)
        sel_ref[h] = jnp.where((gate > -jnp.inf) & (rank < MOBA_TOPK), 1.0, 0.0)

    def scores(h, p):
        kt = k_ref[pl.ds(pl.multiple_of(p * tile, tile), tile), h * hd:(h + 1) * hd]
        return lax.dot_general(kt, q[h], NT_DIMS, preferred_element_type=F32)

    def masked(s, keep_top, keep_bot):
        return jnp.concatenate([jnp.where(keep_top, s[:blk], -jnp.inf),
                                jnp.where(keep_bot, s[blk:], -jnp.inf)], axis=0)

    p_own = i // 2
    causal = lax.broadcasted_iota(I32, (blk, blk), 0) <= lax.broadcasted_iota(I32, (blk, blk), 1)
    i_odd = jnp.full((blk, blk), i % 2, I32) == 1
    s_all = [scores(h, p_own) for h in heads]
    m_all, p_all = [], []
    for h in heads:
        chosen_prev = sel_ref[h, pl.ds(2 * p_own, 1), :] > 0.0
        s = masked(s_all[h], (i_odd & chosen_prev) | (~i_odd & causal), i_odd & causal)
        m = jnp.max(s, axis=0, keepdims=True)
        m_all.append(m)
        p_all.append(jnp.exp2(s - m).astype(BF16))
    state = []
    for h in heads:
        state += [m_all[h], jnp.dot(vt_ref[h, p_own], p_all[h], preferred_element_type=F32)]

    def body(pt, carry):
        s_all = [scores(h, pt) for h in heads]
        m_all, p_all, a_all = [], [], []
        for h in heads:
            m = carry[2 * h]
            s = masked(s_all[h], sel_ref[h, pl.ds(2 * pt, 1), :] > 0.0,
                       sel_ref[h, pl.ds(2 * pt + 1, 1), :] > 0.0)
            m_new = jnp.maximum(m, jnp.max(s, axis=0, keepdims=True))
            a_all.append(jnp.exp2(m - m_new))
            p_all.append(jnp.exp2(s - m_new).astype(BF16))
            m_all.append(m_new)
        out = []
        for h in heads:
            out += [m_all[h], a_all[h] * carry[2 * h + 1]
                    + jnp.dot(vt_ref[h, pt], p_all[h], preferred_element_type=F32)]
        return tuple(out)

    state = lax.fori_loop(0, p_own, body, tuple(state))
    for h in heads:
        acc = state[2 * h + 1]
        o_ref[:, h * hd:(h + 1) * hd] = (acc[0:hd, :] / acc[hd:hd + 1, :]).T.astype(o_ref.dtype)


def _moba_attention(qkv, batch, seq):
    t, d3 = qkv.shape
    d = d3 // 3
    hs = MOBA_HEADS_PER_STEP
    n_head_groups = d // (HEAD_DIM * hs)
    n_blocks = seq // MOBA_BLOCK
    assert seq % MOBA_TILE == 0 and d % (HEAD_DIM * hs) == 0
    nb_pad = -(-n_blocks // SUBLANES) * SUBLANES
    kv_spec = lambda off: pl.BlockSpec((seq, HEAD_DIM * hs), lambda b, h, i: (b, off + h))
    q_spec = pl.BlockSpec((MOBA_BLOCK, HEAD_DIM * hs), lambda b, h, i: (b * n_blocks + i, h))
    return pl.pallas_call(
        functools.partial(_moba_kernel, n_blocks=n_blocks),
        grid=(batch, n_head_groups, n_blocks),
        in_specs=[q_spec, kv_spec(n_head_groups), kv_spec(2 * n_head_groups)],
        out_specs=q_spec,
        out_shape=jax.ShapeDtypeStruct((t, d), BF16),
        scratch_shapes=[pltpu.VMEM((hs, nb_pad, HEAD_DIM), F32),
                        pltpu.VMEM((hs, n_blocks // 2, HEAD_DIM + MOBA_SUM_ROWS, MOBA_TILE), BF16),
                        pltpu.VMEM((hs, nb_pad, MOBA_BLOCK), F32)],
        compiler_params=_params("parallel", "parallel", "arbitrary"),
        name="moba_attention",
    )(qkv, qkv, qkv)


_HGRN_LEVELS = (32, 16, 8, 4, 2, 1)
HGRN_GROUP = 8


def _hgrn_constants():
    c = HGRN_CHUNK
    idx = np.arange(c)
    t, s = idx[:, None], idx[None, :]
    sums = [(s <= t)]
    upper, pair = [], []
    for m in _HGRN_LEVELS:
        same = (t // (2 * m)) == (s // (2 * m))
        t_up, s_up = (t % (2 * m)) >= m, (s % (2 * m)) >= m
        up_rows = same & t_up & s_up & (s <= t)
        lo_rows = same & ~t_up & ~s_up & (s > t)
        if m > 1:
            sums.append(up_rows | lo_rows)
        upper.append(np.broadcast_to(t_up, (c, LANES)))
        pair.append(same & t_up & ~s_up)
    cumsum = np.concatenate([sums[0]] * 3 + [np.zeros((c, c), bool)], axis=1)
    levels = np.concatenate(sums[1:], 0)
    return (cumsum.astype(np.float32), np.concatenate([levels, levels], axis=1).astype(np.float32),
            np.stack(upper).astype(np.float32), np.stack(pair).astype(np.float32))


def _hgrn_kernel(q_ref, f_ref, i_ref, g_ref, lbraw_ref, gn_ref, cumsum_ref, sums_ref, upper_ref, pair_ref, o_ref,
                 qg_ref, oin_ref, u_ref, dl_ref, *, layer, n_chunks):
    c = HGRN_CHUNK
    dk = HEAD_DIM
    raw = lbraw_ref[...]
    e = jnp.exp(raw - jnp.max(raw, axis=0, keepdims=True))
    sm = e / jnp.sum(e, axis=0, keepdims=True)
    cum = sm[0:1, :]
    for l in range(1, layer + 1):
        cum = cum + sm[l:l + 1, :]
    lb = cum - sm[0:1, :]
    gnorm = gn_ref[...]
    eye = (lax.broadcasted_iota(I32, (c, c), 0) == lax.broadcasted_iota(I32, (c, c), 1))

    group = range(HGRN_GROUP)
    chunk = lambda x, n: x[n * c:(n + 1) * c]
    n_levels = len(_HGRN_LEVELS)

    def local(gi, carry):
        rows = pl.ds(pl.multiple_of(gi * (HGRN_GROUP * c), HGRN_GROUP * c), HGRN_GROUP * c)
        qv, fv, iv = q_ref[rows, :], f_ref[rows, :], i_ref[rows, :]
        qs = qv * _sigmoid(qv)
        fg = lb + (1.0 - lb) * _sigmoid(fv)
        log_f = jnp.log(fg)
        kk = 1.0 - fg
        iv_b = iv.astype(BF16)
        hi = log_f.astype(BF16)
        rest = log_f - hi.astype(F32)
        mid = rest.astype(BF16)
        lo = (rest - mid.astype(F32)).astype(BF16)
        zero = jnp.zeros((c, dk), BF16)
        parts4 = jnp.concatenate([jnp.concatenate([chunk(t, n) for t in (hi, mid, lo)] + [zero], axis=0)
                                  for n in group], axis=1)
        parts2 = jnp.concatenate([jnp.concatenate([chunk(t, n) for t in (hi, mid)], axis=0)
                                  for n in group], axis=1)
        s4 = jnp.dot(cumsum_ref[...], parts4, preferred_element_type=F32)
        s2 = jnp.dot(sums_ref[...], parts2, preferred_element_type=F32)
        gcum = jnp.concatenate([s4[:, n * dk:(n + 1) * dk] for n in group], axis=0)
        level_sums = [s2[:, n * dk:(n + 1) * dk] for n in group]
        qg_ref[rows, :] = (qs * jnp.exp(gcum)).astype(BF16)
        qe, ke = [], []
        for n in group:
            for li in range(n_levels):
                is_up = upper_ref[li] > 0.0
                if _HGRN_LEVELS[li] == 1:
                    dec = jnp.exp(jnp.where(is_up, chunk(log_f, n), 0.0))
                else:
                    dec = jnp.exp(level_sums[n][li * c:(li + 1) * c, :])
                qe.append((chunk(qs, n) * dec).astype(BF16))
                ke.append((chunk(kk, n) * dec).astype(BF16))
        prods = [lax.dot_general(a, b, NT_DIMS, preferred_element_type=F32) for a, b in zip(qe, ke)]
        diag = jnp.sum(qs * kk, axis=1, keepdims=True)
        a_all = []
        for n in group:
            a = jnp.where(eye, chunk(diag, n), 0.0)
            for li in range(n_levels):
                a = a + jnp.where(pair_ref[li] > 0.0, prods[n * n_levels + li], 0.0)
            a_all.append(a.astype(BF16))
        o_in = [jnp.dot(a_all[n], chunk(iv_b, n), preferred_element_type=F32) for n in group]
        oin_ref[rows, :] = jnp.concatenate(o_in, axis=0)
        g_last = [gcum[(n + 1) * c - 1:(n + 1) * c, :] for n in group]
        k_dec = [(chunk(kk, n) * jnp.exp(g_last[n] - chunk(gcum, n))).astype(BF16) for n in group]
        for n in group:
            u_ref[gi * HGRN_GROUP + n] = lax.dot_general(chunk(iv_b, n), k_dec[n], TN_DIMS,
                                                         preferred_element_type=F32)
        dl_ref[pl.ds(pl.multiple_of(gi * SUBLANES, SUBLANES), SUBLANES), :] = jnp.concatenate(
            [jnp.exp(g) for g in g_last] + [jnp.zeros((1, dk), F32)] * (SUBLANES - HGRN_GROUP), axis=0)
        return carry

    lax.fori_loop(0, n_chunks // HGRN_GROUP, local, 0)

    def recur(gi, state_t):
        rows = pl.ds(pl.multiple_of(gi * (HGRN_GROUP * c), HGRN_GROUP * c), HGRN_GROUP * c)
        decay = dl_ref[pl.ds(pl.multiple_of(gi * SUBLANES, SUBLANES), SUBLANES), :]
        states = []
        for n in group:
            states.append(state_t.astype(BF16))
            state_t = decay[n:n + 1, :] * state_t + u_ref[gi * HGRN_GROUP + n]
        qg = qg_ref[rows, :]
        o = oin_ref[rows, :] + jnp.concatenate(
            [lax.dot_general(chunk(qg, n), states[n], NT_DIMS, preferred_element_type=F32) for n in group], axis=0)
        gv = g_ref[rows, :]
        o = o * lax.rsqrt(jnp.mean(o * o, axis=1, keepdims=True) + RMS_EPS) * gnorm
        o_ref[rows, :] = (o * (gv * _sigmoid(gv))).astype(o_ref.dtype)
        return state_t

    lax.fori_loop(0, n_chunks // HGRN_GROUP, recur, jnp.zeros((HEAD_DIM, HEAD_DIM), F32))


def _hgrn_mix(proj, lb_raw, g_norm, *, layer, batch, seq):
    t, d4 = proj.shape
    d = d4 // 4
    n_heads = d // HEAD_DIM
    n_chunks = seq // HGRN_CHUNK
    assert seq % (HGRN_CHUNK * HGRN_GROUP) == 0 and HGRN_GROUP <= SUBLANES
    cumsum, sums, upper, pair = _hgrn_constants()
    n_layers = lb_raw.shape[0]
    part = lambda off: pl.BlockSpec((seq, HEAD_DIM), lambda b, h: (b, off + h))
    whole = lambda a: pl.BlockSpec(a.shape, lambda b, h: (0,) * a.ndim)
    return pl.pallas_call(
        functools.partial(_hgrn_kernel, layer=layer, n_chunks=n_chunks),
        grid=(batch, n_heads),
        in_specs=[part(0), part(n_heads), part(2 * n_heads), part(3 * n_heads),
                  pl.BlockSpec((n_layers, HEAD_DIM), lambda b, h: (0, h)),
                  pl.BlockSpec((1, HEAD_DIM), lambda b, h: (0, 0)),
                  whole(cumsum), whole(sums), whole(upper), whole(pair)],
        out_specs=pl.BlockSpec((seq, HEAD_DIM), lambda b, h: (b, h)),
        out_shape=jax.ShapeDtypeStruct((t, d), BF16),
        scratch_shapes=[pltpu.VMEM((seq, HEAD_DIM), BF16),
                        pltpu.VMEM((seq, HEAD_DIM), F32),
                        pltpu.VMEM((n_chunks, HEAD_DIM, HEAD_DIM), F32),
                        pltpu.VMEM((n_chunks // HGRN_GROUP * SUBLANES, HEAD_DIM), F32)],
        compiler_params=_params("parallel", "parallel"),
        name="hgrn_recurrence",
    )(proj, proj, proj, proj, lb_raw, g_norm.reshape(1, HEAD_DIM),
      jnp.asarray(cumsum, BF16), jnp.asarray(sums, BF16), jnp.asarray(upper), jnp.asarray(pair))


def _router_kernel(x_ref, w_ref, b_ref, ids_ref, wts_ref, counts_ref):
    g, eg = N_GROUPS, EXPERTS_PER_GROUP
    tm = x_ref.shape[0]

    @pl.when(pl.program_id(0) == 0)
    def _():
        counts_ref[...] = jnp.zeros_like(counts_ref)

    x, w = x_ref[...], w_ref[...]
    x_hi, w_hi = x.astype(BF16), w.astype(BF16)
    x_lo, w_lo = (x - x_hi.astype(F32)).astype(BF16), (w - w_hi.astype(F32)).astype(BF16)
    logits = ((jnp.dot(x_hi, w_hi, preferred_element_type=F32) + jnp.dot(x_hi, w_lo, preferred_element_type=F32))
              + jnp.dot(x_lo, w_hi, preferred_element_type=F32)) + b_ref[...]
    lane = lax.broadcasted_iota(I32, logits.shape, 1)

    def softmax_over(mask):
        z = jnp.where(mask, logits, -jnp.inf)
        ez = jnp.exp(z - jnp.max(z, axis=1, keepdims=True))
        return ez / jnp.sum(ez, axis=1, keepdims=True)

    def top1(p, mask):
        best = jnp.max(jnp.where(mask, p, -1.0), axis=1, keepdims=True)
        where = jnp.min(jnp.where(mask & (p == best), lane, LANES), axis=1, keepdims=True)
        return best, where

    is_group = lane < g
    g_w, g_idx = top1(softmax_over(is_group), is_group)
    lo = g + g_idx * eg
    in_group = (lane >= lo) & (lane < lo + eg)
    pe = softmax_over(in_group)
    w1, i1 = top1(pe, in_group)
    w2, i2 = top1(pe, in_group & (lane != i1))
    den = w1 + w2
    wt1, wt2 = g_w * (w1 / den), g_w * (w2 / den)
    e1, e2 = i1 - g, i2 - g
    hot1, hot2 = lane == e1, lane == e2
    hot = jnp.where(hot1 | hot2, 1.0, 0.0)
    earlier = lax.broadcasted_iota(I32, (tm, tm), 1) < lax.broadcasted_iota(I32, (tm, tm), 0)
    before = counts_ref[...] + jnp.dot(jnp.where(earlier, 1.0, 0.0).astype(BF16), hot.astype(BF16),
                                       preferred_element_type=F32)
    r1 = jnp.sum(jnp.where(hot1, before, 0.0), axis=1, keepdims=True).astype(I32)
    r2 = jnp.sum(jnp.where(hot2, before, 0.0), axis=1, keepdims=True).astype(I32)
    counts_ref[...] += jnp.sum(hot, axis=0, keepdims=True)
    ids_ref[...] = jnp.where(lane == 0, e1, jnp.where(lane == 1, e2, jnp.where(lane == 2, r1,
                                                                               jnp.where(lane == 3, r2, 0))))
    wts_ref[...] = jnp.where(lane == 0, wt1, jnp.where(lane == 1, wt2, 0.0))


def _router(x, w_r, b_r, *, tm):
    t, d = x.shape
    tm = min(tm, t)
    assert t % tm == 0
    row = lambda i: (i, 0)
    fixed = lambda i: (0, 0)
    return pl.pallas_call(
        _router_kernel,
        grid=(t // tm,),
        in_specs=[pl.BlockSpec((tm, d), row), pl.BlockSpec((d, LANES), fixed), pl.BlockSpec((1, LANES), fixed)],
        out_specs=[pl.BlockSpec((tm, LANES), row), pl.BlockSpec((tm, LANES), row), pl.BlockSpec((1, LANES), fixed)],
        out_shape=[jax.ShapeDtypeStruct((t, LANES), I32), jax.ShapeDtypeStruct((t, LANES), F32),
                   jax.ShapeDtypeStruct((1, LANES), F32)],
        compiler_params=_params("arbitrary"),
        name="moe_router",
    )(x, w_r, b_r)


def _expert_kernel(blk_expert_ref, row_src_ref, n_used_ref, x_hbm, wg_ref, wu_ref, wd_ref, y_ref,
                   xbuf, wg_b, wu_b, wd_b, in_sems):
    i = pl.program_id(0)
    n_blk = pl.num_programs(0)
    d = wg_ref.shape[0]
    n = d // LANES
    rows = xbuf.shape[1] // n
    n_in = xbuf.shape[0]
    n_used = n_used_ref[0]
    active = i < n_used

    def gather(block):
        s = block % n_in
        block = jnp.minimum(block, n_blk - 1)
        for r in range(rows):
            tok = pl.multiple_of(row_src_ref[block * rows + r] * n, n)
            pltpu.make_async_copy(x_hbm.at[pl.ds(tok, n), :], xbuf.at[s, pl.ds(r * n, n), :], in_sems.at[s]).start()

    def wait_gather(block):
        s = block % n_in
        pltpu.make_async_copy(x_hbm.at[pl.ds(0, rows * n), :], xbuf.at[s], in_sems.at[s]).wait()

    def drain(first_unused_block):
        wait_gather(first_unused_block)
        wait_gather(first_unused_block + 1)

    @pl.when(i == 0)
    def _():
        gather(0)
        gather(1)

    @pl.when(active)
    def _():
        wait_gather(i)

        @pl.when((i == 0) | (blk_expert_ref[i] != blk_expert_ref[jnp.maximum(i - 1, 0)]))
        def _():
            wg_b[...] = wg_ref[...].astype(BF16)
            wu_b[...] = wu_ref[...].astype(BF16)
            wd_b[...] = wd_ref[...].astype(BF16)

        xb = _load_slabs(xbuf.at[i % n_in], rows, d).astype(BF16)
        hg = jnp.dot(xb, wg_b[...], preferred_element_type=F32)
        hu = jnp.dot(xb, wu_b[...], preferred_element_type=F32)
        h = (hg * _sigmoid(hg)) * hu
        _store_slabs(y_ref, jnp.dot(h.astype(BF16), wd_b[...], preferred_element_type=F32))
        gather(i + 2)

        @pl.when(i == n_blk - 1)
        def _():
            drain(i + 1)

    @pl.when(i == n_used)
    def _():
        drain(i)

    @pl.when(jnp.logical_not(active))
    def _():
        y_ref[...] = jnp.zeros_like(y_ref)


def _expert_mlp(x_slabs, blk_expert, row_src, n_used, w_gate, w_up, w_down, layer):
    _, n_e, d, f = w_gate.shape
    n_blk = blk_expert.shape[0]
    rows = MOE_ROWS
    n = d // LANES
    by_expert = lambda i, be, rs, nu: (layer, be[i], 0, 0)
    grid_spec = pltpu.PrefetchScalarGridSpec(
        num_scalar_prefetch=3,
        grid=(n_blk,),
        in_specs=[pl.BlockSpec(memory_space=pl.ANY),
                  pl.BlockSpec((None, None, d, f), by_expert), pl.BlockSpec((None, None, d, f), by_expert),
                  pl.BlockSpec((None, None, f, d), by_expert)],
        out_specs=pl.BlockSpec((rows * n, LANES), lambda i, be, rs, nu: (i, 0)),
        scratch_shapes=[pltpu.VMEM((3, rows * n, LANES), F32),
                        pltpu.VMEM((d, f), BF16), pltpu.VMEM((d, f), BF16), pltpu.VMEM((f, d), BF16),
                        pltpu.SemaphoreType.DMA((3,))],
    )
    return pl.pallas_call(
        _expert_kernel,
        grid_spec=grid_spec,
        out_shape=jax.ShapeDtypeStruct((n_blk * rows * n, LANES), F32),
        compiler_params=_params("arbitrary"),
        name="moe_experts",
    )(blk_expert, row_src, n_used, x_slabs, w_gate, w_up, w_down)


def _combine_ln_kernel(pos_ref, y_hbm, x_ref, wts_ref, g_ref, b_ref, xo_ref, xb_ref, ybuf, sems, *, alpha):
    i = pl.program_id(0)
    n_steps = pl.num_programs(0)
    tm, d = x_ref.shape
    n = d // LANES
    n_buf = ybuf.shape[0]

    def gather(step):
        s = step % n_buf
        step = jnp.minimum(step, n_steps - 1)
        for r in range(tm):
            for k in range(TOP_E):
                p = pl.multiple_of(pos_ref[(step * tm + r) * TOP_E + k] * n, n)
                pltpu.make_async_copy(y_hbm.at[pl.ds(p, n), :], ybuf.at[s, k, pl.ds(r * n, n), :], sems.at[s]).start()

    def wait_gather(step):
        s = step % n_buf
        for k in range(TOP_E):
            pltpu.make_async_copy(y_hbm.at[pl.ds(0, tm * n), :], ybuf.at[s, k], sems.at[s]).wait()

    @pl.when(i == 0)
    def _():
        gather(0)
        gather(1)

    wait_gather(i)
    slot = i % n_buf

    def rows_chunk(c, carry):
        rows = pl.ds(pl.multiple_of(c * COMBINE_ROWS, COMBINE_ROWS), COMBINE_ROWS)
        slabs = pl.ds(pl.multiple_of(c * (COMBINE_ROWS * n), COMBINE_ROWS * n), COMBINE_ROWS * n)
        wts = wts_ref[rows, :]
        m = (wts[:, 0:1] * _load_slabs(ybuf.at[slot, 0, slabs], COMBINE_ROWS, d)
             + wts[:, 1:2] * _load_slabs(ybuf.at[slot, 1, slabs], COMBINE_ROWS, d))
        out = _layer_norm(alpha * x_ref[rows, :] + m, g_ref[...], b_ref[...])
        xo_ref[rows, :] = out
        xb_ref[rows, :] = out.astype(BF16)
        return carry

    lax.fori_loop(0, tm // COMBINE_ROWS, rows_chunk, 0)
    gather(i + 2)

    @pl.when(i == n_steps - 1)
    def _():
        wait_gather(i + 1)
        wait_gather(i + 2)


def _combine_ln(pos, y_slabs, x, wts, g, b, *, alpha, tm):
    t, d = x.shape
    tm = min(tm, t)
    n = d // LANES
    assert t % tm == 0
    row = lambda i, p: (i, 0)
    fixed = lambda i, p: (0, 0)
    grid_spec = pltpu.PrefetchScalarGridSpec(
        num_scalar_prefetch=1,
        grid=(t // tm,),
        in_specs=[pl.BlockSpec(memory_space=pl.ANY), pl.BlockSpec((tm, d), row), pl.BlockSpec((tm, LANES), row),
                  pl.BlockSpec((1, d), fixed), pl.BlockSpec((1, d), fixed)],
        out_specs=[pl.BlockSpec((tm, d), row), pl.BlockSpec((tm, d), row)],
        scratch_shapes=[pltpu.VMEM((3, TOP_E, tm * n, LANES), F32), pltpu.SemaphoreType.DMA((3,))],
    )
    return pl.pallas_call(
        functools.partial(_combine_ln_kernel, alpha=alpha),
        grid_spec=grid_spec,
        out_shape=[jax.ShapeDtypeStruct((t, d), F32), jax.ShapeDtypeStruct((t, d), BF16)],
        compiler_params=_params("arbitrary"),
        name="moe_combine_ln",
    )(pos, y_slabs, x, wts, g.reshape(1, d), b.reshape(1, d))


def _dispatch_plan(ids, counts, n_tokens):
    rows = MOE_ROWS
    n_assign = n_tokens * TOP_E
    eid = ids[:, 0:TOP_E].reshape(n_assign)
    rank = ids[:, TOP_E:2 * TOP_E].reshape(n_assign)
    counts = counts[0, :N_EXPERTS].astype(I32)
    padded = (counts + rows - 1) // rows * rows
    pends = jnp.cumsum(padded)
    pstarts = pends - padded
    pos = jnp.take(pstarts, eid) + rank
    n_blk = n_assign // rows + N_EXPERTS
    row_src = jnp.zeros((n_blk * rows,), I32).at[pos].set(jnp.arange(n_assign, dtype=I32) // TOP_E)
    blk_first = jnp.arange(n_blk, dtype=I32) * rows
    blk_expert = jnp.minimum(jnp.sum((pends[None, :] <= blk_first[:, None]).astype(I32), axis=1), N_EXPERTS - 1)
    n_used = (pends[-1] // rows).astype(I32).reshape(1)
    return pos, row_src, blk_expert, n_used


def _moe_ln(x, x_slabs, w_r, b_r, w_gate, w_up, w_down, layer, g, b, *, alpha):
    ids, wts, counts = _router(x, w_r, b_r, tm=256)
    pos, row_src, blk_expert, n_used = _dispatch_plan(ids, counts, x.shape[0])
    y_slabs = _expert_mlp(x_slabs, blk_expert, row_src, n_used, w_gate, w_up, w_down, layer)
    return _combine_ln(pos, y_slabs, x, wts, g, b, alpha=alpha, tm=256)


def kernel(x, moba_w_in, moba_w_o, hgrn_w_in, hgrn_g_norm, hgrn_lb_raw, hgrn_w_o, ln_mix_g, ln_mix_b,
           moe_w_rg, moe_b_rg, moe_w_re, moe_b_re, moe_w_gate, moe_w_up, moe_w_down, ln_ffn_g, ln_ffn_b):
    batch, seq, d = x.shape
    depth = ln_mix_g.shape[0]
    alpha = (2 * depth) ** 0.25
    t = batch * seq
    xf = x.reshape(t, d)
    xb = xf.astype(BF16)
    n_re = N_GROUPS * EXPERTS_PER_GROUP
    moba_w_o_b, hgrn_w_o_b = moba_w_o.astype(BF16), hgrn_w_o.astype(BF16)
    for layer in range(depth):
        j = layer // 2
        if layer % 2 == 0:
            qkv = _matmul(xb, moba_w_in, j, BF16, tm=2048, tn=512, scaled_cols=d, scale=HEAD_DIM ** -0.5 * LOG2_E)
            o = _moba_attention(qkv, batch, seq)
            w_o = moba_w_o_b
        else:
            proj = _matmul(xb, hgrn_w_in, j, F32, tm=2048, tn=512)
            o = _hgrn_mix(proj, hgrn_lb_raw, hgrn_g_norm[j], layer=j, batch=batch, seq=seq)
            w_o = hgrn_w_o_b
        w_re = jnp.transpose(moe_w_re[layer], (1, 0, 2)).reshape(d, n_re)
        w_r = jnp.concatenate([moe_w_rg[layer], w_re, jnp.zeros((d, LANES - N_GROUPS - n_re), F32)], axis=1)
        b_r = jnp.concatenate([moe_b_rg[layer], moe_b_re[layer].reshape(n_re),
                               jnp.zeros((LANES - N_GROUPS - n_re,), F32)]).reshape(1, LANES)
        xf, xs = _proj_ln(o, w_o, j, xf, ln_mix_g[layer], ln_mix_b[layer], alpha=alpha, tm=256)
        xf, xb = _moe_ln(xf, xs, w_r, b_r, moe_w_gate, moe_w_up, moe_w_down, layer,
                         ln_ffn_g[layer], ln_ffn_b[layer], alpha=alpha)
    return xf.reshape(batch, seq, d)
```

```python
import functools

import numpy as np
import jax
import jax.numpy as jnp
from jax import lax
from jax.experimental import pallas as pl
from jax.experimental.pallas import tpu as pltpu

F32 = jnp.float32
BF16 = jnp.bfloat16
I32 = jnp.int32

LANES = 128
SUBLANES = 8
VMEM_LIMIT_BYTES = 56 << 20

HEAD_DIM = 128
MOBA_BLOCK = 256
MOBA_TOPK = 3
HGRN_CHUNK = 64
N_GROUPS = 4
EXPERTS_PER_GROUP = 8
N_EXPERTS = N_GROUPS * EXPERTS_PER_GROUP
TOP_E = 2
LN_EPS = 1e-5
RMS_EPS = 1e-6

MOE_ROWS = 256
COMBINE_ROWS = 64
NT_DIMS = (((1,), (1,)), ((), ()))
TN_DIMS = (((0,), (0,)), ((), ()))


def _params(*semantics):
    return pltpu.CompilerParams(dimension_semantics=semantics, vmem_limit_bytes=VMEM_LIMIT_BYTES)


def _sigmoid(x):
    return 1.0 / (1.0 + jnp.exp(-x))


def _layer_norm(y, g, b):
    mu = jnp.mean(y, axis=-1, keepdims=True)
    d = y - mu
    var = jnp.mean(d * d, axis=-1, keepdims=True)
    return d * lax.rsqrt(var + LN_EPS) * g + b


def _matmul_kernel(x_ref, w_ref, o_ref, *, n_scaled_blocks, scale):
    acc = jnp.dot(x_ref[...], w_ref[...].astype(BF16), preferred_element_type=F32)
    if n_scaled_blocks:
        acc = acc * jnp.where(pl.program_id(1) < n_scaled_blocks, scale, 1.0).astype(F32)
    o_ref[...] = acc.astype(o_ref.dtype)


def _matmul(x, w_stack, layer, out_dtype, *, tm, tn, scaled_cols=0, scale=1.0):
    m, k = x.shape
    n = w_stack.shape[2]
    tm, tn = min(tm, m), min(tn, n)
    assert m % tm == 0 and n % tn == 0 and scaled_cols % tn == 0
    kern = functools.partial(_matmul_kernel, n_scaled_blocks=scaled_cols // tn, scale=scale)
    return pl.pallas_call(
        kern,
        grid=(m // tm, n // tn),
        in_specs=[pl.BlockSpec((tm, k), lambda i, j: (i, 0)),
                  pl.BlockSpec((None, k, tn), lambda i, j: (layer, 0, j))],
        out_specs=pl.BlockSpec((tm, tn), lambda i, j: (i, j)),
        out_shape=jax.ShapeDtypeStruct((m, n), out_dtype),
        compiler_params=_params("parallel", "parallel"),
        name="proj_matmul",
    )(x, w_stack)


def _store_slabs(slab_ref, x):
    rows, d = x.shape
    n = d // LANES
    for c in range(n):
        slab_ref[pl.ds(c, rows, stride=n), :] = x[:, c * LANES:(c + 1) * LANES]


def _load_slabs(slab_ref, rows, d):
    n = d // LANES
    return jnp.concatenate([slab_ref[pl.ds(c, rows, stride=n), :] for c in range(n)], axis=1)


def _proj_ln_kernel(o_ref, w_ref, x_ref, g_ref, b_ref, xo_ref, xs_ref, *, alpha):
    h = jnp.dot(o_ref[...], w_ref[...], preferred_element_type=F32)
    out = _layer_norm(alpha * x_ref[...] + h, g_ref[...], b_ref[...])
    xo_ref[...] = out
    _store_slabs(xs_ref, out)


def _proj_ln(o, w_stack, layer, x, g, b, *, alpha, tm):
    t, d = x.shape
    tm = min(tm, t)
    n = d // LANES
    assert t % tm == 0
    row = lambda i: (i, 0)
    fixed = lambda i: (0, 0)
    return pl.pallas_call(
        functools.partial(_proj_ln_kernel, alpha=alpha),
        grid=(t // tm,),
        in_specs=[pl.BlockSpec((tm, d), row), pl.BlockSpec((None, d, d), lambda i: (layer, 0, 0)),
                  pl.BlockSpec((tm, d), row), pl.BlockSpec((1, d), fixed), pl.BlockSpec((1, d), fixed)],
        out_specs=[pl.BlockSpec((tm, d), row), pl.BlockSpec((tm * n, LANES), row)],
        out_shape=[jax.ShapeDtypeStruct((t, d), F32), jax.ShapeDtypeStruct((t * n, LANES), F32)],
        compiler_params=_params("parallel"),
        name="proj_ln",
    )(o, w_stack, x, g.reshape(1, d), b.reshape(1, d))


MOBA_HEADS_PER_STEP = 4
MOBA_TILE = 2 * MOBA_BLOCK
MOBA_SUM_ROWS = 2 * SUBLANES
LOG2_E = 1.4426950408889634


def _moba_kernel(q_ref, k_ref, v_ref, o_ref, kmean_ref, vt_ref, sel_ref, *, n_blocks):
    blk, tile, hd = MOBA_BLOCK, MOBA_TILE, HEAD_DIM
    heads = range(MOBA_HEADS_PER_STEP)
    i = pl.program_id(2)

    @pl.when(i == 0)
    def _():
        kmean_ref[...] = jnp.zeros_like(kmean_ref)
        for h in heads:
            cols = slice(h * hd, (h + 1) * hd)
            for j in range(n_blocks):
                kj = k_ref[pl.ds(j * blk, blk), cols].astype(F32)
                kmean_ref[h, pl.ds(j, 1), :] = jnp.mean(kj, axis=0, keepdims=True)
            for p in range(n_blocks // 2):
                vt_ref[h, p, 0:hd, :] = v_ref[pl.ds(p * tile, tile), cols].T
                vt_ref[h, p, hd:, :] = jnp.ones((MOBA_SUM_ROWS, tile), BF16)

    q = [q_ref[:, h * hd:(h + 1) * hd] for h in heads]
    for h in heads:
        gate = lax.dot_general(kmean_ref[h], q[h].astype(F32), NT_DIMS,
                               precision=lax.Precision.HIGHEST, preferred_element_type=F32)
        blk_idx = lax.broadcasted_iota(I32, gate.shape, 0)
        gate = jnp.where(blk_idx < i, gate, -jnp.inf)
        rank = jnp.zeros(gate.shape, I32)
        for jp in range(n_blocks):
            row = gate[jp:jp + 1, :]
            beats = (row > gate) | ((row == gate) & (jp < blk_idx))
            rank = rank + jnp.where(beats, 1, 0)
        sel_ref[h] = jnp.where((gate > -jnp.inf) & (rank < MOBA_TOPK), 1.0, 0.0)

    def scores(h, p):
        kt = k_ref[pl.ds(pl.multiple_of(p * tile, tile), tile), h * hd:(h + 1) * hd]
        return lax.dot_general(kt, q[h], NT_DIMS, preferred_element_type=F32)

    def masked(s, keep_top, keep_bot):
        return jnp.concatenate([jnp.where(keep_top, s[:blk], -jnp.inf),
                                jnp.where(keep_bot, s[blk:], -jnp.inf)], axis=0)

    p_own = i // 2
    causal = lax.broadcasted_iota(I32, (blk, blk), 0) <= lax.broadcasted_iota(I32, (blk, blk), 1)
    i_odd = jnp.full((blk, blk), i % 2, I32) == 1
    s_all = [scores(h, p_own) for h in heads]
    m_all, p_all = [], []
    for h in heads:
        chosen_prev = sel_ref[h, pl.ds(2 * p_own, 1), :] > 0.0
        s = masked(s_all[h], (i_odd & chosen_prev) | (~i_odd & causal), i_odd & causal)
        m = jnp.max(s, axis=0, keepdims=True)
        m_all.append(m)
        p_all.append(jnp.exp2(s - m).astype(BF16))
    state = []
    for h in heads:
        state += [m_all[h], jnp.dot(vt_ref[h, p_own], p_all[h], preferred_element_type=F32)]

    def body(pt, carry):
        s_all = [scores(h, pt) for h in heads]
        m_all, p_all, a_all = [], [], []
        for h in heads:
            m = carry[2 * h]
            s = masked(s_all[h], sel_ref[h, pl.ds(2 * pt, 1), :] > 0.0,
                       sel_ref[h, pl.ds(2 * pt + 1, 1), :] > 0.0)
            m_new = jnp.maximum(m, jnp.max(s, axis=0, keepdims=True))
            a_all.append(jnp.exp2(m - m_new))
            p_all.append(jnp.exp2(s - m_new).astype(BF16))
            m_all.append(m_new)
        out = []
        for h in heads:
            out += [m_all[h], a_all[h] * carry[2 * h + 1]
                    + jnp.dot(vt_ref[h, pt], p_all[h], preferred_element_type=F32)]
        return tuple(out)

    state = lax.fori_loop(0, p_own, body, tuple(state))
    for h in heads:
        acc = state[2 * h + 1]
        o_ref[:, h * hd:(h + 1) * hd] = (acc[0:hd, :] / acc[hd:hd + 1, :]).T.astype(o_ref.dtype)


def _moba_attention(qkv, batch, seq):
    t, d3 = qkv.shape
    d = d3 // 3
    hs = MOBA_HEADS_PER_STEP
    n_head_groups = d // (HEAD_DIM * hs)
    n_blocks = seq // MOBA_BLOCK
    assert seq % MOBA_TILE == 0 and d % (HEAD_DIM * hs) == 0
    nb_pad = -(-n_blocks // SUBLANES) * SUBLANES
    kv_spec = lambda off: pl.BlockSpec((seq, HEAD_DIM * hs), lambda b, h, i: (b, off + h))
    q_spec = pl.BlockSpec((MOBA_BLOCK, HEAD_DIM * hs), lambda b, h, i: (b * n_blocks + i, h))
    return pl.pallas_call(
        functools.partial(_moba_kernel, n_blocks=n_blocks),
        grid=(batch, n_head_groups, n_blocks),
        in_specs=[q_spec, kv_spec(n_head_groups), kv_spec(2 * n_head_groups)],
        out_specs=q_spec,
        out_shape=jax.ShapeDtypeStruct((t, d), BF16),
        scratch_shapes=[pltpu.VMEM((hs, nb_pad, HEAD_DIM), F32),
                        pltpu.VMEM((hs, n_blocks // 2, HEAD_DIM + MOBA_SUM_ROWS, MOBA_TILE), BF16),
                        pltpu.VMEM((hs, nb_pad, MOBA_BLOCK), F32)],
        compiler_params=_params("parallel", "parallel", "arbitrary"),
        name="moba_attention",
    )(qkv, qkv, qkv)


_HGRN_LEVELS = (32, 16, 8, 4, 2, 1)
HGRN_GROUP = 8


def _hgrn_constants():
    c = HGRN_CHUNK
    idx = np.arange(c)
    t, s = idx[:, None], idx[None, :]
    sums = [(s <= t)]
    upper, pair = [], []
    for m in _HGRN_LEVELS:
        same = (t // (2 * m)) == (s // (2 * m))
        t_up, s_up = (t % (2 * m)) >= m, (s % (2 * m)) >= m
        up_rows = same & t_up & s_up & (s <= t)
        lo_rows = same & ~t_up & ~s_up & (s > t)
        if m > 1:
            sums.append(up_rows | lo_rows)
        upper.append(np.broadcast_to(t_up, (c, LANES)))
        pair.append(same & t_up & ~s_up)
    cumsum = np.concatenate([sums[0]] * 3 + [np.zeros((c, c), bool)], axis=1)
    levels = np.concatenate(sums[1:], 0)
    return (cumsum.astype(np.float32), np.concatenate([levels, levels], axis=1).astype(np.float32),
            np.stack(upper).astype(np.float32), np.stack(pair).astype(np.float32))


def _hgrn_kernel(q_ref, f_ref, i_ref, g_ref, lbraw_ref, gn_ref, cumsum_ref, sums_ref, upper_ref, pair_ref, o_ref,
                 qg_ref, oin_ref, u_ref, dl_ref, *, layer, n_chunks):
    c = HGRN_CHUNK
    dk = HEAD_DIM
    raw = lbraw_ref[...]
    e = jnp.exp(raw - jnp.max(raw, axis=0, keepdims=True))
    sm = e / jnp.sum(e, axis=0, keepdims=True)
    cum = sm[0:1, :]
    for l in range(1, layer + 1):
        cum = cum + sm[l:l + 1, :]
    lb = cum - sm[0:1, :]
    gnorm = gn_ref[...]
    eye = (lax.broadcasted_iota(I32, (c, c), 0) == lax.broadcasted_iota(I32, (c, c), 1))

    group = range(HGRN_GROUP)
    chunk = lambda x, n: x[n * c:(n + 1) * c]
    n_levels = len(_HGRN_LEVELS)

    def local(gi, carry):
        rows = pl.ds(pl.multiple_of(gi * (HGRN_GROUP * c), HGRN_GROUP * c), HGRN_GROUP * c)
        qv, fv, iv = q_ref[rows, :], f_ref[rows, :], i_ref[rows, :]
        qs = qv * _sigmoid(qv)
        fg = lb + (1.0 - lb) * _sigmoid(fv)
        log_f = jnp.log(fg)
        kk = 1.0 - fg
        iv_b = iv.astype(BF16)
        hi = log_f.astype(BF16)
        rest = log_f - hi.astype(F32)
        mid = rest.astype(BF16)
        lo = (rest - mid.astype(F32)).astype(BF16)
        zero = jnp.zeros((c, dk), BF16)
        parts4 = jnp.concatenate([jnp.concatenate([chunk(t, n) for t in (hi, mid, lo)] + [zero], axis=0)
                                  for n in group], axis=1)
        parts2 = jnp.concatenate([jnp.concatenate([chunk(t, n) for t in (hi, mid)], axis=0)
                                  for n in group], axis=1)
        s4 = jnp.dot(cumsum_ref[...], parts4, preferred_element_type=F32)
        s2 = jnp.dot(sums_ref[...], parts2, preferred_element_type=F32)
        gcum = jnp.concatenate([s4[:, n * dk:(n + 1) * dk] for n in group], axis=0)
        level_sums = [s2[:, n * dk:(n + 1) * dk] for n in group]
        qg_ref[rows, :] = (qs * jnp.exp(gcum)).astype(BF16)
        qe, ke = [], []
        for n in group:
            for li in range(n_levels):
                is_up = upper_ref[li] > 0.0
                if _HGRN_LEVELS[li] == 1:
                    dec = jnp.exp(jnp.where(is_up, chunk(log_f, n), 0.0))
                else:
                    dec = jnp.exp(level_sums[n][li * c:(li + 1) * c, :])
                qe.append((chunk(qs, n) * dec).astype(BF16))
                ke.append((chunk(kk, n) * dec).astype(BF16))
        prods = [lax.dot_general(a, b, NT_DIMS, preferred_element_type=F32) for a, b in zip(qe, ke)]
        diag = jnp.sum(qs * kk, axis=1, keepdims=True)
        a_all = []
        for n in group:
            a = jnp.where(eye, chunk(diag, n), 0.0)
            for li in range(n_levels):
                a = a + jnp.where(pair_ref[li] > 0.0, prods[n * n_levels + li], 0.0)
            a_all.append(a.astype(BF16))
        o_in = [jnp.dot(a_all[n], chunk(iv_b, n), preferred_element_type=F32) for n in group]
        oin_ref[rows, :] = jnp.concatenate(o_in, axis=0)
        g_last = [gcum[(n + 1) * c - 1:(n + 1) * c, :] for n in group]
        k_dec = [(chunk(kk, n) * jnp.exp(g_last[n] - chunk(gcum, n))).astype(BF16) for n in group]
        for n in group:
            u_ref[gi * HGRN_GROUP + n] = lax.dot_general(chunk(iv_b, n), k_dec[n], TN_DIMS,
                                                         preferred_element_type=F32)
        dl_ref[pl.ds(pl.multiple_of(gi * SUBLANES, SUBLANES), SUBLANES), :] = jnp.concatenate(
            [jnp.exp(g) for g in g_last] + [jnp.zeros((1, dk), F32)] * (SUBLANES - HGRN_GROUP), axis=0)
        return carry

    lax.fori_loop(0, n_chunks // HGRN_GROUP, local, 0)

    def recur(gi, state_t):
        rows = pl.ds(pl.multiple_of(gi * (HGRN_GROUP * c), HGRN_GROUP * c), HGRN_GROUP * c)
        decay = dl_ref[pl.ds(pl.multiple_of(gi * SUBLANES, SUBLANES), SUBLANES), :]
        states = []
        for n in group:
            states.append(state_t.astype(BF16))
            state_t = decay[n:n + 1, :] * state_t + u_ref[gi * HGRN_GROUP + n]
        qg = qg_ref[rows, :]
        o = oin_ref[rows, :] + jnp.concatenate(
            [lax.dot_general(chunk(qg, n), states[n], NT_DIMS, preferred_element_type=F32) for n in group], axis=0)
        gv = g_ref[rows, :]
        o = o * lax.rsqrt(jnp.mean(o * o, axis=1, keepdims=True) + RMS_EPS) * gnorm
        o_ref[rows, :] = (o * (gv * _sigmoid(gv))).astype(o_ref.dtype)
        return state_t

    lax.fori_loop(0, n_chunks // HGRN_GROUP, recur, jnp.zeros((HEAD_DIM, HEAD_DIM), F32))


def _hgrn_mix(proj, lb_raw, g_norm, *, layer, batch, seq):
    t, d4 = proj.shape
    d = d4 // 4
    n_heads = d // HEAD_DIM
    n_chunks = seq // HGRN_CHUNK
    assert seq % (HGRN_CHUNK * HGRN_GROUP) == 0 and HGRN_GROUP <= SUBLANES
    cumsum, sums, upper, pair = _hgrn_constants()
    n_layers = lb_raw.shape[0]
    part = lambda off: pl.BlockSpec((seq, HEAD_DIM), lambda b, h: (b, off + h))
    whole = lambda a: pl.BlockSpec(a.shape, lambda b, h: (0,) * a.ndim)
    return pl.pallas_call(
        functools.partial(_hgrn_kernel, layer=layer, n_chunks=n_chunks),
        grid=(batch, n_heads),
        in_specs=[part(0), part(n_heads), part(2 * n_heads), part(3 * n_heads),
                  pl.BlockSpec((n_layers, HEAD_DIM), lambda b, h: (0, h)),
                  pl.BlockSpec((1, HEAD_DIM), lambda b, h: (0, 0)),
                  whole(cumsum), whole(sums), whole(upper), whole(pair)],
        out_specs=pl.BlockSpec((seq, HEAD_DIM), lambda b, h: (b, h)),
        out_shape=jax.ShapeDtypeStruct((t, d), BF16),
        scratch_shapes=[pltpu.VMEM((seq, HEAD_DIM), BF16),
                        pltpu.VMEM((seq, HEAD_DIM), F32),
                        pltpu.VMEM((n_chunks, HEAD_DIM, HEAD_DIM), F32),
                        pltpu.VMEM((n_chunks // HGRN_GROUP * SUBLANES, HEAD_DIM), F32)],
        compiler_params=_params("parallel", "parallel"),
        name="hgrn_recurrence",
    )(proj, proj, proj, proj, lb_raw, g_norm.reshape(1, HEAD_DIM),
      jnp.asarray(cumsum, BF16), jnp.asarray(sums, BF16), jnp.asarray(upper), jnp.asarray(pair))


def _router_kernel(x_ref, w_ref, b_ref, ids_ref, wts_ref, counts_ref):
    g, eg = N_GROUPS, EXPERTS_PER_GROUP
    tm = x_ref.shape[0]

    @pl.when(pl.program_id(0) == 0)
    def _():
        counts_ref[...] = jnp.zeros_like(counts_ref)

    x, w = x_ref[...], w_ref[...]
    x_hi, w_hi = x.astype(BF16), w.astype(BF16)
    x_lo, w_lo = (x - x_hi.astype(F32)).astype(BF16), (w - w_hi.astype(F32)).astype(BF16)
    logits = ((jnp.dot(x_hi, w_hi, preferred_element_type=F32) + jnp.dot(x_hi, w_lo, preferred_element_type=F32))
              + jnp.dot(x_lo, w_hi, preferred_element_type=F32)) + b_ref[...]
    lane = lax.broadcasted_iota(I32, logits.shape, 1)

    def softmax_over(mask):
        z = jnp.where(mask, logits, -jnp.inf)
        ez = jnp.exp(z - jnp.max(z, axis=1, keepdims=True))
        return ez / jnp.sum(ez, axis=1, keepdims=True)

    def top1(p, mask):
        best = jnp.max(jnp.where(mask, p, -1.0), axis=1, keepdims=True)
        where = jnp.min(jnp.where(mask & (p == best), lane, LANES), axis=1, keepdims=True)
        return best, where

    is_group = lane < g
    g_w, g_idx = top1(softmax_over(is_group), is_group)
    lo = g + g_idx * eg
    in_group = (lane >= lo) & (lane < lo + eg)
    pe = softmax_over(in_group)
    w1, i1 = top1(pe, in_group)
    w2, i2 = top1(pe, in_group & (lane != i1))
    den = w1 + w2
    wt1, wt2 = g_w * (w1 / den), g_w * (w2 / den)
    e1, e2 = i1 - g, i2 - g
    hot1, hot2 = lane == e1, lane == e2
    hot = jnp.where(hot1 | hot2, 1.0, 0.0)
    earlier = lax.broadcasted_iota(I32, (tm, tm), 1) < lax.broadcasted_iota(I32, (tm, tm), 0)
    before = counts_ref[...] + jnp.dot(jnp.where(earlier, 1.0, 0.0).astype(BF16), hot.astype(BF16),
                                       preferred_element_type=F32)
    r1 = jnp.sum(jnp.where(hot1, before, 0.0), axis=1, keepdims=True).astype(I32)
    r2 = jnp.sum(jnp.where(hot2, before, 0.0), axis=1, keepdims=True).astype(I32)
    counts_ref[...] += jnp.sum(hot, axis=0, keepdims=True)
    ids_ref[...] = jnp.where(lane == 0, e1, jnp.where(lane == 1, e2, jnp.where(lane == 2, r1,
                                                                               jnp.where(lane == 3, r2, 0))))
    wts_ref[...] = jnp.where(lane == 0, wt1, jnp.where(lane == 1, wt2, 0.0))


def _router(x, w_r, b_r, *, tm):
    t, d = x.shape
    tm = min(tm, t)
    assert t % tm == 0
    row = lambda i: (i, 0)
    fixed = lambda i: (0, 0)
    return pl.pallas_call(
        _router_kernel,
        grid=(t // tm,),
        in_specs=[pl.BlockSpec((tm, d), row), pl.BlockSpec((d, LANES), fixed), pl.BlockSpec((1, LANES), fixed)],
        out_specs=[pl.BlockSpec((tm, LANES), row), pl.BlockSpec((tm, LANES), row), pl.BlockSpec((1, LANES), fixed)],
        out_shape=[jax.ShapeDtypeStruct((t, LANES), I32), jax.ShapeDtypeStruct((t, LANES), F32),
                   jax.ShapeDtypeStruct((1, LANES), F32)],
        compiler_params=_params("arbitrary"),
        name="moe_router",
    )(x, w_r, b_r)


def _expert_kernel(blk_expert_ref, row_src_ref, n_used_ref, x_hbm, wg_ref, wu_ref, wd_ref, y_ref,
                   xbuf, wg_b, wu_b, wd_b, in_sems):
    i = pl.program_id(0)
    n_blk = pl.num_programs(0)
    d = wg_ref.shape[0]
    n = d // LANES
    rows = xbuf.shape[1] // n
    n_in = xbuf.shape[0]
    n_used = n_used_ref[0]
    active = i < n_used

    def gather(block):
        s = block % n_in
        block = jnp.minimum(block, n_blk - 1)
        for r in range(rows):
            tok = pl.multiple_of(row_src_ref[block * rows + r] * n, n)
            pltpu.make_async_copy(x_hbm.at[pl.ds(tok, n), :], xbuf.at[s, pl.ds(r * n, n), :], in_sems.at[s]).start()

    def wait_gather(block):
        s = block % n_in
        pltpu.make_async_copy(x_hbm.at[pl.ds(0, rows * n), :], xbuf.at[s], in_sems.at[s]).wait()

    def drain(first_unused_block):
        wait_gather(first_unused_block)
        wait_gather(first_unused_block + 1)

    @pl.when(i == 0)
    def _():
        gather(0)
        gather(1)

    @pl.when(active)
    def _():
        wait_gather(i)

        @pl.when((i == 0) | (blk_expert_ref[i] != blk_expert_ref[jnp.maximum(i - 1, 0)]))
        def _():
            wg_b[...] = wg_ref[...].astype(BF16)
            wu_b[...] = wu_ref[...].astype(BF16)
            wd_b[...] = wd_ref[...].astype(BF16)

        xb = _load_slabs(xbuf.at[i % n_in], rows, d).astype(BF16)
        hg = jnp.dot(xb, wg_b[...], preferred_element_type=F32)
        hu = jnp.dot(xb, wu_b[...], preferred_element_type=F32)
        h = (hg * _sigmoid(hg)) * hu
        _store_slabs(y_ref, jnp.dot(h.astype(BF16), wd_b[...], preferred_element_type=F32))
        gather(i + 2)

        @pl.when(i == n_blk - 1)
        def _():
            drain(i + 1)

    @pl.when(i == n_used)
    def _():
        drain(i)

    @pl.when(jnp.logical_not(active))
    def _():
        y_ref[...] = jnp.zeros_like(y_ref)


def _expert_mlp(x_slabs, blk_expert, row_src, n_used, w_gate, w_up, w_down, layer):
    _, n_e, d, f = w_gate.shape
    n_blk = blk_expert.shape[0]
    rows = MOE_ROWS
    n = d // LANES
    by_expert = lambda i, be, rs, nu: (layer, be[i], 0, 0)
    grid_spec = pltpu.PrefetchScalarGridSpec(
        num_scalar_prefetch=3,
        grid=(n_blk,),
        in_specs=[pl.BlockSpec(memory_space=pl.ANY),
                  pl.BlockSpec((None, None, d, f), by_expert), pl.BlockSpec((None, None, d, f), by_expert),
                  pl.BlockSpec((None, None, f, d), by_expert)],
        out_specs=pl.BlockSpec((rows * n, LANES), lambda i, be, rs, nu: (i, 0)),
        scratch_shapes=[pltpu.VMEM((3, rows * n, LANES), F32),
                        pltpu.VMEM((d, f), BF16), pltpu.VMEM((d, f), BF16), pltpu.VMEM((f, d), BF16),
                        pltpu.SemaphoreType.DMA((3,))],
    )
    return pl.pallas_call(
        _expert_kernel,
        grid_spec=grid_spec,
        out_shape=jax.ShapeDtypeStruct((n_blk * rows * n, LANES), F32),
        compiler_params=_params("arbitrary"),
        name="moe_experts",
    )(blk_expert, row_src, n_used, x_slabs, w_gate, w_up, w_down)


def _combine_ln_kernel(pos_ref, y_hbm, x_ref, wts_ref, g_ref, b_ref, xo_ref, xb_ref, ybuf, sems, *, alpha):
    i = pl.program_id(0)
    n_steps = pl.num_programs(0)
    tm, d = x_ref.shape
    n = d // LANES
    n_buf = ybuf.shape[0]

    def gather(step):
        s = step % n_buf
        step = jnp.minimum(step, n_steps - 1)
        for r in range(tm):
            for k in range(TOP_E):
                p = pl.multiple_of(pos_ref[(step * tm + r) * TOP_E + k] * n, n)
                pltpu.make_async_copy(y_hbm.at[pl.ds(p, n), :], ybuf.at[s, k, pl.ds(r * n, n), :], sems.at[s]).start()

    def wait_gather(step):
        s = step % n_buf
        for k in range(TOP_E):
            pltpu.make_async_copy(y_hbm.at[pl.ds(0, tm * n), :], ybuf.at[s, k], sems.at[s]).wait()

    @pl.when(i == 0)
    def _():
        gather(0)
        gather(1)

    wait_gather(i)
    slot = i % n_buf

    def rows_chunk(c, carry):
        rows = pl.ds(pl.multiple_of(c * COMBINE_ROWS, COMBINE_ROWS), COMBINE_ROWS)
        slabs = pl.ds(pl.multiple_of(c * (COMBINE_ROWS * n), COMBINE_ROWS * n), COMBINE_ROWS * n)
        wts = wts_ref[rows, :]
        m = (wts[:, 0:1] * _load_slabs(ybuf.at[slot, 0, slabs], COMBINE_ROWS, d)
             + wts[:, 1:2] * _load_slabs(ybuf.at[slot, 1, slabs], COMBINE_ROWS, d))
        out = _layer_norm(alpha * x_ref[rows, :] + m, g_ref[...], b_ref[...])
        xo_ref[rows, :] = out
        xb_ref[rows, :] = out.astype(BF16)
        return carry

    lax.fori_loop(0, tm // COMBINE_ROWS, rows_chunk, 0)
    gather(i + 2)

    @pl.when(i == n_steps - 1)
    def _():
        wait_gather(i + 1)
        wait_gather(i + 2)


def _combine_ln(pos, y_slabs, x, wts, g, b, *, alpha, tm):
    t, d = x.shape
    tm = min(tm, t)
    n = d // LANES
    assert t % tm == 0
    row = lambda i, p: (i, 0)
    fixed = lambda i, p: (0, 0)
    grid_spec = pltpu.PrefetchScalarGridSpec(
        num_scalar_prefetch=1,
        grid=(t // tm,),
        in_specs=[pl.BlockSpec(memory_space=pl.ANY), pl.BlockSpec((tm, d), row), pl.BlockSpec((tm, LANES), row),
                  pl.BlockSpec((1, d), fixed), pl.BlockSpec((1, d), fixed)],
        out_specs=[pl.BlockSpec((tm, d), row), pl.BlockSpec((tm, d), row)],
        scratch_shapes=[pltpu.VMEM((3, TOP_E, tm * n, LANES), F32), pltpu.SemaphoreType.DMA((3,))],
    )
    return pl.pallas_call(
        functools.partial(_combine_ln_kernel, alpha=alpha),
        grid_spec=grid_spec,
        out_shape=[jax.ShapeDtypeStruct((t, d), F32), jax.ShapeDtypeStruct((t, d), BF16)],
        compiler_params=_params("arbitrary"),
        name="moe_combine_ln",
    )(pos, y_slabs, x, wts, g.reshape(1, d), b.reshape(1, d))


def _dispatch_plan(ids, counts, n_tokens):
    rows = MOE_ROWS
    n_assign = n_tokens * TOP_E
    eid = ids[:, 0:TOP_E].reshape(n_assign)
    rank = ids[:, TOP_E:2 * TOP_E].reshape(n_assign)
    counts = counts[0, :N_EXPERTS].astype(I32)
    padded = (counts + rows - 1) // rows * rows
    pends = jnp.cumsum(padded)
    pstarts = pends - padded
    pos = jnp.take(pstarts, eid) + rank
    n_blk = n_assign // rows + N_EXPERTS
    row_src = jnp.zeros((n_blk * rows,), I32).at[pos].set(jnp.arange(n_assign, dtype=I32) // TOP_E)
    blk_first = jnp.arange(n_blk, dtype=I32) * rows
    blk_expert = jnp.minimum(jnp.sum((pends[None, :] <= blk_first[:, None]).astype(I32), axis=1), N_EXPERTS - 1)
    n_used = (pends[-1] // rows).astype(I32).reshape(1)
    return pos, row_src, blk_expert, n_used


def _moe_ln(x, x_slabs, w_r, b_r, w_gate, w_up, w_down, layer, g, b, *, alpha):
    ids, wts, counts = _router(x, w_r, b_r, tm=512)
    pos, row_src, blk_expert, n_used = _dispatch_plan(ids, counts, x.shape[0])
    y_slabs = _expert_mlp(x_slabs, blk_expert, row_src, n_used, w_gate, w_up, w_down, layer)
    return _combine_ln(pos, y_slabs, x, wts, g, b, alpha=alpha, tm=256)


def kernel(x, moba_w_in, moba_w_o, hgrn_w_in, hgrn_g_norm, hgrn_lb_raw, hgrn_w_o, ln_mix_g, ln_mix_b,
           moe_w_rg, moe_b_rg, moe_w_re, moe_b_re, moe_w_gate, moe_w_up, moe_w_down, ln_ffn_g, ln_ffn_b):
    batch, seq, d = x.shape
    depth = ln_mix_g.shape[0]
    alpha = (2 * depth) ** 0.25
    t = batch * seq
    xf = x.reshape(t, d)
    xb = xf.astype(BF16)
    n_re = N_GROUPS * EXPERTS_PER_GROUP
    moba_w_o_b, hgrn_w_o_b = moba_w_o.astype(BF16), hgrn_w_o.astype(BF16)
    for layer in range(depth):
        j = layer // 2
        if layer % 2 == 0:
            qkv = _matmul(xb, moba_w_in, j, BF16, tm=2048, tn=512, scaled_cols=d, scale=HEAD_DIM ** -0.5 * LOG2_E)
            o = _moba_attention(qkv, batch, seq)
            w_o = moba_w_o_b
        else:
            proj = _matmul(xb, hgrn_w_in, j, F32, tm=2048, tn=512)
            o = _hgrn_mix(proj, hgrn_lb_raw, hgrn_g_norm[j], layer=j, batch=batch, seq=seq)
            w_o = hgrn_w_o_b
        w_re = jnp.transpose(moe_w_re[layer], (1, 0, 2)).reshape(d, n_re)
        w_r = jnp.concatenate([moe_w_rg[layer], w_re, jnp.zeros((d, LANES - N_GROUPS - n_re), F32)], axis=1)
        b_r = jnp.concatenate([moe_b_rg[layer], moe_b_re[layer].reshape(n_re),
                               jnp.zeros((LANES - N_GROUPS - n_re,), F32)]).reshape(1, LANES)
        xf, xs = _proj_ln(o, w_o, j, xf, ln_mix_g[layer], ln_mix_b[layer], alpha=alpha, tm=256)
        xf, xb = _moe_ln(xf, xs, w_r, b_r, moe_w_gate, moe_w_up, moe_w_down, layer,
                         ln_ffn_g[layer], ln_ffn_b[layer], alpha=alpha)
    return xf.reshape(batch, seq, d)
```
